```python
import jax, jax.numpy as jnp
from jax import lax
import numpy as np

D_MODEL = 4096
BATCH = 1
SEQ = 8192
DEPTH = 1
DEC_BATCH = 8
DEC_SEQ = 2048
PAST_LEN = 128

N_META = 16
GRID_W = 64
HEAD_DIM = 128
N_Q_HEADS = 16
N_KV_HEADS = 4
Q_PER_KV = N_Q_HEADS // N_KV_HEADS
ATTN_WIDTH = N_Q_HEADS * HEAD_DIM
KV_WIDTH = N_KV_HEADS * HEAD_DIM
Q_BLOCK = 128
ROPE_THETA = 10000.0
RWKV_HEAD = 64
RWKV_WIDTH = D_MODEL // 2
RWKV_HEADS = RWKV_WIDTH // RWKV_HEAD
DECAY_LORA = 128
ICLR_LORA = 128
GATE_LORA = 480
N_BRANCHES = 2
N_EXPERTS = 16
CAPACITY_FACTOR = 2
EXPERT_FF = D_MODEL
NORM_EPS = 1e-6
GN_EPS = 64e-5
IN_SPLITS = (ATTN_WIDTH, KV_WIDTH, KV_WIDTH, RWKV_WIDTH, RWKV_WIDTH, RWKV_WIDTH, DECAY_LORA, ICLR_LORA, GATE_LORA, N_BRANCHES * D_MODEL)
IN_WIDTH = ATTN_WIDTH + 2 * KV_WIDTH + 3 * RWKV_WIDTH + DECAY_LORA + ICLR_LORA + GATE_LORA + N_BRANCHES * D_MODEL

kernel_name = 'hybrid_gqa_birwkv7_ec_moe_encoder'


def rmsnorm(x, g):
    xf = x.astype(jnp.float32)
    y = xf * lax.rsqrt(jnp.mean(xf * xf, axis=-1, keepdims=True) + NORM_EPS)
    return (y * g.astype(jnp.float32)).astype(x.dtype)


def split_columns(p, sizes):
    outs, start = [], 0
    for s in sizes:
        outs.append(p[..., start:start + s])
        start += s
    return outs


def axial_rope_angles(n_tokens):
    rows = n_tokens // GRID_W
    row_ids = jnp.repeat(jnp.arange(rows, dtype=jnp.float32), GRID_W)
    col_ids = jnp.tile(jnp.arange(GRID_W, dtype=jnp.float32), rows)
    half = HEAD_DIM // 2
    inv_freq = 1.0 / (ROPE_THETA ** (jnp.arange(0, half, 2, dtype=jnp.float32) / half))
    ang = jnp.concatenate([row_ids[:, None] * inv_freq, col_ids[:, None] * inv_freq], axis=-1)
    ang = jnp.concatenate([jnp.zeros((N_META, half), jnp.float32), ang], axis=0)
    return jnp.cos(ang), jnp.sin(ang)


def apply_rope(x, cos, sin):
    xf = x.astype(jnp.float32)
    half = HEAD_DIM // 2
    x1, x2 = xf[..., :half], xf[..., half:]
    c, s = cos[None, :, None, :], sin[None, :, None, :]
    return jnp.concatenate([x1 * c - x2 * s, x2 * c + x1 * s], axis=-1).astype(x.dtype)


def attend_block(qb, k, v):
    s = jnp.einsum('bqgrd,bkgd->bgrqk', qb, k, preferred_element_type=jnp.float32) * (HEAD_DIM ** -0.5)
    p = jax.nn.softmax(s, axis=-1)
    return jnp.einsum('bgrqk,bkgd->bqgrd', p.astype(v.dtype), v)


def gqa_attention(q, k, v):
    B, N = q.shape[0], q.shape[1]
    S = N - N_META
    qg = q.reshape(B, N, N_KV_HEADS, Q_PER_KV, HEAD_DIM)
    out_meta = attend_block(qg[:, :N_META], k, v)
    nblk = S // Q_BLOCK
    qr = qg[:, N_META:].reshape(B, nblk, Q_BLOCK, N_KV_HEADS, Q_PER_KV, HEAD_DIM).transpose(1, 0, 2, 3, 4, 5)
    out_real = lax.map(lambda qb: attend_block(qb, k, v), qr)
    out_real = out_real.transpose(1, 0, 2, 3, 4, 5).reshape(B, S, N_KV_HEADS, Q_PER_KV, HEAD_DIM)
    return jnp.concatenate([out_meta, out_real], axis=1).reshape(B, N, ATTN_WIDTH)


def to_heads(t):
    return t.reshape(t.shape[:-1] + (RWKV_HEADS, RWKV_HEAD))


def centred_shift(p, mu_prev, mu_next):
    prev = jnp.pad(p, ((0, 0), (1, 0), (0, 0)))[:, :-1]
    nxt = jnp.pad(p, ((0, 0), (0, 1), (0, 0)))[:, 1:]
    return p + mu_prev * (prev - p) + mu_next * (nxt - p)


def wkv7_scan(r, w, k, v, kk, a):
    B, T, H, N = r.shape
    def step(S, inp):
        r_t, w_t, k_t, v_t, kk_t, a_t = inp
        sa = jnp.einsum('bhij,bhj->bhi', S, -kk_t)
        S = S * w_t[:, :, None, :] + sa[..., :, None] * (kk_t * a_t)[..., None, :] + v_t[..., :, None] * k_t[..., None, :]
        return S, jnp.einsum('bhij,bhj->bhi', S, r_t)
    xs = tuple(jnp.moveaxis(t, 1, 0) for t in (r, w, k, v, kk, a))
    _, y = lax.scan(step, jnp.zeros((B, H, N, N), jnp.float32), xs)
    return jnp.moveaxis(y, 0, 1)


def rwkv_direction(r, k, v, decay_h, iclr_h, decay_up, decay_base, iclr_up, iclr_base, k_k, k_a, reverse):
    f32 = jnp.float32
    logw = -jax.nn.softplus(-(decay_base.astype(f32) + decay_h @ decay_up.astype(f32))) - 0.5
    w = jnp.exp(-jnp.exp(logw))
    a = jax.nn.sigmoid(iclr_base.astype(f32) + iclr_h @ iclr_up.astype(f32))
    kk = to_heads(k * k_k.astype(f32))
    kk = kk / jnp.maximum(jnp.sqrt(jnp.sum(kk * kk, axis=-1, keepdims=True)), 1e-12)
    kd = k * (1.0 + (a - 1.0) * k_a.astype(f32))
    args = [to_heads(t) for t in (r, w, kd, v)] + [kk, to_heads(a)]
    if reverse:
        args = [jnp.flip(t, axis=1) for t in args]
        return jnp.flip(wkv7_scan(*args), axis=1)
    return wkv7_scan(*args)


def rwkv7_mixer(r, k, v, decay_d, iclr_d, gate_d, shift_prev, shift_next, decay_up, decay_base, iclr_up, iclr_base,
                gate_up, k_k, k_a, r_k, ln_x_w, ln_x_b):
    f32 = jnp.float32
    sp, sn = shift_prev.astype(f32), shift_next.astype(f32)
    r = centred_shift(r.astype(f32), sp[0], sn[0])
    k = centred_shift(k.astype(f32), sp[1], sn[1])
    v = centred_shift(v.astype(f32), sp[2], sn[2])
    decay_h = jnp.tanh(decay_d.astype(f32))
    iclr_h = iclr_d.astype(f32)
    y = (rwkv_direction(r, k, v, decay_h, iclr_h, decay_up[0], decay_base[0], iclr_up[0], iclr_base[0], k_k, k_a, False)
         + rwkv_direction(r, k, v, decay_h, iclr_h, decay_up[1], decay_base[1], iclr_up[1], iclr_base[1], k_k, k_a, True))
    mu = jnp.mean(y, axis=-1, keepdims=True)
    var = jnp.mean(jnp.square(y - mu), axis=-1, keepdims=True)
    yn = ((y - mu) * lax.rsqrt(var + GN_EPS)).reshape(r.shape) * ln_x_w.astype(f32) + ln_x_b.astype(f32)
    bonus = (jnp.sum(to_heads(r * k * r_k.astype(f32)), axis=-1, keepdims=True) * to_heads(v)).reshape(r.shape)
    g = jax.nn.sigmoid(gate_d.astype(f32)) @ gate_up.astype(f32)
    return (yn + bonus) * g


def expert_choice_ffn(h, w_router, w_gate, w_up, w_down):
    B, N, D = h.shape
    T = B * N
    cap = CAPACITY_FACTOR * T // N_EXPERTS
    flat = h.reshape(T, D)
    aff = jax.nn.softmax((flat @ w_router).astype(jnp.float32), axis=-1)
    gate, idx = lax.top_k(aff.T, cap)
    xs = flat[idx]
    hid = jax.nn.silu(jnp.einsum('ecd,edf->ecf', xs, w_gate)) * jnp.einsum('ecd,edf->ecf', xs, w_up)
    out = jnp.einsum('ecf,efd->ecd', hid, w_down) * gate[..., None].astype(h.dtype)
    y = jnp.zeros_like(flat).at[idx.reshape(-1)].add(out.reshape(-1, D))
    return y.reshape(B, N, D)


def hybrid_layer(x, cos, sin, norm_mix, w_in, q_norm, k_norm, shift_prev, shift_next, decay_up, decay_base,
                 iclr_up, iclr_base, gate_up, k_k, k_a, r_k, ln_x_w, ln_x_b, w_branch_attn, w_branch_rwkv,
                 w_out, norm_ffn, w_router, w_gate, w_up, w_down):
    B, N, _ = x.shape
    h = rmsnorm(x, norm_mix)
    proj = h @ w_in
    q, k, v, r_w, k_w, v_w, decay_d, iclr_d, gate_d, merge_logits = split_columns(proj, IN_SPLITS)
    q = apply_rope(rmsnorm(q.reshape(B, N, N_Q_HEADS, HEAD_DIM), q_norm), cos, sin)
    k = apply_rope(rmsnorm(k.reshape(B, N, N_KV_HEADS, HEAD_DIM), k_norm), cos, sin)
    v = v.reshape(B, N, N_KV_HEADS, HEAD_DIM)
    attn = gqa_attention(q, k, v)
    rwkv = rwkv7_mixer(r_w, k_w, v_w, decay_d, iclr_d, gate_d, shift_prev, shift_next, decay_up, decay_base,
                       iclr_up, iclr_base, gate_up, k_k, k_a, r_k, ln_x_w, ln_x_b).astype(x.dtype)
    g_attn, g_rwkv = jnp.split(jax.nn.sigmoid(merge_logits.astype(jnp.float32)), 2, axis=-1)
    merged = g_attn * (attn @ w_branch_attn) + g_rwkv * (rwkv @ w_branch_rwkv)
    x = x + merged.astype(x.dtype) @ w_out
    x = x + expert_choice_ffn(rmsnorm(x, norm_ffn), w_router, w_gate, w_up, w_down)
    return x


def encoder_forward(x, meta_tokens, layer_params, norm_final):
    B, S = x.shape[0], x.shape[1]
    meta = jnp.broadcast_to(meta_tokens.astype(x.dtype)[None], (B, N_META, D_MODEL))
    h = jnp.concatenate([meta, x], axis=1)
    cos, sin = axial_rope_angles(S)
    for l in range(DEPTH):
        h = hybrid_layer(h, cos, sin, *[p[l] for p in layer_params])
    return rmsnorm(h, norm_final)[:, N_META:]


def setup_inputs(seed: int = 0) -> dict:
    key = jax.random.key(seed)
    ks = iter(jax.random.split(key, 40))
    nrm = lambda shape, scale: jax.random.normal(next(ks), shape, jnp.float32) * scale
    L, RW, E, F = DEPTH, RWKV_WIDTH, N_EXPERTS, EXPERT_FF
    return {
        'x_prompt': nrm((BATCH, SEQ, D_MODEL), 1.0),
        'x_sample': nrm((DEC_BATCH, DEC_SEQ, D_MODEL), 1.0),
        'meta_tokens': nrm((N_META, D_MODEL), 1.0),
        'norm_mix': 1.0 + nrm((L, D_MODEL), 0.02),
        'w_in': nrm((L, D_MODEL, IN_WIDTH), D_MODEL ** -0.5),
        'q_norm': 1.0 + nrm((L, HEAD_DIM), 0.02),
        'k_norm': 1.0 + nrm((L, HEAD_DIM), 0.02),
        'shift_prev': jax.random.uniform(next(ks), (L, 3, RW), jnp.float32, 0.0, 0.5),
        'shift_next': jax.random.uniform(next(ks), (L, 3, RW), jnp.float32, 0.0, 0.5),
        'decay_up': nrm((L, 2, DECAY_LORA, RW), 0.5 * DECAY_LORA ** -0.5),
        'decay_base': jax.random.uniform(next(ks), (L, 2, RW), jnp.float32, -6.0, -1.0),
        'iclr_up': nrm((L, 2, ICLR_LORA, RW), 0.5 * ICLR_LORA ** -0.5),
        'iclr_base': nrm((L, 2, RW), 0.5),
        'gate_up': nrm((L, GATE_LORA, RW), GATE_LORA ** -0.5),
        'k_k': 0.85 + nrm((L, RW), 0.05),
        'k_a': 1.0 + nrm((L, RW), 0.05),
        'r_k': nrm((L, RW), 0.1),
        'ln_x_w': 1.0 + nrm((L, RW), 0.02),
        'ln_x_b': nrm((L, RW), 0.02),
        'w_branch_attn': nrm((L, ATTN_WIDTH, D_MODEL), ATTN_WIDTH ** -0.5),
        'w_branch_rwkv': nrm((L, RW, D_MODEL), RW ** -0.5),
        'w_out': nrm((L, D_MODEL, D_MODEL), D_MODEL ** -0.5),
        'norm_ffn': 1.0 + nrm((L, D_MODEL), 0.02),
        'w_router': nrm((L, D_MODEL, E), D_MODEL ** -0.5),
        'w_gate': nrm((L, E, D_MODEL, F), D_MODEL ** -0.5),
        'w_up': nrm((L, E, D_MODEL, F), D_MODEL ** -0.5),
        'w_down': nrm((L, E, F, D_MODEL), F ** -0.5),
        'norm_final': 1.0 + nrm((D_MODEL,), 0.02),
    }


def reference(x_prompt, x_sample, meta_tokens, norm_mix, w_in, q_norm, k_norm, shift_prev, shift_next, decay_up,
              decay_base, iclr_up, iclr_base, gate_up, k_k, k_a, r_k, ln_x_w, ln_x_b, w_branch_attn, w_branch_rwkv,
              w_out, norm_ffn, w_router, w_gate, w_up, w_down, norm_final):
    layer_params = (norm_mix, w_in, q_norm, k_norm, shift_prev, shift_next, decay_up, decay_base, iclr_up,
                    iclr_base, gate_up, k_k, k_a, r_k, ln_x_w, ln_x_b, w_branch_attn, w_branch_rwkv, w_out,
                    norm_ffn, w_router, w_gate, w_up, w_down)
    y_prompt = encoder_forward(x_prompt, meta_tokens, layer_params, norm_final)
    y_sample = encoder_forward(x_sample, meta_tokens, layer_params, norm_final)
    return (y_prompt, y_sample)
```

```python
import functools
import math

import numpy as np
import jax
import jax.numpy as jnp
from jax import lax
from jax.experimental import pallas as pl
from jax.experimental.pallas import tpu as pltpu

F32 = jnp.float32
BF16 = jnp.bfloat16

N_META = 16
GRID_W = 64
HEAD_DIM = 128
N_Q_HEADS = 16
N_KV_HEADS = 4
Q_PER_KV = N_Q_HEADS // N_KV_HEADS
ATTN_WIDTH = N_Q_HEADS * HEAD_DIM
KV_WIDTH = N_KV_HEADS * HEAD_DIM
ROPE_THETA = 10000.0
RWKV_HEAD = 64
DECAY_LORA = 128
ICLR_LORA = 128
GATE_LORA = 480
N_EXPERTS = 16
CAPACITY_FACTOR = 2
NORM_EPS = 1e-6
GN_EPS = 64e-5

LANES = 128
FRONT = LANES - N_META
CHUNK = 64
PAIR = 2 * RWKV_HEAD
LORA_PAD = 768
GATE_PAD = LORA_PAD - DECAY_LORA - ICLR_LORA
VMEM_LIMIT = 56 * 1024 * 1024
NEG_BIG = -1e30


def _cparams(sem):
    return pltpu.CompilerParams(dimension_semantics=sem, vmem_limit_bytes=VMEM_LIMIT)


def _pick(n, cands):
    for c in cands:
        if n % c == 0:
            return c
    raise ValueError(f"no tile for {n} in {cands}")


def _split2(x):
    hi = x.astype(BF16)
    lo = (x - hi.astype(F32)).astype(BF16)
    return hi, lo


def _split3(x):
    hi = x.astype(BF16)
    r1 = x - hi.astype(F32)
    mid = r1.astype(BF16)
    lo = (r1 - mid.astype(F32)).astype(BF16)
    return hi, mid, lo


def _dot(a, b):
    return jnp.dot(a, b, preferred_element_type=F32)


def _dot_nt(a, b):
    return lax.dot_general(a, b, (((1,), (1,)), ((), ())), preferred_element_type=F32)


def _dot1(a, b):
    return _dot(a.astype(BF16), b.astype(BF16))


def _dot3(a, b):
    ah, al = _split2(a)
    bh, bl = _split2(b)
    return _dot(ah, bh) + (_dot(ah, bl) + _dot(al, bh))


def _dot_exact_lhs(a_bf16, b):
    hi, mid, lo = _split3(b)
    return _dot(a_bf16, hi) + (_dot(a_bf16, mid) + _dot(a_bf16, lo))


def _segsum(x, ones_bd):
    hi, mid, lo = _split3(x)
    return _dot(hi, ones_bd) + (_dot(mid, ones_bd) + _dot(lo, ones_bd))


def _sigmoid(x):
    return 1.0 / (1.0 + jnp.exp(-x))


def _rmsnorm_kernel(x_ref, g_ref, o_ref):
    x = x_ref[...]
    ms = jnp.mean(x * x, axis=-1, keepdims=True)
    o_ref[...] = (x * lax.rsqrt(ms + NORM_EPS) * g_ref[...]).astype(o_ref.dtype)


def _rmsnorm(x, g, tm, out_dtype):
    R, D = x.shape
    return pl.pallas_call(
        _rmsnorm_kernel,
        grid=(R // tm,),
        in_specs=[pl.BlockSpec((tm, D), lambda i: (i, 0)), pl.BlockSpec((1, D), lambda i: (0, 0))],
        out_specs=pl.BlockSpec((tm, D), lambda i: (i, 0)),
        out_shape=jax.ShapeDtypeStruct((R, D), out_dtype),
        compiler_params=_cparams(("parallel",)),
        name="rmsnorm",
    )(x, g.reshape(1, D))


def _mm_kernel(*refs, epilogue, n_extra):
    x_ref, w_ref = refs[0], refs[1]
    extras = refs[2:2 + n_extra]
    o_ref = refs[2 + n_extra]
    acc = _dot(x_ref[...], w_ref[...])
    if epilogue is not None:
        acc = epilogue(acc, *extras)
    o_ref[...] = acc.astype(o_ref.dtype)


def _matmul(x, w, *, tm, out_dtype, name, tn=None, epilogue=None, extras=()):
    M, K = x.shape
    N = w.shape[1]
    tn = tn or _pick(N, (512, 384, 256, 128))
    in_specs = [pl.BlockSpec((tm, K), lambda i, j: (i, 0)), pl.BlockSpec((K, tn), lambda i, j: (0, j))]
    in_specs += [spec for _, spec in extras]
    return pl.pallas_call(
        functools.partial(_mm_kernel, epilogue=epilogue, n_extra=len(extras)),
        grid=(M // tm, N // tn),
        in_specs=in_specs,
        out_specs=pl.BlockSpec((tm, tn), lambda i, j: (i, j)),
        out_shape=jax.ShapeDtypeStruct((M, N), out_dtype),
        compiler_params=_cparams(("parallel", "parallel")),
        name=name,
    )(x, w, *[a for a, _ in extras])


def _qk_epilogue(acc, g_ref, c_ref, s_ref, *, scale):
    g = g_ref[...]
    c = c_ref[...]
    s = s_ref[...]
    outs = []
    for h in range(acc.shape[1] // HEAD_DIM):
        y = acc[:, h * HEAD_DIM:(h + 1) * HEAD_DIM]
        y = y * lax.rsqrt(jnp.mean(y * y, axis=-1, keepdims=True) + NORM_EPS) * g
        y = y * c + pltpu.roll(y, HEAD_DIM // 2, 1) * s
        outs.append(y * scale if scale != 1.0 else y)
    return jnp.concatenate(outs, axis=1) if len(outs) > 1 else outs[0]


def _sigmoid_epilogue(acc):
    return _sigmoid(acc)


def _residual_epilogue(acc, x_ref):
    return x_ref[...] + acc


def _attn_kernel(q_ref, k_ref, v_ref, bias_ref, o_ref, m_sc, l_sc, acc_sc, *, tq, nk):
    j = pl.program_id(3)

    @pl.when(j == 0)
    def _():
        m_sc[...] = jnp.full(m_sc.shape, NEG_BIG, F32)
        l_sc[...] = jnp.zeros(l_sc.shape, F32)
        acc_sc[...] = jnp.zeros(acc_sc.shape, F32)

    q = q_ref[...]
    q4 = jnp.concatenate([q[:, h * HEAD_DIM:(h + 1) * HEAD_DIM] for h in range(Q_PER_KV)], axis=0)
    s = _dot_nt(q4, k_ref[...]) + bias_ref[...]
    m_prev = m_sc[...]
    m_new = jnp.maximum(m_prev, jnp.max(s, axis=-1, keepdims=True))
    alpha = jnp.exp2(m_prev - m_new)
    p = jnp.exp2(s - m_new)
    l_sc[...] = alpha * l_sc[...] + jnp.sum(p, axis=-1, keepdims=True)
    acc_sc[...] = alpha * acc_sc[...] + _dot(p.astype(BF16), v_ref[...])
    m_sc[...] = m_new

    @pl.when(j == nk - 1)
    def _():
        o = acc_sc[...] / l_sc[...]
        o_ref[...] = jnp.concatenate([o[h * tq:(h + 1) * tq] for h in range(Q_PER_KV)], axis=1).astype(o_ref.dtype)


def _attention(q, k, v, bias, B, Np):
    tq = _pick(Np, (640, 512, 384, 256, 128))
    tk = Np if Np <= 2304 else _pick(Np, (1024, 896, 768, 640, 512, 384, 256, 128))
    nq, nk = Np // tq, Np // tk
    return pl.pallas_call(
        functools.partial(_attn_kernel, tq=tq, nk=nk),
        grid=(B, N_KV_HEADS, nq, nk),
        in_specs=[
            pl.BlockSpec((tq, Q_PER_KV * HEAD_DIM), lambda b, g, i, j: (b * nq + i, g)),
            pl.BlockSpec((tk, HEAD_DIM), lambda b, g, i, j: (b * nk + j, g)),
            pl.BlockSpec((tk, HEAD_DIM), lambda b, g, i, j: (b * nk + j, g)),
            pl.BlockSpec((1, tk), lambda b, g, i, j: (0, j)),
        ],
        out_specs=pl.BlockSpec((tq, Q_PER_KV * HEAD_DIM), lambda b, g, i, j: (b * nq + i, g)),
        out_shape=jax.ShapeDtypeStruct((B * Np, ATTN_WIDTH), BF16),
        scratch_shapes=[
            pltpu.VMEM((Q_PER_KV * tq, 1), F32),
            pltpu.VMEM((Q_PER_KV * tq, 1), F32),
            pltpu.VMEM((Q_PER_KV * tq, HEAD_DIM), F32),
        ],
        compiler_params=_cparams(("parallel", "parallel", "parallel", "arbitrary")),
        name="attention",
    )(q, k, v, bias)


V_SP, V_SN, V_DBASE, V_IBASE, V_KK, V_KA, V_RK, V_LNW, V_LNB = 0, 3, 6, 8, 10, 11, 12, 13, 14
VEC_ROWS = 16
HALO = 8


def _prep_kernel(r_ref, k_ref, v_ref, rp_ref, rn_ref, kp_ref, kn_ref, vp_ref, vn_ref, lora_ref,
                 dup0_ref, dup1_ref, iup0_ref, iup1_ref, gup_ref, vec_ref, ones_ref,
                 r_o, v_o, kk_o, ew0_o, kd0_o, b0_o, ew1_o, kd1_o, b1_o, bonus_o, g_o, *, tm, n_row_tiles):
    i = pl.program_id(0)
    has_prev = (i > 0).astype(F32)
    has_next = (i < n_row_tiles - 1).astype(F32)
    vec = vec_ref[...]
    row = lax.broadcasted_iota(jnp.int32, (tm, 1), 0)

    def shifted(x_ref, p_ref, n_ref, idx):
        x = x_ref[...]
        prev = jnp.where(row == 0, p_ref[HALO - 1:HALO, :] * has_prev, pltpu.roll(x, 1, 0))
        nxt = jnp.where(row == tm - 1, n_ref[0:1, :] * has_next, pltpu.roll(x, tm - 1, 0))
        return x + vec[V_SP + idx:V_SP + idx + 1] * (prev - x) + vec[V_SN + idx:V_SN + idx + 1] * (nxt - x)

    r = shifted(r_ref, rp_ref, rn_ref, 0)
    k = shifted(k_ref, kp_ref, kn_ref, 1)
    v = shifted(v_ref, vp_ref, vn_ref, 2)
    ones = ones_ref[...]

    lora = lora_ref[...]
    decay_h = jnp.tanh(lora[:, 0:DECAY_LORA])
    iclr_h = lora[:, DECAY_LORA:DECAY_LORA + ICLR_LORA]
    gate_h = _sigmoid(lora[:, DECAY_LORA + ICLR_LORA:LORA_PAD])

    kkr = k * vec[V_KK:V_KK + 1]
    kk = kkr / jnp.maximum(jnp.sqrt(_segsum(kkr * kkr, ones)), 1e-12)
    r_o[...] = r
    v_o[...] = v
    kk_o[...] = kk
    bonus_o[...] = _segsum(r * k * vec[V_RK:V_RK + 1], ones) * v
    g_o[...] = _dot(gate_h.astype(BF16), gup_ref[...])

    for d, (dup_ref, iup_ref, ew_o, kd_o, b_o) in enumerate(
            ((dup0_ref, iup0_ref, ew0_o, kd0_o, b0_o), (dup1_ref, iup1_ref, ew1_o, kd1_o, b1_o))):
        u = -(vec[V_DBASE + d:V_DBASE + d + 1] + _dot3(decay_h, dup_ref[...]))
        softplus = jnp.maximum(u, 0.0) + jnp.log(1.0 + jnp.exp(-jnp.abs(u)))
        ew_o[...] = jnp.exp(-softplus - 0.5)
        a = _sigmoid(vec[V_IBASE + d:V_IBASE + d + 1] + _dot3(iclr_h, iup_ref[...]))
        kd_o[...] = k * (1.0 + (a - 1.0) * vec[V_KA:V_KA + 1])
        b_o[...] = kk * a


def _rwkv_prep(rkv, lora, dup, iup, gup, vec, ones_bd, tm, cn):
    R = rkv.shape[0]
    RW = rkv.shape[1] // 3
    J = RW // cn
    n_row_tiles = R // tm
    hb = tm // HALO
    last_hb = R // HALO - 1

    def main(c):
        return pl.BlockSpec((tm, cn), lambda i, j: (i, c * J + j))

    def prev(c):
        return pl.BlockSpec((HALO, cn), lambda i, j: (jnp.maximum(i * hb - 1, 0), c * J + j))

    def nxt(c):
        return pl.BlockSpec((HALO, cn), lambda i, j: (jnp.minimum((i + 1) * hb, last_hb), c * J + j))

    up = pl.BlockSpec((DECAY_LORA, cn), lambda i, j: (0, j))
    in_specs = [main(0), main(1), main(2), prev(0), nxt(0), prev(1), nxt(1), prev(2), nxt(2),
                pl.BlockSpec((tm, LORA_PAD), lambda i, j: (i, 0)),
                up, up, up, up,
                pl.BlockSpec((GATE_PAD, cn), lambda i, j: (0, j)),
                pl.BlockSpec((VEC_ROWS, cn), lambda i, j: (0, j)),
                pl.BlockSpec((cn, cn), lambda i, j: (0, 0))]
    out_spec = pl.BlockSpec((tm, cn), lambda i, j: (i, j))
    n_out = 11
    return pl.pallas_call(
        functools.partial(_prep_kernel, tm=tm, n_row_tiles=n_row_tiles),
        grid=(n_row_tiles, J),
        in_specs=in_specs,
        out_specs=[out_spec] * n_out,
        out_shape=[jax.ShapeDtypeStruct((R, RW), F32)] * n_out,
        compiler_params=_cparams(("parallel", "parallel")),
        name="rwkv_prep",
    )(rkv, rkv, rkv, rkv, rkv, rkv, rkv, rkv, rkv, lora, dup[0], dup[1], iup[0], iup[1], gup, vec, ones_bd)


def _wkv_kernel(first_ref, r_ref, v_ref, kk_ref, ew_ref, kd_ref, b_ref, tri_ref, y_ref, h_sc, *, reverse, npairs):
    c = pl.program_id(1)

    @pl.when(first_ref[c] == 1)
    def _():
        h_sc[...] = jnp.zeros(h_sc.shape, F32)

    L = CHUNK
    ew = ew_ref[...]
    cs = _dot_exact_lhs(tri_ref[...], ew)
    tot = cs[0:1] if reverse else cs[L - 1:L]
    e_neg = jnp.exp(-cs)
    e_prev = jnp.exp(ew - cs)
    e_pos = jnp.exp(cs)
    e_fin = jnp.exp(cs - tot)
    w_tot = jnp.exp(-tot)

    ri = lax.broadcasted_iota(jnp.int32, (PAIR, PAIR), 0)
    ci = lax.broadcasted_iota(jnp.int32, (PAIR, PAIR), 1)
    same = (ri // L) == (ci // L)
    t_i = ri % L
    s_i = ci % L
    if reverse:
        strict = same & (s_i > t_i)
        incl = same & (s_i >= t_i)
    else:
        strict = same & (s_i < t_i)
        incl = same & (s_i <= t_i)
    eye = ri == ci
    head0 = lax.broadcasted_iota(jnp.int32, (L, PAIR), 1) < RWKV_HEAD

    def stack(x):
        return jnp.concatenate([jnp.where(head0, x, 0.0), jnp.where(head0, 0.0, x)], axis=0)

    def dup(x):
        return jnp.concatenate([x, x], axis=0)

    for p in range(npairs):
        sl = slice(p * PAIR, (p + 1) * PAIR)
        r = r_ref[:, sl]
        v = v_ref[:, sl]
        kk = kk_ref[:, sl]
        kd = kd_ref[:, sl]
        b = b_ref[:, sl]
        r_st = stack(r * e_neg[:, sl])
        a_st = stack(-kk * e_prev[:, sl])
        v_st = stack(v)
        k2 = dup(kd * e_pos[:, sl])
        b2 = dup(b * e_pos[:, sl])
        kf_t = stack(kd * e_fin[:, sl]).T
        bf_t = stack(b * e_fin[:, sl]).T

        a_ab = jnp.where(strict, _dot_nt(a_st.astype(BF16), b2.astype(BF16)), 0.0)
        a_ak = jnp.where(strict, _dot_nt(a_st.astype(BF16), k2.astype(BF16)), 0.0)
        m_rb = jnp.where(incl, _dot_nt(r_st.astype(BF16), b2.astype(BF16)), 0.0)
        m_rk = jnp.where(incl, _dot_nt(r_st.astype(BF16), k2.astype(BF16)), 0.0)

        tinv = jnp.where(eye, 1.0, a_ab)
        npow = a_ab
        for _ in range(int(math.log2(L)) - 1):
            npow = _dot3(npow, npow)
            tinv = tinv + _dot3(npow, tinv)

        p_st = _dot3(tinv, a_st)
        q_st = _dot3(tinv, _dot1(a_ak, v_st))
        g_st = r_st + _dot1(m_rb, p_st)
        y0_st = _dot1(m_rk, v_st) + _dot1(m_rb, q_st)
        phi = jnp.where(eye, w_tot[:, sl], 0.0) + _dot1(bf_t, p_st)
        psi = _dot1(kf_t, v_st) + _dot1(bf_t, q_st)

        h = h_sc[p]
        y_st = _dot3(g_st, h) + y0_st
        h_sc[p] = _dot3(phi, h) + psi
        y_ref[:, sl] = y_st[0:L] + y_st[L:2 * L]


def _wkv(first, r, v, kk, ew, kd, b, tri, *, reverse, npairs):
    R, RW = r.shape
    NC = R // CHUNK
    width = npairs * PAIR
    if reverse:
        blk = pl.BlockSpec((CHUNK, width), lambda g, c, f: (NC - 1 - c, g))
    else:
        blk = pl.BlockSpec((CHUNK, width), lambda g, c, f: (c, g))
    grid_spec = pltpu.PrefetchScalarGridSpec(
        num_scalar_prefetch=1,
        grid=(RW // width, NC),
        in_specs=[blk] * 6 + [pl.BlockSpec((CHUNK, CHUNK), lambda g, c, f: (0, 0))],
        out_specs=blk,
        scratch_shapes=[pltpu.VMEM((npairs, PAIR, PAIR), F32)],
    )
    return pl.pallas_call(
        functools.partial(_wkv_kernel, reverse=reverse, npairs=npairs),
        grid_spec=grid_spec,
        out_shape=jax.ShapeDtypeStruct((R, RW), F32),
        compiler_params=_cparams(("parallel", "arbitrary")),
        name="wkv_rev" if reverse else "wkv_fwd",
    )(first, r, v, kk, ew, kd, b, tri)


def _post_kernel(yf_ref, yb_ref, bonus_ref, g_ref, vec_ref, ones_ref, o_ref):
    ones = ones_ref[...]
    vec = vec_ref[...]
    y = yf_ref[...] + yb_ref[...]
    inv_n = 1.0 / RWKV_HEAD
    mu = _segsum(y, ones) * inv_n
    d = y - mu
    var = _segsum(d * d, ones) * inv_n
    yn = d * lax.rsqrt(var + GN_EPS) * vec[V_LNW:V_LNW + 1] + vec[V_LNB:V_LNB + 1]
    o_ref[...] = ((yn + bonus_ref[...]) * g_ref[...]).astype(o_ref.dtype)


def _rwkv_post(yf, yb, bonus, g, vec, ones_bd, tm, cn):
    R, RW = yf.shape
    blk = pl.BlockSpec((tm, cn), lambda i, j: (i, j))
    return pl.pallas_call(
        _post_kernel,
        grid=(R // tm, RW // cn),
        in_specs=[blk, blk, blk, blk, pl.BlockSpec((VEC_ROWS, cn), lambda i, j: (0, j)),
                  pl.BlockSpec((cn, cn), lambda i, j: (0, 0))],
        out_specs=blk,
        out_shape=jax.ShapeDtypeStruct((R, RW), BF16),
        compiler_params=_cparams(("parallel", "parallel")),
        name="rwkv_post",
    )(yf, yb, bonus, g, vec, ones_bd)


def _merge_kernel(a_ref, r_ref, wa_ref, wr_ref, ga_ref, gr_ref, o_ref):
    ya = _dot(a_ref[...], wa_ref[...])
    yr = _dot(r_ref[...], wr_ref[...])
    o_ref[...] = (ga_ref[...].astype(F32) * ya + gr_ref[...].astype(F32) * yr).astype(o_ref.dtype)


def _merge(attn, rwkv, wa, wr, gates, tm, tn):
    R = attn.shape[0]
    D = wa.shape[1]
    J = D // tn
    return pl.pallas_call(
        _merge_kernel,
        grid=(R // tm, J),
        in_specs=[
            pl.BlockSpec((tm, attn.shape[1]), lambda i, j: (i, 0)),
            pl.BlockSpec((tm, rwkv.shape[1]), lambda i, j: (i, 0)),
            pl.BlockSpec((wa.shape[0], tn), lambda i, j: (0, j)),
            pl.BlockSpec((wr.shape[0], tn), lambda i, j: (0, j)),
            pl.BlockSpec((tm, tn), lambda i, j: (i, j)),
            pl.BlockSpec((tm, tn), lambda i, j: (i, J + j)),
        ],
        out_specs=pl.BlockSpec((tm, tn), lambda i, j: (i, j)),
        out_shape=jax.ShapeDtypeStruct((R, D), BF16),
        compiler_params=_cparams(("parallel", "parallel")),
        name="merge",
    )(attn, rwkv, wa, wr, gates, gates)


def _router_kernel(x_ref, g_ref, w_ref, h_ref, aff_ref):
    x = x_ref[...]
    ms = jnp.mean(x * x, axis=-1, keepdims=True)
    h = x * lax.rsqrt(ms + NORM_EPS) * g_ref[...]
    h_ref[...] = h.astype(h_ref.dtype)
    logits = _dot3(h, w_ref[...])
    lane = lax.broadcasted_iota(jnp.int32, logits.shape, 1)
    logits = jnp.where(lane < N_EXPERTS, logits, NEG_BIG)
    e = jnp.exp(logits - jnp.max(logits, axis=-1, keepdims=True))
    aff_ref[...] = e / jnp.sum(e, axis=-1, keepdims=True)


def _router(x1, g, w_router_pad, tm):
    R, D = x1.shape
    return pl.pallas_call(
        _router_kernel,
        grid=(R // tm,),
        in_specs=[pl.BlockSpec((tm, D), lambda i: (i, 0)), pl.BlockSpec((1, D), lambda i: (0, 0)),
                  pl.BlockSpec((D, LANES), lambda i: (0, 0))],
        out_specs=[pl.BlockSpec((tm, D), lambda i: (i, 0)), pl.BlockSpec((tm, LANES), lambda i: (i, 0))],
        out_shape=[jax.ShapeDtypeStruct((R, D), BF16), jax.ShapeDtypeStruct((R, LANES), F32)],
        compiler_params=_cparams(("parallel",)),
        name="ffn_norm_router",
    )(x1, g.reshape(1, D), w_router_pad)


def _ffn_up_kernel(x_ref, wg_ref, wu_ref, o_ref):
    x = x_ref[...]
    hg = _dot(x, wg_ref[...].astype(BF16))
    hu = _dot(x, wu_ref[...].astype(BF16))
    o_ref[...] = (hg * _sigmoid(hg) * hu).astype(o_ref.dtype)


def _ffn_down_kernel(h_ref, wd_ref, gate_ref, o_ref):
    o_ref[...] = (_dot(h_ref[...], wd_ref[...].astype(BF16)) * gate_ref[...]).astype(o_ref.dtype)


def _expert_ffn(xs, gate, w_gate, w_up, w_down, tm, tf):
    E, D, F = w_gate.shape
    nt = xs.shape[0] // (E * tm)
    wspec = pl.BlockSpec((None, D, tf), lambda e, i, f: (e, 0, f))
    hid = pl.pallas_call(
        _ffn_up_kernel,
        grid=(E, nt, F // tf),
        in_specs=[pl.BlockSpec((tm, D), lambda e, i, f: (e * nt + i, 0)), wspec, wspec],
        out_specs=pl.BlockSpec((tm, tf), lambda e, i, f: (e * nt + i, f)),
        out_shape=jax.ShapeDtypeStruct((xs.shape[0], F), BF16),
        compiler_params=_cparams(("parallel", "parallel", "parallel")),
        name="ffn_up",
    )(xs, w_gate, w_up)
    return pl.pallas_call(
        _ffn_down_kernel,
        grid=(E, nt, D // tf),
        in_specs=[pl.BlockSpec((tm, F), lambda e, i, f: (e * nt + i, 0)),
                  pl.BlockSpec((None, F, tf), lambda e, i, f: (e, 0, f)),
                  pl.BlockSpec((tm, 1), lambda e, i, f: (e * nt + i, 0))],
        out_specs=pl.BlockSpec((tm, tf), lambda e, i, f: (e * nt + i, f)),
        out_shape=jax.ShapeDtypeStruct((xs.shape[0], D), F32),
        compiler_params=_cparams(("parallel", "parallel", "parallel")),
        name="ffn_down",
    )(hid, w_down, gate)


def _final_kernel(x_ref, y_ref, g_ref, o_ref):
    x = x_ref[...] + y_ref[...]
    ms = jnp.mean(x * x, axis=-1, keepdims=True)
    o_ref[...] = x * lax.rsqrt(ms + NORM_EPS) * g_ref[...]


def _final(x1, y, g, tm):
    R, D = x1.shape
    blk = pl.BlockSpec((tm, D), lambda i: (i, 0))
    return pl.pallas_call(
        _final_kernel,
        grid=(R // tm,),
        in_specs=[blk, blk, pl.BlockSpec((1, D), lambda i: (0, 0))],
        out_specs=blk,
        out_shape=jax.ShapeDtypeStruct((R, D), F32),
        compiler_params=_cparams(("parallel",)),
        name="final_norm",
    )(x1, y, g.reshape(1, D))


def _pack_group(x, meta):
    B, S, D = x.shape
    z = jnp.zeros((B, FRONT, D), x.dtype)
    m = jnp.broadcast_to(meta.astype(x.dtype)[None], (B, N_META, D))
    return jnp.concatenate([z, m, x], axis=1).reshape(B * (S + LANES), D)


def _rope_tables(S, B):
    rows = S // GRID_W
    row_ids = jnp.repeat(jnp.arange(rows, dtype=F32), GRID_W)
    col_ids = jnp.tile(jnp.arange(GRID_W, dtype=F32), rows)
    half = HEAD_DIM // 2
    inv_freq = 1.0 / (ROPE_THETA ** (jnp.arange(0, half, 2, dtype=F32) / half))
    ang = jnp.concatenate([row_ids[:, None] * inv_freq, col_ids[:, None] * inv_freq], axis=-1)
    ang = jnp.concatenate([jnp.zeros((LANES, half), F32), ang], axis=0)
    c, s = jnp.cos(ang), jnp.sin(ang)
    return (jnp.tile(jnp.concatenate([c, c], axis=-1), (B, 1)),
            jnp.tile(jnp.concatenate([-s, s], axis=-1), (B, 1)))


def _key_bias(Np):
    pos = np.arange(Np)
    return jnp.asarray(np.where(pos < FRONT, NEG_BIG, 0.0).astype(np.float32)[None, :])


def _seq_flags(groups):
    fwd = []
    for B, Np in groups:
        nc = Np // CHUNK
        for _ in range(B):
            fwd += [1] + [0] * (nc - 1)
    fwd = np.asarray(fwd, np.int32)
    last = np.roll(fwd, -1)
    return jnp.asarray(fwd), jnp.asarray(last[::-1].copy())


def kernel(x_prompt, x_sample, meta_tokens, norm_mix, w_in, q_norm, k_norm, shift_prev, shift_next, decay_up, decay_base, iclr_up, iclr_base, gate_up, k_k, k_a, r_k, ln_x_w, ln_x_b, w_branch_attn, w_branch_rwkv, w_out, norm_ffn, w_router, w_gate, w_up, w_down, norm_final):
    assert norm_mix.shape[0] == 1, "one layer"
    D = x_prompt.shape[-1]
    RW = D // 2
    groups = [(x.shape[0], x.shape[1] + LANES) for x in (x_prompt, x_sample)]
    seqs = [x.shape[1] for x in (x_prompt, x_sample)]
    assert all(s % LANES == 0 for s in seqs)
    offs = [0, groups[0][0] * groups[0][1]]
    R = offs[1] + groups[1][0] * groups[1][1]
    tm = _pick(R, (384, 256, 128))

    x = jnp.concatenate([_pack_group(x_prompt, meta_tokens), _pack_group(x_sample, meta_tokens)], axis=0)
    tabs = [_rope_tables(S, B) for S, (B, _) in zip(seqs, groups)]
    cos_t = jnp.concatenate([t[0] for t in tabs], axis=0)
    sin_t = jnp.concatenate([t[1] for t in tabs], axis=0)

    w_in0 = w_in[0]
    c0 = 0
    cols = {}
    for name, width in (("q", ATTN_WIDTH), ("k", KV_WIDTH), ("v", KV_WIDTH), ("rkv", 3 * RW),
                        ("lora", DECAY_LORA + ICLR_LORA + GATE_LORA), ("merge", 2 * D)):
        cols[name] = w_in0[:, c0:c0 + width].astype(BF16)
        c0 += width
    w_lora = jnp.pad(cols["lora"], ((0, 0), (0, LORA_PAD - cols["lora"].shape[1])))

    h = _rmsnorm(x, norm_mix[0], tm, BF16)
    rope_specs = lambda g: [(g.reshape(1, HEAD_DIM), pl.BlockSpec((1, HEAD_DIM), lambda i, j: (0, 0))),
                            (cos_t, pl.BlockSpec((tm, HEAD_DIM), lambda i, j: (i, 0))),
                            (sin_t, pl.BlockSpec((tm, HEAD_DIM), lambda i, j: (i, 0)))]
    q_scale = HEAD_DIM ** -0.5 * math.log2(math.e)
    q = _matmul(h, cols["q"], tm=tm, out_dtype=BF16, name="proj_q",
                epilogue=functools.partial(_qk_epilogue, scale=q_scale), extras=rope_specs(q_norm[0]))
    k = _matmul(h, cols["k"], tm=tm, out_dtype=BF16, name="proj_k",
                epilogue=functools.partial(_qk_epilogue, scale=1.0), extras=rope_specs(k_norm[0]))
    v = _matmul(h, cols["v"], tm=tm, out_dtype=BF16, name="proj_v")
    rkv = _matmul(h, cols["rkv"], tm=tm, out_dtype=F32, name="proj_rkv")
    lora = _matmul(h, w_lora, tm=tm, tn=LORA_PAD, out_dtype=F32, name="proj_lora")
    gates = _matmul(h, cols["merge"], tm=tm, out_dtype=BF16, name="proj_merge_gates",
                    epilogue=_sigmoid_epilogue)

    attn_parts = []
    for (B, Np), off in zip(groups, offs):
        sl = slice(off, off + B * Np)
        attn_parts.append(_attention(q[sl], k[sl], v[sl], _key_bias(Np), B, Np))
    attn = jnp.concatenate(attn_parts, axis=0)

    cn = _pick(RW, (256, 128))
    vec = jnp.concatenate([shift_prev[0], shift_next[0], decay_base[0], iclr_base[0], k_k, k_a, r_k,
                           ln_x_w, ln_x_b, jnp.zeros((VEC_ROWS - 15, RW), F32)], axis=0).astype(F32)
    hid_idx = np.arange(cn) // RWKV_HEAD
    ones_bd = jnp.asarray((hid_idx[:, None] == hid_idx[None, :]).astype(np.float32)).astype(BF16)
    gup = jnp.pad(gate_up[0], ((0, GATE_PAD - GATE_LORA), (0, 0))).astype(BF16)
    r_s, v_s, kk, ew0, kd0, b0, ew1, kd1, b1, bonus, g_rwkv = _rwkv_prep(
        rkv, lora, decay_up[0], iclr_up[0], gup, vec, ones_bd, tm, cn)
    first_fwd, first_rev = _seq_flags(groups)
    t_idx = np.arange(CHUNK)
    tri_f = jnp.asarray((t_idx[None, :] <= t_idx[:, None]).astype(np.float32)).astype(BF16)
    tri_r = jnp.asarray((t_idx[None, :] >= t_idx[:, None]).astype(np.float32)).astype(BF16)
    npairs = _pick(RW // PAIR, (4, 2, 1))
    y_f = _wkv(first_fwd, r_s, v_s, kk, ew0, kd0, b0, tri_f, reverse=False, npairs=npairs)
    y_b = _wkv(first_rev, r_s, v_s, kk, ew1, kd1, b1, tri_r, reverse=True, npairs=npairs)
    rwkv = _rwkv_post(y_f, y_b, bonus, g_rwkv, vec, ones_bd, tm, cn)

    merged = _merge(attn, rwkv, w_branch_attn[0].astype(BF16), w_branch_rwkv[0].astype(BF16), gates, tm, _pick(D, (512, 256, 128)))
    x1 = _matmul(merged, w_out[0].astype(BF16), tm=tm, out_dtype=F32, name="out_proj",
                 epilogue=_residual_epilogue, extras=[(x, pl.BlockSpec((tm, _pick(D, (512, 384, 256, 128))), lambda i, j: (i, j)))])

    w_router_pad = jnp.pad(w_router[0], ((0, 0), (0, LANES - N_EXPERTS)))
    h2, aff = _router(x1, norm_ffn[0], w_router_pad, tm)
    aff = aff[:, :N_EXPERTS]
    idx_parts, gate_parts = [], []
    for (B, Np), off in zip(groups, offs):
        valid = jnp.asarray((np.arange(B * Np) % Np) >= FRONT)
        a_g = jnp.where(valid[:, None], aff[off:off + B * Np], -1.0)
        cap = CAPACITY_FACTOR * (B * (Np - FRONT)) // N_EXPERTS
        gate_g, idx_g = lax.top_k(a_g.T, cap)
        idx_parts.append(idx_g + off)
        gate_parts.append(gate_g)
    idx = jnp.concatenate(idx_parts, axis=1)
    gate = jnp.concatenate(gate_parts, axis=1)
    c_tot = idx.shape[1]
    nt = -(-c_tot // 1040)
    tme = -(-(-(-c_tot // nt)) // 16) * 16
    ct = nt * tme
    idx = jnp.pad(idx, ((0, 0), (0, ct - c_tot))).reshape(-1)
    gate = jnp.pad(gate, ((0, 0), (0, ct - c_tot))).reshape(-1, 1)
    xs = jnp.take(h2, idx, axis=0)
    tf = _pick(D, (256, 128))
    out = _expert_ffn(xs, gate, w_gate[0], w_up[0], w_down[0], tme, tf)
    y = jnp.zeros((R, D), F32).at[idx].add(out)

    z = _final(x1, y, norm_final, tm)
    outs = []
    for (B, Np), off in zip(groups, offs):
        outs.append(z[off:off + B * Np].reshape(B, Np, D)[:, LANES:])
    return tuple(outs)
```

```python
import functools
import math

import numpy as np
import jax
import jax.numpy as jnp
from jax import lax
from jax.experimental import pallas as pl
from jax.experimental.pallas import tpu as pltpu

F32 = jnp.float32
BF16 = jnp.bfloat16

N_META = 16
GRID_W = 64
HEAD_DIM = 128
N_Q_HEADS = 16
N_KV_HEADS = 4
Q_PER_KV = N_Q_HEADS // N_KV_HEADS
ATTN_WIDTH = N_Q_HEADS * HEAD_DIM
KV_WIDTH = N_KV_HEADS * HEAD_DIM
ROPE_THETA = 10000.0
RWKV_HEAD = 64
DECAY_LORA = 128
ICLR_LORA = 128
GATE_LORA = 480
N_EXPERTS = 16
CAPACITY_FACTOR = 2
NORM_EPS = 1e-6
GN_EPS = 64e-5

LANES = 128
FRONT = LANES - N_META
CHUNK = 64
PAIR = 2 * RWKV_HEAD
LORA_PAD = 768
GATE_PAD = LORA_PAD - DECAY_LORA - ICLR_LORA
VMEM_LIMIT = 56 * 1024 * 1024
NEG_BIG = -1e30


def _cparams(sem):
    return pltpu.CompilerParams(dimension_semantics=sem, vmem_limit_bytes=VMEM_LIMIT)


def _pick(n, cands):
    for c in cands:
        if n % c == 0:
            return c
    raise ValueError(f"no tile for {n} in {cands}")


def _split2(x):
    hi = x.astype(BF16)
    lo = (x - hi.astype(F32)).astype(BF16)
    return hi, lo


def _split3(x):
    hi = x.astype(BF16)
    r1 = x - hi.astype(F32)
    mid = r1.astype(BF16)
    lo = (r1 - mid.astype(F32)).astype(BF16)
    return hi, mid, lo


def _dot(a, b):
    return jnp.dot(a, b, preferred_element_type=F32)


def _dot_nt(a, b):
    return lax.dot_general(a, b, (((1,), (1,)), ((), ())), preferred_element_type=F32)


def _dot1(a, b):
    return _dot(a.astype(BF16), b.astype(BF16))


def _dot3(a, b):
    ah, al = _split2(a)
    bh, bl = _split2(b)
    return _dot(ah, bh) + (_dot(ah, bl) + _dot(al, bh))


def _dot_exact_lhs(a_bf16, b):
    hi, mid, lo = _split3(b)
    return _dot(a_bf16, hi) + (_dot(a_bf16, mid) + _dot(a_bf16, lo))


def _segsum(x, ones_bd):
    hi, mid, lo = _split3(x)
    return _dot(hi, ones_bd) + (_dot(mid, ones_bd) + _dot(lo, ones_bd))


def _sigmoid(x):
    return 1.0 / (1.0 + jnp.exp(-x))


def _rmsnorm_kernel(x_ref, g_ref, o_ref):
    x = x_ref[...]
    ms = jnp.mean(x * x, axis=-1, keepdims=True)
    o_ref[...] = (x * lax.rsqrt(ms + NORM_EPS) * g_ref[...]).astype(o_ref.dtype)


def _rmsnorm(x, g, tm, out_dtype):
    R, D = x.shape
    return pl.pallas_call(
        _rmsnorm_kernel,
        grid=(R // tm,),
        in_specs=[pl.BlockSpec((tm, D), lambda i: (i, 0)), pl.BlockSpec((1, D), lambda i: (0, 0))],
        out_specs=pl.BlockSpec((tm, D), lambda i: (i, 0)),
        out_shape=jax.ShapeDtypeStruct((R, D), out_dtype),
        compiler_params=_cparams(("parallel",)),
        name="rmsnorm",
    )(x, g.reshape(1, D))


def _mm_kernel(*refs, epilogue, n_extra):
    x_ref, w_ref = refs[0], refs[1]
    extras = refs[2:2 + n_extra]
    o_ref = refs[2 + n_extra]
    acc = _dot(x_ref[...], w_ref[...])
    if epilogue is not None:
        acc = epilogue(acc, *extras)
    o_ref[...] = acc.astype(o_ref.dtype)


def _matmul(x, w, *, tm, out_dtype, name, tn=None, epilogue=None, extras=()):
    M, K = x.shape
    N = w.shape[1]
    tn = tn or _pick(N, (512, 384, 256, 128))
    in_specs = [pl.BlockSpec((tm, K), lambda i, j: (i, 0)), pl.BlockSpec((K, tn), lambda i, j: (0, j))]
    in_specs += [spec for _, spec in extras]
    return pl.pallas_call(
        functools.partial(_mm_kernel, epilogue=epilogue, n_extra=len(extras)),
        grid=(M // tm, N // tn),
        in_specs=in_specs,
        out_specs=pl.BlockSpec((tm, tn), lambda i, j: (i, j)),
        out_shape=jax.ShapeDtypeStruct((M, N), out_dtype),
        compiler_params=_cparams(("parallel", "parallel")),
        name=name,
    )(x, w, *[a for a, _ in extras])


def _qk_epilogue(acc, g_ref, c_ref, s_ref, *, scale):
    g = g_ref[...]
    c = c_ref[...]
    s = s_ref[...]
    outs = []
    for h in range(acc.shape[1] // HEAD_DIM):
        y = acc[:, h * HEAD_DIM:(h + 1) * HEAD_DIM]
        y = y * lax.rsqrt(jnp.mean(y * y, axis=-1, keepdims=True) + NORM_EPS) * g
        y = y * c + pltpu.roll(y, HEAD_DIM // 2, 1) * s
        outs.append(y * scale if scale != 1.0 else y)
    return jnp.concatenate(outs, axis=1) if len(outs) > 1 else outs[0]


def _sigmoid_epilogue(acc):
    return _sigmoid(acc)


def _residual_epilogue(acc, x_ref):
    return x_ref[...] + acc


def _attn_kernel(q_ref, k_ref, v_ref, bias_ref, o_ref, m_sc, l_sc, acc_sc, *, tq, nk):
    j = pl.program_id(3)

    @pl.when(j == 0)
    def _():
        m_sc[...] = jnp.full(m_sc.shape, NEG_BIG, F32)
        l_sc[...] = jnp.zeros(l_sc.shape, F32)
        acc_sc[...] = jnp.zeros(acc_sc.shape, F32)

    q = q_ref[...]
    q4 = jnp.concatenate([q[:, h * HEAD_DIM:(h + 1) * HEAD_DIM] for h in range(Q_PER_KV)], axis=0)
    s = _dot_nt(q4, k_ref[...]) + bias_ref[...]
    m_prev = m_sc[...]
    m_new = jnp.maximum(m_prev, jnp.max(s, axis=-1, keepdims=True))
    alpha = jnp.exp2(m_prev - m_new)
    p = jnp.exp2(s - m_new)
    l_sc[...] = alpha * l_sc[...] + jnp.sum(p, axis=-1, keepdims=True)
    acc_sc[...] = alpha * acc_sc[...] + _dot(p.astype(BF16), v_ref[...])
    m_sc[...] = m_new

    @pl.when(j == nk - 1)
    def _():
        o = acc_sc[...] / l_sc[...]
        o_ref[...] = jnp.concatenate([o[h * tq:(h + 1) * tq] for h in range(Q_PER_KV)], axis=1).astype(o_ref.dtype)


def _attention(q, k, v, bias, B, Np):
    tq = _pick(Np, (640, 512, 384, 256, 128))
    tk = Np if Np <= 2304 else _pick(Np, (1024, 896, 768, 640, 512, 384, 256, 128))
    nq, nk = Np // tq, Np // tk
    return pl.pallas_call(
        functools.partial(_attn_kernel, tq=tq, nk=nk),
        grid=(B, N_KV_HEADS, nq, nk),
        in_specs=[
            pl.BlockSpec((tq, Q_PER_KV * HEAD_DIM), lambda b, g, i, j: (b * nq + i, g)),
            pl.BlockSpec((tk, HEAD_DIM), lambda b, g, i, j: (b * nk + j, g)),
            pl.BlockSpec((tk, HEAD_DIM), lambda b, g, i, j: (b * nk + j, g)),
            pl.BlockSpec((1, tk), lambda b, g, i, j: (0, j)),
        ],
        out_specs=pl.BlockSpec((tq, Q_PER_KV * HEAD_DIM), lambda b, g, i, j: (b * nq + i, g)),
        out_shape=jax.ShapeDtypeStruct((B * Np, ATTN_WIDTH), BF16),
        scratch_shapes=[
            pltpu.VMEM((Q_PER_KV * tq, 1), F32),
            pltpu.VMEM((Q_PER_KV * tq, 1), F32),
            pltpu.VMEM((Q_PER_KV * tq, HEAD_DIM), F32),
        ],
        compiler_params=_cparams(("parallel", "parallel", "parallel", "arbitrary")),
        name="attention",
    )(q, k, v, bias)


V_SP, V_SN, V_DBASE, V_IBASE, V_KK, V_KA, V_RK, V_LNW, V_LNB = 0, 3, 6, 8, 10, 11, 12, 13, 14
VEC_ROWS = 16
HALO = 8


def _prep_kernel(r_ref, k_ref, v_ref, rp_ref, rn_ref, kp_ref, kn_ref, vp_ref, vn_ref, lora_ref,
                 dup0_ref, dup1_ref, iup0_ref, iup1_ref, gup_ref, vec_ref, ones_ref,
                 r_o, v_o, kk_o, ew0_o, kd0_o, b0_o, ew1_o, kd1_o, b1_o, bonus_o, g_o, *, tm, n_row_tiles):
    i = pl.program_id(0)
    has_prev = (i > 0).astype(F32)
    has_next = (i < n_row_tiles - 1).astype(F32)
    vec = vec_ref[...]
    row = lax.broadcasted_iota(jnp.int32, (tm, 1), 0)

    def shifted(x_ref, p_ref, n_ref, idx):
        x = x_ref[...]
        prev = jnp.where(row == 0, p_ref[HALO - 1:HALO, :] * has_prev, pltpu.roll(x, 1, 0))
        nxt = jnp.where(row == tm - 1, n_ref[0:1, :] * has_next, pltpu.roll(x, tm - 1, 0))
        return x + vec[V_SP + idx:V_SP + idx + 1] * (prev - x) + vec[V_SN + idx:V_SN + idx + 1] * (nxt - x)

    r = shifted(r_ref, rp_ref, rn_ref, 0)
    k = shifted(k_ref, kp_ref, kn_ref, 1)
    v = shifted(v_ref, vp_ref, vn_ref, 2)
    ones = ones_ref[...]

    lora = lora_ref[...]
    decay_h = jnp.tanh(lora[:, 0:DECAY_LORA])
    iclr_h = lora[:, DECAY_LORA:DECAY_LORA + ICLR_LORA]
    gate_h = _sigmoid(lora[:, DECAY_LORA + ICLR_LORA:LORA_PAD])

    kkr = k * vec[V_KK:V_KK + 1]
    kk = kkr / jnp.maximum(jnp.sqrt(_segsum(kkr * kkr, ones)), 1e-12)
    r_o[...] = r
    v_o[...] = v
    kk_o[...] = kk
    bonus_o[...] = _segsum(r * k * vec[V_RK:V_RK + 1], ones) * v
    g_o[...] = _dot(gate_h.astype(BF16), gup_ref[...])

    for d, (dup_ref, iup_ref, ew_o, kd_o, b_o) in enumerate(
            ((dup0_ref, iup0_ref, ew0_o, kd0_o, b0_o), (dup1_ref, iup1_ref, ew1_o, kd1_o, b1_o))):
        u = -(vec[V_DBASE + d:V_DBASE + d + 1] + _dot3(decay_h, dup_ref[...]))
        softplus = jnp.maximum(u, 0.0) + jnp.log(1.0 + jnp.exp(-jnp.abs(u)))
        ew_o[...] = jnp.exp(-softplus - 0.5)
        a = _sigmoid(vec[V_IBASE + d:V_IBASE + d + 1] + _dot3(iclr_h, iup_ref[...]))
        kd_o[...] = k * (1.0 + (a - 1.0) * vec[V_KA:V_KA + 1])
        b_o[...] = kk * a


def _rwkv_prep(rkv, lora, dup, iup, gup, vec, ones_bd, tm, cn):
    R = rkv.shape[0]
    RW = rkv.shape[1] // 3
    J = RW // cn
    n_row_tiles = R // tm
    hb = tm // HALO
    last_hb = R // HALO - 1

    def main(c):
        return pl.BlockSpec((tm, cn), lambda i, j: (i, c * J + j))

    def prev(c):
        return pl.BlockSpec((HALO, cn), lambda i, j: (jnp.maximum(i * hb - 1, 0), c * J + j))

    def nxt(c):
        return pl.BlockSpec((HALO, cn), lambda i, j: (jnp.minimum((i + 1) * hb, last_hb), c * J + j))

    up = pl.BlockSpec((DECAY_LORA, cn), lambda i, j: (0, j))
    in_specs = [main(0), main(1), main(2), prev(0), nxt(0), prev(1), nxt(1), prev(2), nxt(2),
                pl.BlockSpec((tm, LORA_PAD), lambda i, j: (i, 0)),
                up, up, up, up,
                pl.BlockSpec((GATE_PAD, cn), lambda i, j: (0, j)),
                pl.BlockSpec((VEC_ROWS, cn), lambda i, j: (0, j)),
                pl.BlockSpec((cn, cn), lambda i, j: (0, 0))]
    out_spec = pl.BlockSpec((tm, cn), lambda i, j: (i, j))
    n_out = 11
    return pl.pallas_call(
        functools.partial(_prep_kernel, tm=tm, n_row_tiles=n_row_tiles),
        grid=(n_row_tiles, J),
        in_specs=in_specs,
        out_specs=[out_spec] * n_out,
        out_shape=[jax.ShapeDtypeStruct((R, RW), F32)] * n_out,
        compiler_params=_cparams(("parallel", "parallel")),
        name="rwkv_prep",
    )(rkv, rkv, rkv, rkv, rkv, rkv, rkv, rkv, rkv, lora, dup[0], dup[1], iup[0], iup[1], gup, vec, ones_bd)


def _wkv_kernel(first_ref, r_ref, v_ref, kk_ref, ew_ref, kd_ref, b_ref, tri_ref, y_ref, h_sc, *, reverse, npairs):
    c = pl.program_id(1)

    @pl.when(first_ref[c] == 1)
    def _():
        h_sc[...] = jnp.zeros(h_sc.shape, F32)

    L = CHUNK
    ew = ew_ref[...]
    cs = _dot_exact_lhs(tri_ref[...], ew)
    tot = cs[0:1] if reverse else cs[L - 1:L]
    e_neg = jnp.exp(-cs)
    e_prev = jnp.exp(ew - cs)
    e_pos = jnp.exp(cs)
    e_fin = jnp.exp(cs - tot)
    w_tot = jnp.exp(-tot)

    ri = lax.broadcasted_iota(jnp.int32, (PAIR, PAIR), 0)
    ci = lax.broadcasted_iota(jnp.int32, (PAIR, PAIR), 1)
    same = (ri // L) == (ci // L)
    t_i = ri % L
    s_i = ci % L
    if reverse:
        strict = same & (s_i > t_i)
        incl = same & (s_i >= t_i)
    else:
        strict = same & (s_i < t_i)
        incl = same & (s_i <= t_i)
    eye = ri == ci
    head0 = lax.broadcasted_iota(jnp.int32, (L, PAIR), 1) < RWKV_HEAD

    def stack(x):
        return jnp.concatenate([jnp.where(head0, x, 0.0), jnp.where(head0, 0.0, x)], axis=0)

    P2 = 2 * PAIR
    pairs = range(npairs)
    sls = [slice(p * PAIR, (p + 1) * PAIR) for p in pairs]

    r_st, a_st, v_st, bk2, kf_t, bf_t = [], [], [], [], [], []
    for sl in sls:
        kd = kd_ref[:, sl]
        b = b_ref[:, sl]
        bt = (b * e_pos[:, sl]).astype(BF16)
        kt = (kd * e_pos[:, sl]).astype(BF16)
        r_st.append(stack(r_ref[:, sl] * e_neg[:, sl]))
        a_st.append(stack(-kk_ref[:, sl] * e_prev[:, sl]))
        v_st.append(stack(v_ref[:, sl]).astype(BF16))
        bk2.append(jnp.concatenate([bt, bt, kt, kt], axis=0))
        kf_t.append(stack(kd * e_fin[:, sl]).T)
        bf_t.append(stack(b * e_fin[:, sl]).T)

    a_ab, m_rb, vlhs = [], [], []
    for p in pairs:
        sc = _dot_nt(jnp.concatenate([a_st[p], r_st[p]], axis=0).astype(BF16), bk2[p])
        a_ab.append(jnp.where(strict, sc[0:PAIR, 0:PAIR], 0.0))
        a_ak = jnp.where(strict, sc[0:PAIR, PAIR:P2], 0.0)
        m_rb.append(jnp.where(incl, sc[PAIR:P2, 0:PAIR], 0.0))
        m_rk = jnp.where(incl, sc[PAIR:P2, PAIR:P2], 0.0)
        vlhs.append(jnp.concatenate([a_ak, m_rk, kf_t[p]], axis=0).astype(BF16))

    vprod = [_dot(vlhs[p], v_st[p]) for p in pairs]

    x = [jnp.concatenate([a_st[p], vprod[p][0:PAIR]], axis=1) for p in pairs]
    npow = a_ab
    x = [x[p] + _dot1(npow[p], x[p]) for p in pairs]
    for _ in range(int(math.log2(L)) - 1):
        npow = [_dot1(npow[p], npow[p]) for p in pairs]
        x = [x[p] + _dot1(npow[p], x[p]) for p in pairs]

    for p in pairs:
        sl = sls[p]
        w = _dot1(jnp.concatenate([m_rb[p], bf_t[p]], axis=0), x[p])
        g_st = r_st[p] + w[0:PAIR, 0:PAIR]
        y0_st = vprod[p][PAIR:P2] + w[0:PAIR, PAIR:P2]
        phi = jnp.where(eye, w_tot[:, sl], 0.0) + w[PAIR:P2, 0:PAIR]
        psi = vprod[p][P2:P2 + PAIR] + w[PAIR:P2, PAIR:P2]
        out = _dot1(jnp.concatenate([g_st, phi], axis=0), h_sc[p])
        y_st = out[0:PAIR] + y0_st
        h_sc[p] = out[PAIR:P2] + psi
        y_ref[:, sl] = y_st[0:L] + y_st[L:2 * L]


def _wkv(first, r, v, kk, ew, kd, b, tri, *, reverse, npairs):
    R, RW = r.shape
    NC = R // CHUNK
    width = npairs * PAIR
    if reverse:
        blk = pl.BlockSpec((CHUNK, width), lambda g, c, f: (NC - 1 - c, g))
    else:
        blk = pl.BlockSpec((CHUNK, width), lambda g, c, f: (c, g))
    grid_spec = pltpu.PrefetchScalarGridSpec(
        num_scalar_prefetch=1,
        grid=(RW // width, NC),
        in_specs=[blk] * 6 + [pl.BlockSpec((CHUNK, CHUNK), lambda g, c, f: (0, 0))],
        out_specs=blk,
        scratch_shapes=[pltpu.VMEM((npairs, PAIR, PAIR), F32)],
    )
    return pl.pallas_call(
        functools.partial(_wkv_kernel, reverse=reverse, npairs=npairs),
        grid_spec=grid_spec,
        out_shape=jax.ShapeDtypeStruct((R, RW), F32),
        compiler_params=_cparams(("parallel", "arbitrary")),
        name="wkv_rev" if reverse else "wkv_fwd",
    )(first, r, v, kk, ew, kd, b, tri)


def _post_kernel(yf_ref, yb_ref, bonus_ref, g_ref, vec_ref, ones_ref, o_ref):
    ones = ones_ref[...]
    vec = vec_ref[...]
    y = yf_ref[...] + yb_ref[...]
    inv_n = 1.0 / RWKV_HEAD
    mu = _segsum(y, ones) * inv_n
    d = y - mu
    var = _segsum(d * d, ones) * inv_n
    yn = d * lax.rsqrt(var + GN_EPS) * vec[V_LNW:V_LNW + 1] + vec[V_LNB:V_LNB + 1]
    o_ref[...] = ((yn + bonus_ref[...]) * g_ref[...]).astype(o_ref.dtype)


def _rwkv_post(yf, yb, bonus, g, vec, ones_bd, tm, cn):
    R, RW = yf.shape
    blk = pl.BlockSpec((tm, cn), lambda i, j: (i, j))
    return pl.pallas_call(
        _post_kernel,
        grid=(R // tm, RW // cn),
        in_specs=[blk, blk, blk, blk, pl.BlockSpec((VEC_ROWS, cn), lambda i, j: (0, j)),
                  pl.BlockSpec((cn, cn), lambda i, j: (0, 0))],
        out_specs=blk,
        out_shape=jax.ShapeDtypeStruct((R, RW), BF16),
        compiler_params=_cparams(("parallel", "parallel")),
        name="rwkv_post",
    )(yf, yb, bonus, g, vec, ones_bd)


def _merge_kernel(a_ref, r_ref, wa_ref, wr_ref, ga_ref, gr_ref, o_ref):
    ya = _dot(a_ref[...], wa_ref[...])
    yr = _dot(r_ref[...], wr_ref[...])
    o_ref[...] = (ga_ref[...].astype(F32) * ya + gr_ref[...].astype(F32) * yr).astype(o_ref.dtype)


def _merge(attn, rwkv, wa, wr, gates, tm, tn):
    R = attn.shape[0]
    D = wa.shape[1]
    J = D // tn
    return pl.pallas_call(
        _merge_kernel,
        grid=(R // tm, J),
        in_specs=[
            pl.BlockSpec((tm, attn.shape[1]), lambda i, j: (i, 0)),
            pl.BlockSpec((tm, rwkv.shape[1]), lambda i, j: (i, 0)),
            pl.BlockSpec((wa.shape[0], tn), lambda i, j: (0, j)),
            pl.BlockSpec((wr.shape[0], tn), lambda i, j: (0, j)),
            pl.BlockSpec((tm, tn), lambda i, j: (i, j)),
            pl.BlockSpec((tm, tn), lambda i, j: (i, J + j)),
        ],
        out_specs=pl.BlockSpec((tm, tn), lambda i, j: (i, j)),
        out_shape=jax.ShapeDtypeStruct((R, D), BF16),
        compiler_params=_cparams(("parallel", "parallel")),
        name="merge",
    )(attn, rwkv, wa, wr, gates, gates)


def _router_kernel(x_ref, g_ref, w_ref, h_ref, aff_ref):
    x = x_ref[...]
    ms = jnp.mean(x * x, axis=-1, keepdims=True)
    h = x * lax.rsqrt(ms + NORM_EPS) * g_ref[...]
    h_ref[...] = h.astype(h_ref.dtype)
    logits = _dot3(h, w_ref[...])
    lane = lax.broadcasted_iota(jnp.int32, logits.shape, 1)
    logits = jnp.where(lane < N_EXPERTS, logits, NEG_BIG)
    e = jnp.exp(logits - jnp.max(logits, axis=-1, keepdims=True))
    aff_ref[...] = e / jnp.sum(e, axis=-1, keepdims=True)


def _router(x1, g, w_router_pad, tm):
    R, D = x1.shape
    return pl.pallas_call(
        _router_kernel,
        grid=(R // tm,),
        in_specs=[pl.BlockSpec((tm, D), lambda i: (i, 0)), pl.BlockSpec((1, D), lambda i: (0, 0)),
                  pl.BlockSpec((D, LANES), lambda i: (0, 0))],
        out_specs=[pl.BlockSpec((tm, D), lambda i: (i, 0)), pl.BlockSpec((tm, LANES), lambda i: (i, 0))],
        out_shape=[jax.ShapeDtypeStruct((R, D), BF16), jax.ShapeDtypeStruct((R, LANES), F32)],
        compiler_params=_cparams(("parallel",)),
        name="ffn_norm_router",
    )(x1, g.reshape(1, D), w_router_pad)


def _ffn_up_kernel(x_ref, wg_ref, wu_ref, o_ref):
    x = x_ref[...]
    hg = _dot(x, wg_ref[...].astype(BF16))
    hu = _dot(x, wu_ref[...].astype(BF16))
    o_ref[...] = (hg * _sigmoid(hg) * hu).astype(o_ref.dtype)


def _ffn_down_kernel(h_ref, wd_ref, gate_ref, o_ref):
    o_ref[...] = (_dot(h_ref[...], wd_ref[...].astype(BF16)) * gate_ref[...]).astype(o_ref.dtype)


def _expert_ffn(xs, gate, w_gate, w_up, w_down, tm, tf):
    E, D, F = w_gate.shape
    nt = xs.shape[0] // (E * tm)
    wspec = pl.BlockSpec((None, D, tf), lambda e, i, f: (e, 0, f))
    hid = pl.pallas_call(
        _ffn_up_kernel,
        grid=(E, nt, F // tf),
        in_specs=[pl.BlockSpec((tm, D), lambda e, i, f: (e * nt + i, 0)), wspec, wspec],
        out_specs=pl.BlockSpec((tm, tf), lambda e, i, f: (e * nt + i, f)),
        out_shape=jax.ShapeDtypeStruct((xs.shape[0], F), BF16),
        compiler_params=_cparams(("parallel", "parallel", "parallel")),
        name="ffn_up",
    )(xs, w_gate, w_up)
    return pl.pallas_call(
        _ffn_down_kernel,
        grid=(E, nt, D // tf),
        in_specs=[pl.BlockSpec((tm, F), lambda e, i, f: (e * nt + i, 0)),
                  pl.BlockSpec((None, F, tf), lambda e, i, f: (e, 0, f)),
                  pl.BlockSpec((tm, 1), lambda e, i, f: (e * nt + i, 0))],
        out_specs=pl.BlockSpec((tm, tf), lambda e, i, f: (e * nt + i, f)),
        out_shape=jax.ShapeDtypeStruct((xs.shape[0], D), F32),
        compiler_params=_cparams(("parallel", "parallel", "parallel")),
        name="ffn_down",
    )(hid, w_down, gate)


def _final_kernel(x_ref, y_ref, g_ref, o_ref):
    x = x_ref[...] + y_ref[...]
    ms = jnp.mean(x * x, axis=-1, keepdims=True)
    o_ref[...] = x * lax.rsqrt(ms + NORM_EPS) * g_ref[...]


def _final(x1, y, g, B, Np, off):
    D = x1.shape[1]
    nb = Np // LANES
    base = off // LANES
    blk = pl.BlockSpec((LANES, D), lambda b, j: (base + b * nb + 1 + j, 0))
    return pl.pallas_call(
        _final_kernel,
        grid=(B, nb - 1),
        in_specs=[blk, blk, pl.BlockSpec((1, D), lambda b, j: (0, 0))],
        out_specs=pl.BlockSpec((None, LANES, D), lambda b, j: (b, j, 0)),
        out_shape=jax.ShapeDtypeStruct((B, Np - LANES, D), F32),
        compiler_params=_cparams(("parallel", "parallel")),
        name="final_norm",
    )(x1, y, g.reshape(1, D))


def _pack_group(x, meta):
    B, S, D = x.shape
    z = jnp.zeros((B, FRONT, D), x.dtype)
    m = jnp.broadcast_to(meta.astype(x.dtype)[None], (B, N_META, D))
    return jnp.concatenate([z, m, x], axis=1).reshape(B * (S + LANES), D)


def _rope_tables(S, B):
    rows = S // GRID_W
    row_ids = jnp.repeat(jnp.arange(rows, dtype=F32), GRID_W)
    col_ids = jnp.tile(jnp.arange(GRID_W, dtype=F32), rows)
    half = HEAD_DIM // 2
    inv_freq = 1.0 / (ROPE_THETA ** (jnp.arange(0, half, 2, dtype=F32) / half))
    ang = jnp.concatenate([row_ids[:, None] * inv_freq, col_ids[:, None] * inv_freq], axis=-1)
    ang = jnp.concatenate([jnp.zeros((LANES, half), F32), ang], axis=0)
    c, s = jnp.cos(ang), jnp.sin(ang)
    return (jnp.tile(jnp.concatenate([c, c], axis=-1), (B, 1)),
            jnp.tile(jnp.concatenate([-s, s], axis=-1), (B, 1)))


def _key_bias(Np):
    pos = np.arange(Np)
    return jnp.asarray(np.where(pos < FRONT, NEG_BIG, 0.0).astype(np.float32)[None, :])


def _seq_flags(groups):
    fwd = []
    for B, Np in groups:
        nc = Np // CHUNK
        for _ in range(B):
            fwd += [1] + [0] * (nc - 1)
    fwd = np.asarray(fwd, np.int32)
    last = np.roll(fwd, -1)
    return jnp.asarray(fwd), jnp.asarray(last[::-1].copy())


def kernel(x_prompt, x_sample, meta_tokens, norm_mix, w_in, q_norm, k_norm, shift_prev, shift_next, decay_up, decay_base, iclr_up, iclr_base, gate_up, k_k, k_a, r_k, ln_x_w, ln_x_b, w_branch_attn, w_branch_rwkv, w_out, norm_ffn, w_router, w_gate, w_up, w_down, norm_final):
    assert norm_mix.shape[0] == 1, "one layer"
    D = x_prompt.shape[-1]
    RW = D // 2
    groups = [(x.shape[0], x.shape[1] + LANES) for x in (x_prompt, x_sample)]
    seqs = [x.shape[1] for x in (x_prompt, x_sample)]
    assert all(s % LANES == 0 for s in seqs)
    offs = [0, groups[0][0] * groups[0][1]]
    r_used = offs[1] + groups[1][0] * groups[1][1]
    tm = 768 if r_used >= 8 * 768 else LANES
    tme = tm // 3 if tm % 3 == 0 else tm
    tmr = tm // 2 if tm % 256 == 0 else tm
    R = -(-r_used // tm) * tm
    tail = R - r_used

    x = jnp.concatenate([_pack_group(x_prompt, meta_tokens), _pack_group(x_sample, meta_tokens),
                         jnp.zeros((tail, D), x_prompt.dtype)], axis=0)
    tabs = [_rope_tables(S, B) for S, (B, _) in zip(seqs, groups)]
    cos_t = jnp.concatenate([t[0] for t in tabs] + [jnp.zeros((tail, HEAD_DIM), F32)], axis=0)
    sin_t = jnp.concatenate([t[1] for t in tabs] + [jnp.zeros((tail, HEAD_DIM), F32)], axis=0)

    w_in0 = w_in[0]
    c0 = 0
    cols = {}
    for name, width in (("q", ATTN_WIDTH), ("k", KV_WIDTH), ("v", KV_WIDTH), ("rkv", 3 * RW),
                        ("lora", DECAY_LORA + ICLR_LORA + GATE_LORA), ("merge", 2 * D)):
        cols[name] = w_in0[:, c0:c0 + width].astype(BF16)
        c0 += width
    w_lora = jnp.pad(cols["lora"], ((0, 0), (0, LORA_PAD - cols["lora"].shape[1])))

    h = _rmsnorm(x, norm_mix[0], tme, BF16)
    rope_specs = lambda g: [(g.reshape(1, HEAD_DIM), pl.BlockSpec((1, HEAD_DIM), lambda i, j: (0, 0))),
                            (cos_t, pl.BlockSpec((tm, HEAD_DIM), lambda i, j: (i, 0))),
                            (sin_t, pl.BlockSpec((tm, HEAD_DIM), lambda i, j: (i, 0)))]
    q_scale = HEAD_DIM ** -0.5 * math.log2(math.e)
    q = _matmul(h, cols["q"], tm=tm, out_dtype=BF16, name="proj_q",
                epilogue=functools.partial(_qk_epilogue, scale=q_scale), extras=rope_specs(q_norm[0]))
    k = _matmul(h, cols["k"], tm=tm, out_dtype=BF16, name="proj_k",
                epilogue=functools.partial(_qk_epilogue, scale=1.0), extras=rope_specs(k_norm[0]))
    v = _matmul(h, cols["v"], tm=tm, out_dtype=BF16, name="proj_v")
    rkv = _matmul(h, cols["rkv"], tm=tm, out_dtype=F32, name="proj_rkv")
    lora = _matmul(h, w_lora, tm=tm, tn=LORA_PAD, out_dtype=F32, name="proj_lora")
    gates = _matmul(h, cols["merge"], tm=tm, out_dtype=BF16, name="proj_merge_gates",
                    epilogue=_sigmoid_epilogue)

    attn_parts = []
    for (B, Np), off in zip(groups, offs):
        sl = slice(off, off + B * Np)
        attn_parts.append(_attention(q[sl], k[sl], v[sl], _key_bias(Np), B, Np))
    attn = jnp.concatenate(attn_parts + [jnp.zeros((tail, ATTN_WIDTH), BF16)], axis=0)

    cn = _pick(RW, (256, 128))
    vec = jnp.concatenate([shift_prev[0], shift_next[0], decay_base[0], iclr_base[0], k_k, k_a, r_k,
                           ln_x_w, ln_x_b, jnp.zeros((VEC_ROWS - 15, RW), F32)], axis=0).astype(F32)
    hid_idx = np.arange(cn) // RWKV_HEAD
    ones_bd = jnp.asarray((hid_idx[:, None] == hid_idx[None, :]).astype(np.float32)).astype(BF16)
    gup = jnp.pad(gate_up[0], ((0, GATE_PAD - GATE_LORA), (0, 0))).astype(BF16)
    r_s, v_s, kk, ew0, kd0, b0, ew1, kd1, b1, bonus, g_rwkv = _rwkv_prep(
        rkv, lora, decay_up[0], iclr_up[0], gup, vec, ones_bd, tmr, cn)
    first_fwd, first_rev = _seq_flags(groups + ([(1, tail)] if tail else []))
    t_idx = np.arange(CHUNK)
    tri_f = jnp.asarray((t_idx[None, :] <= t_idx[:, None]).astype(np.float32)).astype(BF16)
    tri_r = jnp.asarray((t_idx[None, :] >= t_idx[:, None]).astype(np.float32)).astype(BF16)
    npairs = _pick(RW // PAIR, (8, 4, 2, 1))
    y_f = _wkv(first_fwd, r_s, v_s, kk, ew0, kd0, b0, tri_f, reverse=False, npairs=npairs)
    y_b = _wkv(first_rev, r_s, v_s, kk, ew1, kd1, b1, tri_r, reverse=True, npairs=npairs)
    rwkv = _rwkv_post(y_f, y_b, bonus, g_rwkv, vec, ones_bd, tmr, cn)

    merged = _merge(attn, rwkv, w_branch_attn[0].astype(BF16), w_branch_rwkv[0].astype(BF16), gates, tm, _pick(D, (512, 256, 128)))
    x1 = _matmul(merged, w_out[0].astype(BF16), tm=tm, out_dtype=F32, name="out_proj",
                 epilogue=_residual_epilogue, extras=[(x, pl.BlockSpec((tm, _pick(D, (512, 384, 256, 128))), lambda i, j: (i, j)))])

    w_router_pad = jnp.pad(w_router[0], ((0, 0), (0, LANES - N_EXPERTS)))
    h2, aff = _router(x1, norm_ffn[0], w_router_pad, tme)
    aff = aff[:, :N_EXPERTS]
    idx_parts, gate_parts = [], []
    for (B, Np), off in zip(groups, offs):
        valid = jnp.asarray((np.arange(B * Np) % Np) >= FRONT)
        a_g = jnp.where(valid[:, None], aff[off:off + B * Np], -1.0)
        cap = CAPACITY_FACTOR * (B * (Np - FRONT)) // N_EXPERTS
        gate_g, idx_g = lax.top_k(a_g.T, cap)
        idx_parts.append(idx_g + off)
        gate_parts.append(gate_g)
    idx = jnp.concatenate(idx_parts, axis=1)
    gate = jnp.concatenate(gate_parts, axis=1)
    c_tot = idx.shape[1]
    nt = -(-c_tot // 1040)
    tmx = -(-(-(-c_tot // nt)) // 16) * 16
    ct = nt * tmx
    idx = jnp.pad(idx, ((0, 0), (0, ct - c_tot))).reshape(-1)
    gate = jnp.pad(gate, ((0, 0), (0, ct - c_tot))).reshape(-1, 1)
    xs = jnp.take(h2, idx, axis=0)
    tf = _pick(D, (256, 128))
    out = _expert_ffn(xs, gate, w_gate[0], w_up[0], w_down[0], tmx, tf)
    y = jnp.zeros((R, D), F32).at[idx].add(out)

    return tuple(_final(x1, y, norm_final, B, Np, off) for (B, Np), off in zip(groups, offs))
```

```python
import functools
import math

import numpy as np
import jax
import jax.numpy as jnp
from jax import lax
from jax.experimental import pallas as pl
from jax.experimental.pallas import tpu as pltpu

F32 = jnp.float32
BF16 = jnp.bfloat16

N_META = 16
GRID_W = 64
HEAD_DIM = 128
N_Q_HEADS = 16
N_KV_HEADS = 4
Q_PER_KV = N_Q_HEADS // N_KV_HEADS
ATTN_WIDTH = N_Q_HEADS * HEAD_DIM
KV_WIDTH = N_KV_HEADS * HEAD_DIM
ROPE_THETA = 10000.0
RWKV_HEAD = 64
DECAY_LORA = 128
ICLR_LORA = 128
GATE_LORA = 480
N_EXPERTS = 16
CAPACITY_FACTOR = 2
NORM_EPS = 1e-6
GN_EPS = 64e-5

LANES = 128
FRONT = LANES - N_META
CHUNK = 64
PAIR = 2 * RWKV_HEAD
LORA_PAD = 768
GATE_PAD = LORA_PAD - DECAY_LORA - ICLR_LORA
VMEM_LIMIT = 56 * 1024 * 1024
NEG_BIG = -1e30


def _cparams(sem):
    return pltpu.CompilerParams(dimension_semantics=sem, vmem_limit_bytes=VMEM_LIMIT)


def _pick(n, cands):
    for c in cands:
        if n % c == 0:
            return c
    raise ValueError(f"no tile for {n} in {cands}")


def _split2(x):
    hi = x.astype(BF16)
    lo = (x - hi.astype(F32)).astype(BF16)
    return hi, lo


def _split3(x):
    hi = x.astype(BF16)
    r1 = x - hi.astype(F32)
    mid = r1.astype(BF16)
    lo = (r1 - mid.astype(F32)).astype(BF16)
    return hi, mid, lo


def _dot(a, b):
    return jnp.dot(a, b, preferred_element_type=F32)


def _dot_nt(a, b):
    return lax.dot_general(a, b, (((1,), (1,)), ((), ())), preferred_element_type=F32)


def _dot1(a, b):
    return _dot(a.astype(BF16), b.astype(BF16))


def _dot3(a, b):
    ah, al = _split2(a)
    bh, bl = _split2(b)
    return _dot(ah, bh) + (_dot(ah, bl) + _dot(al, bh))


def _dot_exact_lhs(a_bf16, b):
    hi, mid, lo = _split3(b)
    return _dot(a_bf16, hi) + (_dot(a_bf16, mid) + _dot(a_bf16, lo))


def _segsum(x, ones_bd):
    hi, mid, lo = _split3(x)
    return _dot(hi, ones_bd) + (_dot(mid, ones_bd) + _dot(lo, ones_bd))


def _sigmoid(x):
    return 1.0 / (1.0 + jnp.exp(-x))


def _rmsnorm_kernel(x_ref, g_ref, o_ref):
    x = x_ref[...]
    ms = jnp.mean(x * x, axis=-1, keepdims=True)
    o_ref[...] = (x * lax.rsqrt(ms + NORM_EPS) * g_ref[...]).astype(o_ref.dtype)


def _rmsnorm(x, g, tm, out_dtype):
    R, D = x.shape
    return pl.pallas_call(
        _rmsnorm_kernel,
        grid=(R // tm,),
        in_specs=[pl.BlockSpec((tm, D), lambda i: (i, 0)), pl.BlockSpec((1, D), lambda i: (0, 0))],
        out_specs=pl.BlockSpec((tm, D), lambda i: (i, 0)),
        out_shape=jax.ShapeDtypeStruct((R, D), out_dtype),
        compiler_params=_cparams(("parallel",)),
        name="rmsnorm",
    )(x, g.reshape(1, D))


def _mm_kernel(*refs, epilogue, n_extra):
    x_ref, w_ref = refs[0], refs[1]
    extras = refs[2:2 + n_extra]
    o_ref = refs[2 + n_extra]
    acc = _dot(x_ref[...], w_ref[...])
    if epilogue is not None:
        acc = epilogue(acc, *extras)
    o_ref[...] = acc.astype(o_ref.dtype)


def _matmul(x, w, *, tm, out_dtype, name, tn=None, epilogue=None, extras=()):
    M, K = x.shape
    N = w.shape[1]
    tn = tn or _pick(N, (512, 384, 256, 128))
    in_specs = [pl.BlockSpec((tm, K), lambda i, j: (i, 0)), pl.BlockSpec((K, tn), lambda i, j: (0, j))]
    in_specs += [spec for _, spec in extras]
    return pl.pallas_call(
        functools.partial(_mm_kernel, epilogue=epilogue, n_extra=len(extras)),
        grid=(M // tm, N // tn),
        in_specs=in_specs,
        out_specs=pl.BlockSpec((tm, tn), lambda i, j: (i, j)),
        out_shape=jax.ShapeDtypeStruct((M, N), out_dtype),
        compiler_params=_cparams(("parallel", "parallel")),
        name=name,
    )(x, w, *[a for a, _ in extras])


def _qk_epilogue(acc, g_ref, c_ref, s_ref, *, scale):
    g = g_ref[...]
    c = c_ref[...]
    s = s_ref[...]
    outs = []
    for h in range(acc.shape[1] // HEAD_DIM):
        y = acc[:, h * HEAD_DIM:(h + 1) * HEAD_DIM]
        y = y * lax.rsqrt(jnp.mean(y * y, axis=-1, keepdims=True) + NORM_EPS) * g
        y = y * c + pltpu.roll(y, HEAD_DIM // 2, 1) * s
        outs.append(y * scale if scale != 1.0 else y)
    return jnp.concatenate(outs, axis=1) if len(outs) > 1 else outs[0]


def _sigmoid_epilogue(acc):
    return _sigmoid(acc)


def _residual_epilogue(acc, x_ref):
    return x_ref[...] + acc


def _attn_kernel(q_ref, k_ref, v_ref, bias_ref, o_ref, m_sc, l_sc, acc_sc, s_sc, p_sc, *, tq, nk, rb):
    j = pl.program_id(3)
    M = Q_PER_KV * tq

    @pl.when(j == 0)
    def _():
        m_sc[...] = jnp.full(m_sc.shape, NEG_BIG, F32)
        l_sc[...] = jnp.zeros(l_sc.shape, F32)
        acc_sc[...] = jnp.zeros(acc_sc.shape, F32)

    q = q_ref[...]
    q4 = jnp.concatenate([q[:, h * HEAD_DIM:(h + 1) * HEAD_DIM] for h in range(Q_PER_KV)], axis=0)
    s_sc[...] = _dot_nt(q4, k_ref[...])

    def softmax_rows(use_bias):
        def body(i, carry):
            rows = pl.ds(pl.multiple_of(i * rb, rb), rb)
            s = s_sc[rows, :]
            if use_bias:
                s = s + bias_ref[...]
            m_prev = m_sc[rows, :]
            m_new = jnp.maximum(m_prev, jnp.max(s, axis=-1, keepdims=True))
            alpha = jnp.exp2(m_prev - m_new)
            p = jnp.exp2(s - m_new)
            l_sc[rows, :] = alpha * l_sc[rows, :] + jnp.sum(p, axis=-1, keepdims=True)
            m_sc[rows, :] = m_new
            acc_sc[rows, :] = acc_sc[rows, :] * alpha
            p_sc[rows, :] = p.astype(BF16)
            return carry
        lax.fori_loop(0, M // rb, body, 0, unroll=2)

    if nk == 1:
        softmax_rows(True)
    else:
        pl.when(j == 0)(functools.partial(softmax_rows, True))
        pl.when(j > 0)(functools.partial(softmax_rows, False))
    acc_sc[...] += _dot(p_sc[...], v_ref[...])

    @pl.when(j == nk - 1)
    def _():
        o = acc_sc[...] / l_sc[...]
        o_ref[...] = jnp.concatenate([o[h * tq:(h + 1) * tq] for h in range(Q_PER_KV)], axis=1).astype(o_ref.dtype)


SOFTMAX_BLOCK_VREGS = 24


def _attention(q, k, v, bias, B, Np):
    tq = _pick(Np, (640, 512, 384, 256, 128))
    tk = Np if Np <= 2304 else _pick(Np, (1024, 896, 768, 640, 512, 384, 256, 128))
    nq, nk = Np // tq, Np // tk
    rb = max(16, min(64, (SOFTMAX_BLOCK_VREGS * 8 * LANES // tk) // 16 * 16))
    return pl.pallas_call(
        functools.partial(_attn_kernel, tq=tq, nk=nk, rb=rb),
        grid=(B, N_KV_HEADS, nq, nk),
        in_specs=[
            pl.BlockSpec((tq, Q_PER_KV * HEAD_DIM), lambda b, g, i, j: (b * nq + i, g)),
            pl.BlockSpec((tk, HEAD_DIM), lambda b, g, i, j: (b * nk + j, g)),
            pl.BlockSpec((tk, HEAD_DIM), lambda b, g, i, j: (b * nk + j, g)),
            pl.BlockSpec((1, tk), lambda b, g, i, j: (0, j)),
        ],
        out_specs=pl.BlockSpec((tq, Q_PER_KV * HEAD_DIM), lambda b, g, i, j: (b * nq + i, g)),
        out_shape=jax.ShapeDtypeStruct((B * Np, ATTN_WIDTH), BF16),
        scratch_shapes=[
            pltpu.VMEM((Q_PER_KV * tq, 1), F32),
            pltpu.VMEM((Q_PER_KV * tq, 1), F32),
            pltpu.VMEM((Q_PER_KV * tq, HEAD_DIM), F32),
            pltpu.VMEM((Q_PER_KV * tq, tk), F32),
            pltpu.VMEM((Q_PER_KV * tq, tk), BF16),
        ],
        compiler_params=_cparams(("parallel", "parallel", "parallel", "arbitrary")),
        name="attention",
    )(q, k, v, bias)


V_SP, V_SN, V_DBASE, V_IBASE, V_KK, V_KA, V_RK, V_LNW, V_LNB = 0, 3, 6, 8, 10, 11, 12, 13, 14
VEC_ROWS = 16
HALO = 8


def _prep_kernel(r_ref, k_ref, v_ref, rp_ref, rn_ref, kp_ref, kn_ref, vp_ref, vn_ref, lora_ref,
                 dup0_ref, dup1_ref, iup0_ref, iup1_ref, gup_ref, vec_ref, ones_ref,
                 r_o, v_o, kk_o, ew0_o, kd0_o, b0_o, ew1_o, kd1_o, b1_o, bonus_o, g_o, *, tm, n_row_tiles):
    i = pl.program_id(0)
    has_prev = (i > 0).astype(F32)
    has_next = (i < n_row_tiles - 1).astype(F32)
    vec = vec_ref[...]
    row = lax.broadcasted_iota(jnp.int32, (tm, 1), 0)

    def shifted(x_ref, p_ref, n_ref, idx):
        x = x_ref[...]
        prev = jnp.where(row == 0, p_ref[HALO - 1:HALO, :] * has_prev, pltpu.roll(x, 1, 0))
        nxt = jnp.where(row == tm - 1, n_ref[0:1, :] * has_next, pltpu.roll(x, tm - 1, 0))
        return x + vec[V_SP + idx:V_SP + idx + 1] * (prev - x) + vec[V_SN + idx:V_SN + idx + 1] * (nxt - x)

    r = shifted(r_ref, rp_ref, rn_ref, 0)
    k = shifted(k_ref, kp_ref, kn_ref, 1)
    v = shifted(v_ref, vp_ref, vn_ref, 2)
    ones = ones_ref[...]

    lora = lora_ref[...]
    decay_h = jnp.tanh(lora[:, 0:DECAY_LORA])
    iclr_h = lora[:, DECAY_LORA:DECAY_LORA + ICLR_LORA]
    gate_h = _sigmoid(lora[:, DECAY_LORA + ICLR_LORA:LORA_PAD])

    kkr = k * vec[V_KK:V_KK + 1]
    kk = kkr / jnp.maximum(jnp.sqrt(_segsum(kkr * kkr, ones)), 1e-12)
    r_o[...] = r.astype(r_o.dtype)
    v_o[...] = v.astype(v_o.dtype)
    kk_o[...] = kk.astype(kk_o.dtype)
    bonus_o[...] = (_segsum(r * k * vec[V_RK:V_RK + 1], ones) * v).astype(bonus_o.dtype)
    g_o[...] = _dot(gate_h.astype(BF16), gup_ref[...]).astype(g_o.dtype)

    for d, (dup_ref, iup_ref, ew_o, kd_o, b_o) in enumerate(
            ((dup0_ref, iup0_ref, ew0_o, kd0_o, b0_o), (dup1_ref, iup1_ref, ew1_o, kd1_o, b1_o))):
        u = -(vec[V_DBASE + d:V_DBASE + d + 1] + _dot3(decay_h, dup_ref[...]))
        softplus = jnp.maximum(u, 0.0) + jnp.log(1.0 + jnp.exp(-jnp.abs(u)))
        ew_o[...] = jnp.exp(-softplus - 0.5)
        a = _sigmoid(vec[V_IBASE + d:V_IBASE + d + 1] + _dot3(iclr_h, iup_ref[...]))
        kd_o[...] = (k * (1.0 + (a - 1.0) * vec[V_KA:V_KA + 1])).astype(kd_o.dtype)
        b_o[...] = (kk * a).astype(b_o.dtype)


def _rwkv_prep(rkv, lora, dup, iup, gup, vec, ones_bd, tm, cn):
    R = rkv.shape[0]
    RW = rkv.shape[1] // 3
    J = RW // cn
    n_row_tiles = R // tm
    hb = tm // HALO
    last_hb = R // HALO - 1

    def main(c):
        return pl.BlockSpec((tm, cn), lambda i, j: (i, c * J + j))

    def prev(c):
        return pl.BlockSpec((HALO, cn), lambda i, j: (jnp.maximum(i * hb - 1, 0), c * J + j))

    def nxt(c):
        return pl.BlockSpec((HALO, cn), lambda i, j: (jnp.minimum((i + 1) * hb, last_hb), c * J + j))

    up = pl.BlockSpec((DECAY_LORA, cn), lambda i, j: (0, j))
    in_specs = [main(0), main(1), main(2), prev(0), nxt(0), prev(1), nxt(1), prev(2), nxt(2),
                pl.BlockSpec((tm, LORA_PAD), lambda i, j: (i, 0)),
                up, up, up, up,
                pl.BlockSpec((GATE_PAD, cn), lambda i, j: (0, j)),
                pl.BlockSpec((VEC_ROWS, cn), lambda i, j: (0, j)),
                pl.BlockSpec((cn, cn), lambda i, j: (0, 0))]
    out_spec = pl.BlockSpec((tm, cn), lambda i, j: (i, j))
    out_dtypes = [BF16, BF16, BF16, F32, BF16, BF16, F32, BF16, BF16, BF16, BF16]
    return pl.pallas_call(
        functools.partial(_prep_kernel, tm=tm, n_row_tiles=n_row_tiles),
        grid=(n_row_tiles, J),
        in_specs=in_specs,
        out_specs=[out_spec] * len(out_dtypes),
        out_shape=[jax.ShapeDtypeStruct((R, RW), dt) for dt in out_dtypes],
        compiler_params=_cparams(("parallel", "parallel")),
        name="rwkv_prep",
    )(rkv, rkv, rkv, rkv, rkv, rkv, rkv, rkv, rkv, lora, dup[0], dup[1], iup[0], iup[1], gup, vec, ones_bd)


def _wkv_kernel(first_ref, r_ref, v_ref, kk_ref, ew_ref, kd_ref, b_ref, tri_ref, y_ref, h_sc, *, reverse, npairs):
    c = pl.program_id(1)

    @pl.when(first_ref[c] == 1)
    def _():
        h_sc[...] = jnp.zeros(h_sc.shape, F32)

    L = CHUNK
    ew = ew_ref[...]
    cs = _dot_exact_lhs(tri_ref[...], ew)
    tot = cs[0:1] if reverse else cs[L - 1:L]
    e_neg = jnp.exp(-cs)
    e_prev = jnp.exp(ew - cs)
    e_pos = jnp.exp(cs)
    e_fin = jnp.exp(cs - tot)
    w_tot = jnp.exp(-tot)

    ri = lax.broadcasted_iota(jnp.int32, (PAIR, PAIR), 0)
    ci = lax.broadcasted_iota(jnp.int32, (PAIR, PAIR), 1)
    same = (ri // L) == (ci // L)
    t_i = ri % L
    s_i = ci % L
    if reverse:
        strict = same & (s_i > t_i)
        incl = same & (s_i >= t_i)
    else:
        strict = same & (s_i < t_i)
        incl = same & (s_i <= t_i)
    eye = ri == ci
    head0 = lax.broadcasted_iota(jnp.int32, (L, PAIR), 1) < RWKV_HEAD

    def stack(x):
        return jnp.concatenate([jnp.where(head0, x, 0.0), jnp.where(head0, 0.0, x)], axis=0)

    P2 = 2 * PAIR
    pairs = range(npairs)
    sls = [slice(p * PAIR, (p + 1) * PAIR) for p in pairs]

    r_st, a_st, v_st, bk2, kf_t, bf_t = [], [], [], [], [], []
    for sl in sls:
        kd = kd_ref[:, sl].astype(F32)
        b = b_ref[:, sl].astype(F32)
        bt = (b * e_pos[:, sl]).astype(BF16)
        kt = (kd * e_pos[:, sl]).astype(BF16)
        r_st.append(stack(r_ref[:, sl].astype(F32) * e_neg[:, sl]))
        a_st.append(stack(-kk_ref[:, sl].astype(F32) * e_prev[:, sl]))
        v_st.append(stack(v_ref[:, sl].astype(F32)).astype(BF16))
        bk2.append(jnp.concatenate([bt, bt, kt, kt], axis=0))
        kf_t.append(stack(kd * e_fin[:, sl]).T)
        bf_t.append(stack(b * e_fin[:, sl]).T)

    a_ab, m_rb, vlhs = [], [], []
    for p in pairs:
        sc = _dot_nt(jnp.concatenate([a_st[p], r_st[p]], axis=0).astype(BF16), bk2[p])
        a_ab.append(jnp.where(strict, sc[0:PAIR, 0:PAIR], 0.0))
        a_ak = jnp.where(strict, sc[0:PAIR, PAIR:P2], 0.0)
        m_rb.append(jnp.where(incl, sc[PAIR:P2, 0:PAIR], 0.0))
        m_rk = jnp.where(incl, sc[PAIR:P2, PAIR:P2], 0.0)
        vlhs.append(jnp.concatenate([a_ak, m_rk, kf_t[p]], axis=0).astype(BF16))

    vprod = [_dot(vlhs[p], v_st[p]) for p in pairs]

    x = [jnp.concatenate([a_st[p], vprod[p][0:PAIR]], axis=1) for p in pairs]
    npow = a_ab
    x = [x[p] + _dot1(npow[p], x[p]) for p in pairs]
    for _ in range(int(math.log2(L)) - 1):
        npow = [_dot1(npow[p], npow[p]) for p in pairs]
        x = [x[p] + _dot1(npow[p], x[p]) for p in pairs]

    for p in pairs:
        sl = sls[p]
        w = _dot1(jnp.concatenate([m_rb[p], bf_t[p]], axis=0), x[p])
        g_st = r_st[p] + w[0:PAIR, 0:PAIR]
        y0_st = vprod[p][PAIR:P2] + w[0:PAIR, PAIR:P2]
        phi = jnp.where(eye, w_tot[:, sl], 0.0) + w[PAIR:P2, 0:PAIR]
        psi = vprod[p][P2:P2 + PAIR] + w[PAIR:P2, PAIR:P2]
        out = _dot1(jnp.concatenate([g_st, phi], axis=0), h_sc[p])
        y_st = out[0:PAIR] + y0_st
        h_sc[p] = out[PAIR:P2] + psi
        y_ref[:, sl] = y_st[0:L] + y_st[L:2 * L]


def _wkv(first, r, v, kk, ew, kd, b, tri, *, reverse, npairs):
    R, RW = r.shape
    NC = R // CHUNK
    width = npairs * PAIR
    if reverse:
        blk = pl.BlockSpec((CHUNK, width), lambda g, c, f: (NC - 1 - c, g))
    else:
        blk = pl.BlockSpec((CHUNK, width), lambda g, c, f: (c, g))
    grid_spec = pltpu.PrefetchScalarGridSpec(
        num_scalar_prefetch=1,
        grid=(RW // width, NC),
        in_specs=[blk] * 6 + [pl.BlockSpec((CHUNK, CHUNK), lambda g, c, f: (0, 0))],
        out_specs=blk,
        scratch_shapes=[pltpu.VMEM((npairs, PAIR, PAIR), F32)],
    )
    return pl.pallas_call(
        functools.partial(_wkv_kernel, reverse=reverse, npairs=npairs),
        grid_spec=grid_spec,
        out_shape=jax.ShapeDtypeStruct((R, RW), F32),
        compiler_params=_cparams(("parallel", "arbitrary")),
        name="wkv_rev" if reverse else "wkv_fwd",
    )(first, r, v, kk, ew, kd, b, tri)


def _post_kernel(yf_ref, yb_ref, bonus_ref, g_ref, vec_ref, ones_ref, o_ref):
    ones = ones_ref[...]
    vec = vec_ref[...]
    y = yf_ref[...] + yb_ref[...]
    inv_n = 1.0 / RWKV_HEAD
    mu = _segsum(y, ones) * inv_n
    d = y - mu
    var = _segsum(d * d, ones) * inv_n
    yn = d * lax.rsqrt(var + GN_EPS) * vec[V_LNW:V_LNW + 1] + vec[V_LNB:V_LNB + 1]
    o_ref[...] = ((yn + bonus_ref[...]) * g_ref[...]).astype(o_ref.dtype)


def _rwkv_post(yf, yb, bonus, g, vec, ones_bd, tm, cn):
    R, RW = yf.shape
    blk = pl.BlockSpec((tm, cn), lambda i, j: (i, j))
    return pl.pallas_call(
        _post_kernel,
        grid=(R // tm, RW // cn),
        in_specs=[blk, blk, blk, blk, pl.BlockSpec((VEC_ROWS, cn), lambda i, j: (0, j)),
                  pl.BlockSpec((cn, cn), lambda i, j: (0, 0))],
        out_specs=blk,
        out_shape=jax.ShapeDtypeStruct((R, RW), BF16),
        compiler_params=_cparams(("parallel", "parallel")),
        name="rwkv_post",
    )(yf, yb, bonus, g, vec, ones_bd)


def _merge_kernel(a_ref, r_ref, wa_ref, wr_ref, ga_ref, gr_ref, o_ref):
    ya = _dot(a_ref[...], wa_ref[...])
    yr = _dot(r_ref[...], wr_ref[...])
    o_ref[...] = (ga_ref[...].astype(F32) * ya + gr_ref[...].astype(F32) * yr).astype(o_ref.dtype)


def _merge(attn, rwkv, wa, wr, gates, tm, tn):
    R = attn.shape[0]
    D = wa.shape[1]
    J = D // tn
    return pl.pallas_call(
        _merge_kernel,
        grid=(R // tm, J),
        in_specs=[
            pl.BlockSpec((tm, attn.shape[1]), lambda i, j: (i, 0)),
            pl.BlockSpec((tm, rwkv.shape[1]), lambda i, j: (i, 0)),
            pl.BlockSpec((wa.shape[0], tn), lambda i, j: (0, j)),
            pl.BlockSpec((wr.shape[0], tn), lambda i, j: (0, j)),
            pl.BlockSpec((tm, tn), lambda i, j: (i, j)),
            pl.BlockSpec((tm, tn), lambda i, j: (i, J + j)),
        ],
        out_specs=pl.BlockSpec((tm, tn), lambda i, j: (i, j)),
        out_shape=jax.ShapeDtypeStruct((R, D), BF16),
        compiler_params=_cparams(("parallel", "parallel")),
        name="merge",
    )(attn, rwkv, wa, wr, gates, gates)


def _router_kernel(x_ref, g_ref, w_ref, h_ref, aff_ref):
    x = x_ref[...]
    ms = jnp.mean(x * x, axis=-1, keepdims=True)
    h = x * lax.rsqrt(ms + NORM_EPS) * g_ref[...]
    h_ref[...] = h.astype(h_ref.dtype)
    logits = _dot3(h, w_ref[...])
    lane = lax.broadcasted_iota(jnp.int32, logits.shape, 1)
    logits = jnp.where(lane < N_EXPERTS, logits, NEG_BIG)
    e = jnp.exp(logits - jnp.max(logits, axis=-1, keepdims=True))
    aff_ref[...] = e / jnp.sum(e, axis=-1, keepdims=True)


def _router(x1, g, w_router_pad, tm):
    R, D = x1.shape
    return pl.pallas_call(
        _router_kernel,
        grid=(R // tm,),
        in_specs=[pl.BlockSpec((tm, D), lambda i: (i, 0)), pl.BlockSpec((1, D), lambda i: (0, 0)),
                  pl.BlockSpec((D, LANES), lambda i: (0, 0))],
        out_specs=[pl.BlockSpec((tm, D), lambda i: (i, 0)), pl.BlockSpec((tm, LANES), lambda i: (i, 0))],
        out_shape=[jax.ShapeDtypeStruct((R, D), BF16), jax.ShapeDtypeStruct((R, LANES), F32)],
        compiler_params=_cparams(("parallel",)),
        name="ffn_norm_router",
    )(x1, g.reshape(1, D), w_router_pad)


def _ffn_up_kernel(x_ref, wg_ref, wu_ref, o_ref):
    x = x_ref[...]
    hg = _dot(x, wg_ref[...].astype(BF16))
    hu = _dot(x, wu_ref[...].astype(BF16))
    o_ref[...] = (hg * _sigmoid(hg) * hu).astype(o_ref.dtype)


def _ffn_down_kernel(h_ref, wd_ref, gate_ref, o_ref):
    o_ref[...] = (_dot(h_ref[...], wd_ref[...].astype(BF16)) * gate_ref[...]).astype(o_ref.dtype)


def _expert_ffn(xs, gate, w_gate, w_up, w_down, tm, tf):
    E, D, F = w_gate.shape
    nt = xs.shape[0] // (E * tm)
    wspec = pl.BlockSpec((None, D, tf), lambda e, i, f: (e, 0, f))
    hid = pl.pallas_call(
        _ffn_up_kernel,
        grid=(E, nt, F // tf),
        in_specs=[pl.BlockSpec((tm, D), lambda e, i, f: (e * nt + i, 0)), wspec, wspec],
        out_specs=pl.BlockSpec((tm, tf), lambda e, i, f: (e * nt + i, f)),
        out_shape=jax.ShapeDtypeStruct((xs.shape[0], F), BF16),
        compiler_params=_cparams(("parallel", "parallel", "parallel")),
        name="ffn_up",
    )(xs, w_gate, w_up)
    return pl.pallas_call(
        _ffn_down_kernel,
        grid=(E, nt, D // tf),
        in_specs=[pl.BlockSpec((tm, F), lambda e, i, f: (e * nt + i, 0)),
                  pl.BlockSpec((None, F, tf), lambda e, i, f: (e, 0, f)),
                  pl.BlockSpec((tm, 1), lambda e, i, f: (e * nt + i, 0))],
        out_specs=pl.BlockSpec((tm, tf), lambda e, i, f: (e * nt + i, f)),
        out_shape=jax.ShapeDtypeStruct((xs.shape[0], D), F32),
        compiler_params=_cparams(("parallel", "parallel", "parallel")),
        name="ffn_down",
    )(hid, w_down, gate)


def _final_kernel(x_ref, y_ref, g_ref, o_ref):
    x = x_ref[...] + y_ref[...]
    ms = jnp.mean(x * x, axis=-1, keepdims=True)
    o_ref[...] = x * lax.rsqrt(ms + NORM_EPS) * g_ref[...]


def _final(x1, y, g, B, Np, off):
    D = x1.shape[1]
    nb = Np // LANES
    base = off // LANES
    blk = pl.BlockSpec((LANES, D), lambda b, j: (base + b * nb + 1 + j, 0))
    return pl.pallas_call(
        _final_kernel,
        grid=(B, nb - 1),
        in_specs=[blk, blk, pl.BlockSpec((1, D), lambda b, j: (0, 0))],
        out_specs=pl.BlockSpec((None, LANES, D), lambda b, j: (b, j, 0)),
        out_shape=jax.ShapeDtypeStruct((B, Np - LANES, D), F32),
        compiler_params=_cparams(("parallel", "parallel")),
        name="final_norm",
    )(x1, y, g.reshape(1, D))


def _pack_group(x, meta):
    B, S, D = x.shape
    z = jnp.zeros((B, FRONT, D), x.dtype)
    m = jnp.broadcast_to(meta.astype(x.dtype)[None], (B, N_META, D))
    return jnp.concatenate([z, m, x], axis=1).reshape(B * (S + LANES), D)


def _rope_tables(S, B):
    rows = S // GRID_W
    row_ids = jnp.repeat(jnp.arange(rows, dtype=F32), GRID_W)
    col_ids = jnp.tile(jnp.arange(GRID_W, dtype=F32), rows)
    half = HEAD_DIM // 2
    inv_freq = 1.0 / (ROPE_THETA ** (jnp.arange(0, half, 2, dtype=F32) / half))
    ang = jnp.concatenate([row_ids[:, None] * inv_freq, col_ids[:, None] * inv_freq], axis=-1)
    ang = jnp.concatenate([jnp.zeros((LANES, half), F32), ang], axis=0)
    c, s = jnp.cos(ang), jnp.sin(ang)
    return (jnp.tile(jnp.concatenate([c, c], axis=-1), (B, 1)),
            jnp.tile(jnp.concatenate([-s, s], axis=-1), (B, 1)))


def _key_bias(Np):
    pos = np.arange(Np)
    return jnp.asarray(np.where(pos < FRONT, NEG_BIG, 0.0).astype(np.float32)[None, :])


def _seq_flags(groups):
    fwd = []
    for B, Np in groups:
        nc = Np // CHUNK
        for _ in range(B):
            fwd += [1] + [0] * (nc - 1)
    fwd = np.asarray(fwd, np.int32)
    last = np.roll(fwd, -1)
    return jnp.asarray(fwd), jnp.asarray(last[::-1].copy())


def kernel(x_prompt, x_sample, meta_tokens, norm_mix, w_in, q_norm, k_norm, shift_prev, shift_next, decay_up, decay_base, iclr_up, iclr_base, gate_up, k_k, k_a, r_k, ln_x_w, ln_x_b, w_branch_attn, w_branch_rwkv, w_out, norm_ffn, w_router, w_gate, w_up, w_down, norm_final):
    assert norm_mix.shape[0] == 1, "one layer"
    D = x_prompt.shape[-1]
    RW = D // 2
    groups = [(x.shape[0], x.shape[1] + LANES) for x in (x_prompt, x_sample)]
    seqs = [x.shape[1] for x in (x_prompt, x_sample)]
    assert all(s % LANES == 0 for s in seqs)
    offs = [0, groups[0][0] * groups[0][1]]
    r_used = offs[1] + groups[1][0] * groups[1][1]
    tm = 768 if r_used >= 8 * 768 else LANES
    tme = tm // 3 if tm % 3 == 0 else tm
    tmr = tm // 2 if tm % 256 == 0 else tm
    R = -(-r_used // tm) * tm
    tail = R - r_used

    x = jnp.concatenate([_pack_group(x_prompt, meta_tokens), _pack_group(x_sample, meta_tokens),
                         jnp.zeros((tail, D), x_prompt.dtype)], axis=0)
    tabs = [_rope_tables(S, B) for S, (B, _) in zip(seqs, groups)]
    cos_t = jnp.concatenate([t[0] for t in tabs] + [jnp.zeros((tail, HEAD_DIM), F32)], axis=0)
    sin_t = jnp.concatenate([t[1] for t in tabs] + [jnp.zeros((tail, HEAD_DIM), F32)], axis=0)

    w_in0 = w_in[0]
    c0 = 0
    cols = {}
    for name, width in (("q", ATTN_WIDTH), ("k", KV_WIDTH), ("v", KV_WIDTH), ("rkv", 3 * RW),
                        ("lora", DECAY_LORA + ICLR_LORA + GATE_LORA), ("merge", 2 * D)):
        cols[name] = w_in0[:, c0:c0 + width].astype(BF16)
        c0 += width
    w_lora = jnp.pad(cols["lora"], ((0, 0), (0, LORA_PAD - cols["lora"].shape[1])))

    h = _rmsnorm(x, norm_mix[0], tme, BF16)
    rope_specs = lambda g: [(g.reshape(1, HEAD_DIM), pl.BlockSpec((1, HEAD_DIM), lambda i, j: (0, 0))),
                            (cos_t, pl.BlockSpec((tm, HEAD_DIM), lambda i, j: (i, 0))),
                            (sin_t, pl.BlockSpec((tm, HEAD_DIM), lambda i, j: (i, 0)))]
    q_scale = HEAD_DIM ** -0.5 * math.log2(math.e)
    q = _matmul(h, cols["q"], tm=tm, out_dtype=BF16, name="proj_q",
                epilogue=functools.partial(_qk_epilogue, scale=q_scale), extras=rope_specs(q_norm[0]))
    k = _matmul(h, cols["k"], tm=tm, out_dtype=BF16, name="proj_k",
                epilogue=functools.partial(_qk_epilogue, scale=1.0), extras=rope_specs(k_norm[0]))
    v = _matmul(h, cols["v"], tm=tm, out_dtype=BF16, name="proj_v")
    rkv = _matmul(h, cols["rkv"], tm=tm, out_dtype=F32, name="proj_rkv")
    lora = _matmul(h, w_lora, tm=tm, tn=LORA_PAD, out_dtype=F32, name="proj_lora")
    gates = _matmul(h, cols["merge"], tm=tm, out_dtype=BF16, name="proj_merge_gates",
                    epilogue=_sigmoid_epilogue)

    attn_parts = []
    for (B, Np), off in zip(groups, offs):
        sl = slice(off, off + B * Np)
        attn_parts.append(_attention(q[sl], k[sl], v[sl], _key_bias(Np), B, Np))
    attn = jnp.concatenate(attn_parts + [jnp.zeros((tail, ATTN_WIDTH), BF16)], axis=0)

    cn = _pick(RW, (256, 128))
    vec = jnp.concatenate([shift_prev[0], shift_next[0], decay_base[0], iclr_base[0], k_k, k_a, r_k,
                           ln_x_w, ln_x_b, jnp.zeros((VEC_ROWS - 15, RW), F32)], axis=0).astype(F32)
    hid_idx = np.arange(cn) // RWKV_HEAD
    ones_bd = jnp.asarray((hid_idx[:, None] == hid_idx[None, :]).astype(np.float32)).astype(BF16)
    gup = jnp.pad(gate_up[0], ((0, GATE_PAD - GATE_LORA), (0, 0))).astype(BF16)
    r_s, v_s, kk, ew0, kd0, b0, ew1, kd1, b1, bonus, g_rwkv = _rwkv_prep(
        rkv, lora, decay_up[0], iclr_up[0], gup, vec, ones_bd, tmr, cn)
    first_fwd, first_rev = _seq_flags(groups + ([(1, tail)] if tail else []))
    t_idx = np.arange(CHUNK)
    tri_f = jnp.asarray((t_idx[None, :] <= t_idx[:, None]).astype(np.float32)).astype(BF16)
    tri_r = jnp.asarray((t_idx[None, :] >= t_idx[:, None]).astype(np.float32)).astype(BF16)
    npairs = _pick(RW // PAIR, (8, 4, 2, 1))
    y_f = _wkv(first_fwd, r_s, v_s, kk, ew0, kd0, b0, tri_f, reverse=False, npairs=npairs)
    y_b = _wkv(first_rev, r_s, v_s, kk, ew1, kd1, b1, tri_r, reverse=True, npairs=npairs)
    rwkv = _rwkv_post(y_f, y_b, bonus, g_rwkv, vec, ones_bd, tmr, cn)

    merged = _merge(attn, rwkv, w_branch_attn[0].astype(BF16), w_branch_rwkv[0].astype(BF16), gates, tm, _pick(D, (512, 256, 128)))
    x1 = _matmul(merged, w_out[0].astype(BF16), tm=tm, out_dtype=F32, name="out_proj",
                 epilogue=_residual_epilogue, extras=[(x, pl.BlockSpec((tm, _pick(D, (512, 384, 256, 128))), lambda i, j: (i, j)))])

    w_router_pad = jnp.pad(w_router[0], ((0, 0), (0, LANES - N_EXPERTS)))
    h2, aff = _router(x1, norm_ffn[0], w_router_pad, tme)
    aff = aff[:, :N_EXPERTS]
    idx_parts, gate_parts = [], []
    for (B, Np), off in zip(groups, offs):
        valid = jnp.asarray((np.arange(B * Np) % Np) >= FRONT)
        a_g = jnp.where(valid[:, None], aff[off:off + B * Np], -1.0)
        cap = CAPACITY_FACTOR * (B * (Np - FRONT)) // N_EXPERTS
        gate_g, idx_g = lax.top_k(a_g.T, cap)
        idx_parts.append(idx_g + off)
        gate_parts.append(gate_g)
    idx = jnp.concatenate(idx_parts, axis=1)
    gate = jnp.concatenate(gate_parts, axis=1)
    c_tot = idx.shape[1]
    nt = -(-c_tot // 1040)
    tmx = -(-(-(-c_tot // nt)) // 16) * 16
    ct = nt * tmx
    idx = jnp.pad(idx, ((0, 0), (0, ct - c_tot))).reshape(-1)
    gate = jnp.pad(gate, ((0, 0), (0, ct - c_tot))).reshape(-1, 1)
    xs = jnp.take(h2, idx, axis=0)
    tf = _pick(D, (256, 128))
    out = _expert_ffn(xs, gate, w_gate[0], w_up[0], w_down[0], tmx, tf)
    y = jnp.zeros((R, D), F32).at[idx].add(out)

    return tuple(_final(x1, y, norm_final, B, Np, off) for (B, Np), off in zip(groups, offs))
```

```python
import functools
import math

import numpy as np
import jax
import jax.numpy as jnp
from jax import lax
from jax.experimental import pallas as pl
from jax.experimental.pallas import tpu as pltpu

F32 = jnp.float32
BF16 = jnp.bfloat16

N_META = 16
GRID_W = 64
HEAD_DIM = 128
N_Q_HEADS = 16
N_KV_HEADS = 4
Q_PER_KV = N_Q_HEADS // N_KV_HEADS
ATTN_WIDTH = N_Q_HEADS * HEAD_DIM
KV_WIDTH = N_KV_HEADS * HEAD_DIM
ROPE_THETA = 10000.0
RWKV_HEAD = 64
DECAY_LORA = 128
ICLR_LORA = 128
GATE_LORA = 480
N_EXPERTS = 16
CAPACITY_FACTOR = 2
NORM_EPS = 1e-6
GN_EPS = 64e-5

LANES = 128
FRONT = LANES - N_META
CHUNK = 64
PAIR = 2 * RWKV_HEAD
LORA_PAD = 768
GATE_PAD = LORA_PAD - DECAY_LORA - ICLR_LORA
VMEM_LIMIT = 56 * 1024 * 1024
NEG_BIG = -1e30


def _cparams(sem):
    return pltpu.CompilerParams(dimension_semantics=sem, vmem_limit_bytes=VMEM_LIMIT)


def _pick(n, cands):
    for c in cands:
        if n % c == 0:
            return c
    raise ValueError(f"no tile for {n} in {cands}")


def _split2(x):
    hi = x.astype(BF16)
    lo = (x - hi.astype(F32)).astype(BF16)
    return hi, lo


def _split3(x):
    hi = x.astype(BF16)
    r1 = x - hi.astype(F32)
    mid = r1.astype(BF16)
    lo = (r1 - mid.astype(F32)).astype(BF16)
    return hi, mid, lo


def _dot(a, b):
    return jnp.dot(a, b, preferred_element_type=F32)


def _dot_nt(a, b):
    return lax.dot_general(a, b, (((1,), (1,)), ((), ())), preferred_element_type=F32)


def _dot1(a, b):
    return _dot(a.astype(BF16), b.astype(BF16))


def _dot3(a, b):
    ah, al = _split2(a)
    bh, bl = _split2(b)
    return _dot(ah, bh) + (_dot(ah, bl) + _dot(al, bh))


def _dot_exact_lhs(a_bf16, b):
    hi, mid, lo = _split3(b)
    return _dot(a_bf16, hi) + (_dot(a_bf16, mid) + _dot(a_bf16, lo))


def _segsum(x, ones_bd):
    hi, mid, lo = _split3(x)
    return _dot(hi, ones_bd) + (_dot(mid, ones_bd) + _dot(lo, ones_bd))


def _sigmoid(x):
    return 1.0 / (1.0 + jnp.exp(-x))


def _rmsnorm_kernel(x_ref, g_ref, o_ref):
    x = x_ref[...]
    ms = jnp.mean(x * x, axis=-1, keepdims=True)
    o_ref[...] = (x * lax.rsqrt(ms + NORM_EPS) * g_ref[...]).astype(o_ref.dtype)


def _rmsnorm(x, g, tm, out_dtype):
    R, D = x.shape
    return pl.pallas_call(
        _rmsnorm_kernel,
        grid=(R // tm,),
        in_specs=[pl.BlockSpec((tm, D), lambda i: (i, 0)), pl.BlockSpec((1, D), lambda i: (0, 0))],
        out_specs=pl.BlockSpec((tm, D), lambda i: (i, 0)),
        out_shape=jax.ShapeDtypeStruct((R, D), out_dtype),
        compiler_params=_cparams(("parallel",)),
        name="rmsnorm",
    )(x, g.reshape(1, D))


def _mm_kernel(*refs, epilogue, n_extra):
    x_ref, w_ref = refs[0], refs[1]
    extras = refs[2:2 + n_extra]
    o_ref = refs[2 + n_extra]
    acc = _dot(x_ref[...], w_ref[...])
    if epilogue is not None:
        acc = epilogue(acc, *extras)
    o_ref[...] = acc.astype(o_ref.dtype)


def _matmul(x, w, *, tm, out_dtype, name, tn=None, epilogue=None, extras=()):
    M, K = x.shape
    N = w.shape[1]
    tn = tn or _pick(N, (512, 384, 256, 128))
    in_specs = [pl.BlockSpec((tm, K), lambda i, j: (i, 0)), pl.BlockSpec((K, tn), lambda i, j: (0, j))]
    in_specs += [spec for _, spec in extras]
    return pl.pallas_call(
        functools.partial(_mm_kernel, epilogue=epilogue, n_extra=len(extras)),
        grid=(M // tm, N // tn),
        in_specs=in_specs,
        out_specs=pl.BlockSpec((tm, tn), lambda i, j: (i, j)),
        out_shape=jax.ShapeDtypeStruct((M, N), out_dtype),
        compiler_params=_cparams(("parallel", "parallel")),
        name=name,
    )(x, w, *[a for a, _ in extras])


def _qk_epilogue(acc, g_ref, c_ref, s_ref, *, scale):
    g = g_ref[...]
    c = c_ref[...]
    s = s_ref[...]
    outs = []
    for h in range(acc.shape[1] // HEAD_DIM):
        y = acc[:, h * HEAD_DIM:(h + 1) * HEAD_DIM]
        y = y * lax.rsqrt(jnp.mean(y * y, axis=-1, keepdims=True) + NORM_EPS) * g
        y = y * c + pltpu.roll(y, HEAD_DIM // 2, 1) * s
        outs.append(y * scale if scale != 1.0 else y)
    return jnp.concatenate(outs, axis=1) if len(outs) > 1 else outs[0]


def _sigmoid_epilogue(acc):
    return _sigmoid(acc)


def _residual_epilogue(acc, x_ref):
    return x_ref[...] + acc


def _attn_kernel(q_ref, k_ref, vt_ref, bias_ref, o_ref, m_sc, l_sc, acc_sc, *, tq, nk):
    j = pl.program_id(3)
    M = Q_PER_KV * tq

    @pl.when(j == 0)
    def _():
        m_sc[...] = jnp.full(m_sc.shape, NEG_BIG, F32)
        l_sc[...] = jnp.zeros(l_sc.shape, F32)
        acc_sc[...] = jnp.zeros(acc_sc.shape, F32)

    q = q_ref[...]
    q4 = jnp.concatenate([q[:, h * HEAD_DIM:(h + 1) * HEAD_DIM] for h in range(Q_PER_KV)], axis=0)

    def step(use_bias):
        s = _dot_nt(k_ref[...], q4)
        if use_bias:
            s = s + jnp.tile(bias_ref[...], (1, M // LANES))
        m_prev = m_sc[...]
        m_new = jnp.maximum(m_prev, jnp.max(s, axis=0, keepdims=True))
        alpha = jnp.exp2(m_prev - m_new)
        p = jnp.exp2(s - m_new)
        l_sc[...] = alpha * l_sc[...] + jnp.sum(p, axis=0, keepdims=True)
        acc_sc[...] = alpha * acc_sc[...] + _dot(vt_ref[...], p.astype(BF16))
        m_sc[...] = m_new

    if nk == 1:
        step(True)
    else:
        pl.when(j == 0)(functools.partial(step, True))
        pl.when(j > 0)(functools.partial(step, False))

    @pl.when(j == nk - 1)
    def _():
        o = (acc_sc[...] / l_sc[...]).T
        o_ref[...] = jnp.concatenate([o[h * tq:(h + 1) * tq] for h in range(Q_PER_KV)], axis=1).astype(o_ref.dtype)


def _attention(q, k, vt, bias, B, Np):
    tq = _pick(Np, (640, 512, 384, 256, 128))
    tk = Np if Np <= 2304 else _pick(Np, (1024, 896, 768, 640, 512, 384, 256, 128))
    nq, nk = Np // tq, Np // tk
    M = Q_PER_KV * tq
    return pl.pallas_call(
        functools.partial(_attn_kernel, tq=tq, nk=nk),
        grid=(B, N_KV_HEADS, nq, nk),
        in_specs=[
            pl.BlockSpec((tq, Q_PER_KV * HEAD_DIM), lambda b, g, i, j: (b * nq + i, g)),
            pl.BlockSpec((tk, HEAD_DIM), lambda b, g, i, j: (b * nk + j, g)),
            pl.BlockSpec((HEAD_DIM, tk), lambda b, g, i, j: (g, b * nk + j)),
            pl.BlockSpec((tk, LANES), lambda b, g, i, j: (j, 0)),
        ],
        out_specs=pl.BlockSpec((tq, Q_PER_KV * HEAD_DIM), lambda b, g, i, j: (b * nq + i, g)),
        out_shape=jax.ShapeDtypeStruct((B * Np, ATTN_WIDTH), BF16),
        scratch_shapes=[
            pltpu.VMEM((1, M), F32),
            pltpu.VMEM((1, M), F32),
            pltpu.VMEM((HEAD_DIM, M), F32),
        ],
        compiler_params=_cparams(("parallel", "parallel", "parallel", "arbitrary")),
        name="attention",
    )(q, k, vt, bias)


V_SP, V_SN, V_DBASE, V_IBASE, V_KK, V_KA, V_RK, V_LNW, V_LNB = 0, 3, 6, 8, 10, 11, 12, 13, 14
VEC_ROWS = 16
HALO = 8


def _prep_kernel(r_ref, k_ref, v_ref, rp_ref, rn_ref, kp_ref, kn_ref, vp_ref, vn_ref, lora_ref,
                 dup0_ref, dup1_ref, iup0_ref, iup1_ref, gup_ref, vec_ref, ones_ref,
                 r_o, v_o, kk_o, ew0_o, kd0_o, b0_o, ew1_o, kd1_o, b1_o, bonus_o, g_o, *, tm, n_row_tiles):
    i = pl.program_id(0)
    has_prev = (i > 0).astype(F32)
    has_next = (i < n_row_tiles - 1).astype(F32)
    vec = vec_ref[...]
    row = lax.broadcasted_iota(jnp.int32, (tm, 1), 0)

    def shifted(x_ref, p_ref, n_ref, idx):
        x = x_ref[...]
        prev = jnp.where(row == 0, p_ref[HALO - 1:HALO, :] * has_prev, pltpu.roll(x, 1, 0))
        nxt = jnp.where(row == tm - 1, n_ref[0:1, :] * has_next, pltpu.roll(x, tm - 1, 0))
        return x + vec[V_SP + idx:V_SP + idx + 1] * (prev - x) + vec[V_SN + idx:V_SN + idx + 1] * (nxt - x)

    r = shifted(r_ref, rp_ref, rn_ref, 0)
    k = shifted(k_ref, kp_ref, kn_ref, 1)
    v = shifted(v_ref, vp_ref, vn_ref, 2)
    ones = ones_ref[...]

    lora = lora_ref[...]
    decay_h = jnp.tanh(lora[:, 0:DECAY_LORA])
    iclr_h = lora[:, DECAY_LORA:DECAY_LORA + ICLR_LORA]
    gate_h = _sigmoid(lora[:, DECAY_LORA + ICLR_LORA:LORA_PAD])

    kkr = k * vec[V_KK:V_KK + 1]
    kk = kkr / jnp.maximum(jnp.sqrt(_segsum(kkr * kkr, ones)), 1e-12)
    r_o[...] = r.astype(r_o.dtype)
    v_o[...] = v.astype(v_o.dtype)
    kk_o[...] = kk.astype(kk_o.dtype)
    bonus_o[...] = (_segsum(r * k * vec[V_RK:V_RK + 1], ones) * v).astype(bonus_o.dtype)
    g_o[...] = _dot(gate_h.astype(BF16), gup_ref[...]).astype(g_o.dtype)

    for d, (dup_ref, iup_ref, ew_o, kd_o, b_o) in enumerate(
            ((dup0_ref, iup0_ref, ew0_o, kd0_o, b0_o), (dup1_ref, iup1_ref, ew1_o, kd1_o, b1_o))):
        u = -(vec[V_DBASE + d:V_DBASE + d + 1] + _dot3(decay_h, dup_ref[...]))
        softplus = jnp.maximum(u, 0.0) + jnp.log(1.0 + jnp.exp(-jnp.abs(u)))
        ew_o[...] = jnp.exp(-softplus - 0.5)
        a = _sigmoid(vec[V_IBASE + d:V_IBASE + d + 1] + _dot3(iclr_h, iup_ref[...]))
        kd_o[...] = (k * (1.0 + (a - 1.0) * vec[V_KA:V_KA + 1])).astype(kd_o.dtype)
        b_o[...] = (kk * a).astype(b_o.dtype)


def _rwkv_prep(rkv, lora, dup, iup, gup, vec, ones_bd, tm, cn):
    R = rkv.shape[0]
    RW = rkv.shape[1] // 3
    J = RW // cn
    n_row_tiles = R // tm
    hb = tm // HALO
    last_hb = R // HALO - 1

    def main(c):
        return pl.BlockSpec((tm, cn), lambda i, j: (i, c * J + j))

    def prev(c):
        return pl.BlockSpec((HALO, cn), lambda i, j: (jnp.maximum(i * hb - 1, 0), c * J + j))

    def nxt(c):
        return pl.BlockSpec((HALO, cn), lambda i, j: (jnp.minimum((i + 1) * hb, last_hb), c * J + j))

    up = pl.BlockSpec((DECAY_LORA, cn), lambda i, j: (0, j))
    in_specs = [main(0), main(1), main(2), prev(0), nxt(0), prev(1), nxt(1), prev(2), nxt(2),
                pl.BlockSpec((tm, LORA_PAD), lambda i, j: (i, 0)),
                up, up, up, up,
                pl.BlockSpec((GATE_PAD, cn), lambda i, j: (0, j)),
                pl.BlockSpec((VEC_ROWS, cn), lambda i, j: (0, j)),
                pl.BlockSpec((cn, cn), lambda i, j: (0, 0))]
    out_spec = pl.BlockSpec((tm, cn), lambda i, j: (i, j))
    out_dtypes = [BF16, BF16, BF16, F32, BF16, BF16, F32, BF16, BF16, BF16, BF16]
    return pl.pallas_call(
        functools.partial(_prep_kernel, tm=tm, n_row_tiles=n_row_tiles),
        grid=(n_row_tiles, J),
        in_specs=in_specs,
        out_specs=[out_spec] * len(out_dtypes),
        out_shape=[jax.ShapeDtypeStruct((R, RW), dt) for dt in out_dtypes],
        compiler_params=_cparams(("parallel", "parallel")),
        name="rwkv_prep",
    )(rkv, rkv, rkv, rkv, rkv, rkv, rkv, rkv, rkv, lora, dup[0], dup[1], iup[0], iup[1], gup, vec, ones_bd)


def _wkv_kernel(first_ref, r_ref, v_ref, kk_ref, ew_ref, kd_ref, b_ref, tri_ref, y_ref, h_sc, *, reverse, npairs):
    c = pl.program_id(1)

    @pl.when(first_ref[c] == 1)
    def _():
        h_sc[...] = jnp.zeros(h_sc.shape, F32)

    L = CHUNK
    ew = ew_ref[...]
    cs = _dot_exact_lhs(tri_ref[...], ew)
    tot = cs[0:1] if reverse else cs[L - 1:L]
    e_neg = jnp.exp(-cs)
    e_prev = jnp.exp(ew - cs)
    e_pos = jnp.exp(cs)
    e_fin = jnp.exp(cs - tot)
    w_tot = jnp.exp(-tot)

    ri = lax.broadcasted_iota(jnp.int32, (PAIR, PAIR), 0)
    ci = lax.broadcasted_iota(jnp.int32, (PAIR, PAIR), 1)
    same = (ri // L) == (ci // L)
    t_i = ri % L
    s_i = ci % L
    if reverse:
        strict = same & (s_i > t_i)
        incl = same & (s_i >= t_i)
    else:
        strict = same & (s_i < t_i)
        incl = same & (s_i <= t_i)
    eye = ri == ci
    head0 = lax.broadcasted_iota(jnp.int32, (L, PAIR), 1) < RWKV_HEAD

    def stack(x):
        return jnp.concatenate([jnp.where(head0, x, 0.0), jnp.where(head0, 0.0, x)], axis=0)

    P2 = 2 * PAIR
    pairs = range(npairs)
    sls = [slice(p * PAIR, (p + 1) * PAIR) for p in pairs]

    r_st, a_st, v_st, bk2, kf_t, bf_t = [], [], [], [], [], []
    for sl in sls:
        kd = kd_ref[:, sl].astype(F32)
        b = b_ref[:, sl].astype(F32)
        bt = (b * e_pos[:, sl]).astype(BF16)
        kt = (kd * e_pos[:, sl]).astype(BF16)
        r_st.append(stack(r_ref[:, sl].astype(F32) * e_neg[:, sl]))
        a_st.append(stack(-kk_ref[:, sl].astype(F32) * e_prev[:, sl]))
        v_st.append(stack(v_ref[:, sl].astype(F32)).astype(BF16))
        bk2.append(jnp.concatenate([bt, bt, kt, kt], axis=0))
        kf_t.append(stack(kd * e_fin[:, sl]).T)
        bf_t.append(stack(b * e_fin[:, sl]).T)

    a_ab, a_ak, wlhs = [], [], []
    for p in pairs:
        sc = _dot_nt(jnp.concatenate([a_st[p], r_st[p]], axis=0).astype(BF16), bk2[p])
        a_ab.append(jnp.where(strict, sc[0:PAIR, 0:PAIR], 0.0))
        a_ak.append(jnp.where(strict, sc[0:PAIR, PAIR:P2], 0.0).astype(BF16))
        m_rb = jnp.where(incl, sc[PAIR:P2, 0:PAIR], 0.0)
        m_rk = jnp.where(incl, sc[PAIR:P2, PAIR:P2], 0.0)
        wlhs.append(jnp.concatenate([jnp.concatenate([m_rk, m_rb], axis=1),
                                     jnp.concatenate([kf_t[p], bf_t[p]], axis=1)], axis=0).astype(BF16))

    x = [jnp.concatenate([a_st[p], _dot(a_ak[p], v_st[p])], axis=1) for p in pairs]
    npow = a_ab
    n_sq = int(math.log2(L)) - 1
    for it in range(n_sq + 1):
        lhs = [npow[p].astype(BF16) for p in pairs]
        if it < n_sq:
            prod = [_dot(lhs[p], jnp.concatenate([lhs[p], x[p].astype(BF16)], axis=1)) for p in pairs]
            npow = [prod[p][:, 0:PAIR] for p in pairs]
            x = [x[p] + prod[p][:, PAIR:] for p in pairs]
        else:
            x = [x[p] + _dot(lhs[p], x[p].astype(BF16)) for p in pairs]

    zero = jnp.zeros((PAIR, PAIR), BF16)
    for p in pairs:
        sl = sls[p]
        rhs = jnp.concatenate([jnp.concatenate([zero, v_st[p]], axis=1), x[p].astype(BF16)], axis=0)
        w = _dot(wlhs[p], rhs)
        g_st = r_st[p] + w[0:PAIR, 0:PAIR]
        y0_st = w[0:PAIR, PAIR:P2]
        phi = jnp.where(eye, w_tot[:, sl], 0.0) + w[PAIR:P2, 0:PAIR]
        psi = w[PAIR:P2, PAIR:P2]
        out = _dot1(jnp.concatenate([g_st, phi], axis=0), h_sc[p])
        y_st = out[0:PAIR] + y0_st
        h_sc[p] = out[PAIR:P2] + psi
        y_ref[:, sl] = y_st[0:L] + y_st[L:2 * L]


def _wkv(first, r, v, kk, ew, kd, b, tri, *, reverse, npairs):
    R, RW = r.shape
    NC = R // CHUNK
    width = npairs * PAIR
    if reverse:
        blk = pl.BlockSpec((CHUNK, width), lambda g, c, f: (NC - 1 - c, g))
    else:
        blk = pl.BlockSpec((CHUNK, width), lambda g, c, f: (c, g))
    grid_spec = pltpu.PrefetchScalarGridSpec(
        num_scalar_prefetch=1,
        grid=(RW // width, NC),
        in_specs=[blk] * 6 + [pl.BlockSpec((CHUNK, CHUNK), lambda g, c, f: (0, 0))],
        out_specs=blk,
        scratch_shapes=[pltpu.VMEM((npairs, PAIR, PAIR), F32)],
    )
    return pl.pallas_call(
        functools.partial(_wkv_kernel, reverse=reverse, npairs=npairs),
        grid_spec=grid_spec,
        out_shape=jax.ShapeDtypeStruct((R, RW), F32),
        compiler_params=_cparams(("parallel", "arbitrary")),
        name="wkv_rev" if reverse else "wkv_fwd",
    )(first, r, v, kk, ew, kd, b, tri)


def _post_kernel(yf_ref, yb_ref, bonus_ref, g_ref, vec_ref, ones_ref, o_ref):
    ones = ones_ref[...]
    vec = vec_ref[...]
    y = yf_ref[...] + yb_ref[...]
    inv_n = 1.0 / RWKV_HEAD
    mu = _segsum(y, ones) * inv_n
    d = y - mu
    var = _segsum(d * d, ones) * inv_n
    yn = d * lax.rsqrt(var + GN_EPS) * vec[V_LNW:V_LNW + 1] + vec[V_LNB:V_LNB + 1]
    o_ref[...] = ((yn + bonus_ref[...]) * g_ref[...]).astype(o_ref.dtype)


def _rwkv_post(yf, yb, bonus, g, vec, ones_bd, tm, cn):
    R, RW = yf.shape
    blk = pl.BlockSpec((tm, cn), lambda i, j: (i, j))
    return pl.pallas_call(
        _post_kernel,
        grid=(R // tm, RW // cn),
        in_specs=[blk, blk, blk, blk, pl.BlockSpec((VEC_ROWS, cn), lambda i, j: (0, j)),
                  pl.BlockSpec((cn, cn), lambda i, j: (0, 0))],
        out_specs=blk,
        out_shape=jax.ShapeDtypeStruct((R, RW), BF16),
        compiler_params=_cparams(("parallel", "parallel")),
        name="rwkv_post",
    )(yf, yb, bonus, g, vec, ones_bd)


def _merge_kernel(a_ref, r_ref, wa_ref, wr_ref, ga_ref, gr_ref, o_ref):
    ya = _dot(a_ref[...], wa_ref[...])
    yr = _dot(r_ref[...], wr_ref[...])
    o_ref[...] = (ga_ref[...].astype(F32) * ya + gr_ref[...].astype(F32) * yr).astype(o_ref.dtype)


def _merge(attn, rwkv, wa, wr, gates, tm, tn):
    R = attn.shape[0]
    D = wa.shape[1]
    J = D // tn
    return pl.pallas_call(
        _merge_kernel,
        grid=(R // tm, J),
        in_specs=[
            pl.BlockSpec((tm, attn.shape[1]), lambda i, j: (i, 0)),
            pl.BlockSpec((tm, rwkv.shape[1]), lambda i, j: (i, 0)),
            pl.BlockSpec((wa.shape[0], tn), lambda i, j: (0, j)),
            pl.BlockSpec((wr.shape[0], tn), lambda i, j: (0, j)),
            pl.BlockSpec((tm, tn), lambda i, j: (i, j)),
            pl.BlockSpec((tm, tn), lambda i, j: (i, J + j)),
        ],
        out_specs=pl.BlockSpec((tm, tn), lambda i, j: (i, j)),
        out_shape=jax.ShapeDtypeStruct((R, D), BF16),
        compiler_params=_cparams(("parallel", "parallel")),
        name="merge",
    )(attn, rwkv, wa, wr, gates, gates)


def _router_kernel(x_ref, g_ref, w_ref, h_ref, aff_ref):
    x = x_ref[...]
    ms = jnp.mean(x * x, axis=-1, keepdims=True)
    h = x * lax.rsqrt(ms + NORM_EPS) * g_ref[...]
    h_ref[...] = h.astype(h_ref.dtype)
    logits = _dot3(h, w_ref[...])
    lane = lax.broadcasted_iota(jnp.int32, logits.shape, 1)
    logits = jnp.where(lane < N_EXPERTS, logits, NEG_BIG)
    e = jnp.exp(logits - jnp.max(logits, axis=-1, keepdims=True))
    aff_ref[...] = e / jnp.sum(e, axis=-1, keepdims=True)


def _router(x1, g, w_router_pad, tm):
    R, D = x1.shape
    return pl.pallas_call(
        _router_kernel,
        grid=(R // tm,),
        in_specs=[pl.BlockSpec((tm, D), lambda i: (i, 0)), pl.BlockSpec((1, D), lambda i: (0, 0)),
                  pl.BlockSpec((D, LANES), lambda i: (0, 0))],
        out_specs=[pl.BlockSpec((tm, D), lambda i: (i, 0)), pl.BlockSpec((tm, LANES), lambda i: (i, 0))],
        out_shape=[jax.ShapeDtypeStruct((R, D), BF16), jax.ShapeDtypeStruct((R, LANES), F32)],
        compiler_params=_cparams(("parallel",)),
        name="ffn_norm_router",
    )(x1, g.reshape(1, D), w_router_pad)


def _ffn_up_kernel(x_ref, wg_ref, wu_ref, o_ref):
    x = x_ref[...]
    hg = _dot(x, wg_ref[...].astype(BF16))
    hu = _dot(x, wu_ref[...].astype(BF16))
    o_ref[...] = (hg * _sigmoid(hg) * hu).astype(o_ref.dtype)


def _ffn_down_kernel(h_ref, wd_ref, gate_ref, o_ref):
    o_ref[...] = (_dot(h_ref[...], wd_ref[...].astype(BF16)) * gate_ref[...]).astype(o_ref.dtype)


def _expert_ffn(xs, gate, w_gate, w_up, w_down, tm, tf):
    E, D, F = w_gate.shape
    nt = xs.shape[0] // (E * tm)
    wspec = pl.BlockSpec((None, D, tf), lambda e, i, f: (e, 0, f))
    hid = pl.pallas_call(
        _ffn_up_kernel,
        grid=(E, nt, F // tf),
        in_specs=[pl.BlockSpec((tm, D), lambda e, i, f: (e * nt + i, 0)), wspec, wspec],
        out_specs=pl.BlockSpec((tm, tf), lambda e, i, f: (e * nt + i, f)),
        out_shape=jax.ShapeDtypeStruct((xs.shape[0], F), BF16),
        compiler_params=_cparams(("parallel", "parallel", "parallel")),
        name="ffn_up",
    )(xs, w_gate, w_up)
    return pl.pallas_call(
        _ffn_down_kernel,
        grid=(E, nt, D // tf),
        in_specs=[pl.BlockSpec((tm, F), lambda e, i, f: (e * nt + i, 0)),
                  pl.BlockSpec((None, F, tf), lambda e, i, f: (e, 0, f)),
                  pl.BlockSpec((tm, 1), lambda e, i, f: (e * nt + i, 0))],
        out_specs=pl.BlockSpec((tm, tf), lambda e, i, f: (e * nt + i, f)),
        out_shape=jax.ShapeDtypeStruct((xs.shape[0], D), F32),
        compiler_params=_cparams(("parallel", "parallel", "parallel")),
        name="ffn_down",
    )(hid, w_down, gate)


def _final_kernel(x_ref, y_ref, g_ref, o_ref):
    x = x_ref[...] + y_ref[...]
    ms = jnp.mean(x * x, axis=-1, keepdims=True)
    o_ref[...] = x * lax.rsqrt(ms + NORM_EPS) * g_ref[...]


def _final(x1, y, g, B, Np, off):
    D = x1.shape[1]
    nb = Np // LANES
    base = off // LANES
    blk = pl.BlockSpec((LANES, D), lambda b, j: (base + b * nb + 1 + j, 0))
    return pl.pallas_call(
        _final_kernel,
        grid=(B, nb - 1),
        in_specs=[blk, blk, pl.BlockSpec((1, D), lambda b, j: (0, 0))],
        out_specs=pl.BlockSpec((None, LANES, D), lambda b, j: (b, j, 0)),
        out_shape=jax.ShapeDtypeStruct((B, Np - LANES, D), F32),
        compiler_params=_cparams(("parallel", "parallel")),
        name="final_norm",
    )(x1, y, g.reshape(1, D))


def _pack_group(x, meta):
    B, S, D = x.shape
    z = jnp.zeros((B, FRONT, D), x.dtype)
    m = jnp.broadcast_to(meta.astype(x.dtype)[None], (B, N_META, D))
    return jnp.concatenate([z, m, x], axis=1).reshape(B * (S + LANES), D)


def _rope_tables(S, B):
    rows = S // GRID_W
    row_ids = jnp.repeat(jnp.arange(rows, dtype=F32), GRID_W)
    col_ids = jnp.tile(jnp.arange(GRID_W, dtype=F32), rows)
    half = HEAD_DIM // 2
    inv_freq = 1.0 / (ROPE_THETA ** (jnp.arange(0, half, 2, dtype=F32) / half))
    ang = jnp.concatenate([row_ids[:, None] * inv_freq, col_ids[:, None] * inv_freq], axis=-1)
    ang = jnp.concatenate([jnp.zeros((LANES, half), F32), ang], axis=0)
    c, s = jnp.cos(ang), jnp.sin(ang)
    return (jnp.tile(jnp.concatenate([c, c], axis=-1), (B, 1)),
            jnp.tile(jnp.concatenate([-s, s], axis=-1), (B, 1)))


def _key_bias(Np):
    col = np.where(np.arange(Np) < FRONT, NEG_BIG, 0.0).astype(np.float32)
    return jnp.asarray(np.repeat(col[:, None], LANES, axis=1))


def _seq_flags(groups):
    fwd = []
    for B, Np in groups:
        nc = Np // CHUNK
        for _ in range(B):
            fwd += [1] + [0] * (nc - 1)
    fwd = np.asarray(fwd, np.int32)
    last = np.roll(fwd, -1)
    return jnp.asarray(fwd), jnp.asarray(last[::-1].copy())


def kernel(x_prompt, x_sample, meta_tokens, norm_mix, w_in, q_norm, k_norm, shift_prev, shift_next, decay_up, decay_base, iclr_up, iclr_base, gate_up, k_k, k_a, r_k, ln_x_w, ln_x_b, w_branch_attn, w_branch_rwkv, w_out, norm_ffn, w_router, w_gate, w_up, w_down, norm_final):
    assert norm_mix.shape[0] == 1, "one layer"
    D = x_prompt.shape[-1]
    RW = D // 2
    groups = [(x.shape[0], x.shape[1] + LANES) for x in (x_prompt, x_sample)]
    seqs = [x.shape[1] for x in (x_prompt, x_sample)]
    assert all(s % LANES == 0 for s in seqs)
    offs = [0, groups[0][0] * groups[0][1]]
    r_used = offs[1] + groups[1][0] * groups[1][1]
    tm = 768 if r_used >= 8 * 768 else LANES
    tme = tm // 3 if tm % 3 == 0 else tm
    tmr = tm // 2 if tm % 256 == 0 else tm
    R = -(-r_used // tm) * tm
    tail = R - r_used

    x = jnp.concatenate([_pack_group(x_prompt, meta_tokens), _pack_group(x_sample, meta_tokens),
                         jnp.zeros((tail, D), x_prompt.dtype)], axis=0)
    tabs = [_rope_tables(S, B) for S, (B, _) in zip(seqs, groups)]
    cos_t = jnp.concatenate([t[0] for t in tabs] + [jnp.zeros((tail, HEAD_DIM), F32)], axis=0)
    sin_t = jnp.concatenate([t[1] for t in tabs] + [jnp.zeros((tail, HEAD_DIM), F32)], axis=0)

    w_in0 = w_in[0]
    c0 = 0
    cols = {}
    for name, width in (("q", ATTN_WIDTH), ("k", KV_WIDTH), ("v", KV_WIDTH), ("rkv", 3 * RW),
                        ("lora", DECAY_LORA + ICLR_LORA + GATE_LORA), ("merge", 2 * D)):
        cols[name] = w_in0[:, c0:c0 + width].astype(BF16)
        c0 += width
    w_lora = jnp.pad(cols["lora"], ((0, 0), (0, LORA_PAD - cols["lora"].shape[1])))

    h = _rmsnorm(x, norm_mix[0], tme, BF16)
    rope_specs = lambda g: [(g.reshape(1, HEAD_DIM), pl.BlockSpec((1, HEAD_DIM), lambda i, j: (0, 0))),
                            (cos_t, pl.BlockSpec((tm, HEAD_DIM), lambda i, j: (i, 0))),
                            (sin_t, pl.BlockSpec((tm, HEAD_DIM), lambda i, j: (i, 0)))]
    q_scale = HEAD_DIM ** -0.5 * math.log2(math.e)
    q = _matmul(h, cols["q"], tm=tm, out_dtype=BF16, name="proj_q",
                epilogue=functools.partial(_qk_epilogue, scale=q_scale), extras=rope_specs(q_norm[0]))
    k = _matmul(h, cols["k"], tm=tm, out_dtype=BF16, name="proj_k",
                epilogue=functools.partial(_qk_epilogue, scale=1.0), extras=rope_specs(k_norm[0]))
    v = _matmul(h, cols["v"], tm=tm, out_dtype=BF16, name="proj_v")
    rkv = _matmul(h, cols["rkv"], tm=tm, out_dtype=F32, name="proj_rkv")
    lora = _matmul(h, w_lora, tm=tm, tn=LORA_PAD, out_dtype=F32, name="proj_lora")
    gates = _matmul(h, cols["merge"], tm=tm, out_dtype=BF16, name="proj_merge_gates",
                    epilogue=_sigmoid_epilogue)

    attn_parts = []
    for (B, Np), off in zip(groups, offs):
        sl = slice(off, off + B * Np)
        attn_parts.append(_attention(q[sl], k[sl], v[sl].T, _key_bias(Np), B, Np))
    attn = jnp.concatenate(attn_parts + [jnp.zeros((tail, ATTN_WIDTH), BF16)], axis=0)

    cn = _pick(RW, (256, 128))
    vec = jnp.concatenate([shift_prev[0], shift_next[0], decay_base[0], iclr_base[0], k_k, k_a, r_k,
                           ln_x_w, ln_x_b, jnp.zeros((VEC_ROWS - 15, RW), F32)], axis=0).astype(F32)
    hid_idx = np.arange(cn) // RWKV_HEAD
    ones_bd = jnp.asarray((hid_idx[:, None] == hid_idx[None, :]).astype(np.float32)).astype(BF16)
    gup = jnp.pad(gate_up[0], ((0, GATE_PAD - GATE_LORA), (0, 0))).astype(BF16)
    r_s, v_s, kk, ew0, kd0, b0, ew1, kd1, b1, bonus, g_rwkv = _rwkv_prep(
        rkv, lora, decay_up[0], iclr_up[0], gup, vec, ones_bd, tmr, cn)
    first_fwd, first_rev = _seq_flags(groups + ([(1, tail)] if tail else []))
    t_idx = np.arange(CHUNK)
    tri_f = jnp.asarray((t_idx[None, :] <= t_idx[:, None]).astype(np.float32)).astype(BF16)
    tri_r = jnp.asarray((t_idx[None, :] >= t_idx[:, None]).astype(np.float32)).astype(BF16)
    npairs = _pick(RW // PAIR, (16, 8, 4, 2, 1))
    y_f = _wkv(first_fwd, r_s, v_s, kk, ew0, kd0, b0, tri_f, reverse=False, npairs=npairs)
    y_b = _wkv(first_rev, r_s, v_s, kk, ew1, kd1, b1, tri_r, reverse=True, npairs=npairs)
    rwkv = _rwkv_post(y_f, y_b, bonus, g_rwkv, vec, ones_bd, tmr, cn)

    merged = _merge(attn, rwkv, w_branch_attn[0].astype(BF16), w_branch_rwkv[0].astype(BF16), gates, tm, _pick(D, (512, 256, 128)))
    x1 = _matmul(merged, w_out[0].astype(BF16), tm=tm, out_dtype=F32, name="out_proj",
                 epilogue=_residual_epilogue, extras=[(x, pl.BlockSpec((tm, _pick(D, (512, 384, 256, 128))), lambda i, j: (i, j)))])

    w_router_pad = jnp.pad(w_router[0], ((0, 0), (0, LANES - N_EXPERTS)))
    h2, aff = _router(x1, norm_ffn[0], w_router_pad, tme)
    aff = aff[:, :N_EXPERTS]
    idx_parts, gate_parts = [], []
    for (B, Np), off in zip(groups, offs):
        valid = jnp.asarray((np.arange(B * Np) % Np) >= FRONT)
        a_g = jnp.where(valid[:, None], aff[off:off + B * Np], -1.0)
        cap = CAPACITY_FACTOR * (B * (Np - FRONT)) // N_EXPERTS
        gate_g, idx_g = lax.top_k(a_g.T, cap)
        idx_parts.append(idx_g + off)
        gate_parts.append(gate_g)
    idx = jnp.concatenate(idx_parts, axis=1)
    gate = jnp.concatenate(gate_parts, axis=1)
    c_tot = idx.shape[1]
    nt = -(-c_tot // 1040)
    tmx = -(-(-(-c_tot // nt)) // 16) * 16
    ct = nt * tmx
    idx = jnp.pad(idx, ((0, 0), (0, ct - c_tot))).reshape(-1)
    gate = jnp.pad(gate, ((0, 0), (0, ct - c_tot))).reshape(-1, 1)
    xs = jnp.take(h2, idx, axis=0)
    tf = _pick(D, (256, 128))
    out = _expert_ffn(xs, gate, w_gate[0], w_up[0], w_down[0], tmx, tf)
    y = jnp.zeros((R, D), F32).at[idx].add(out)

    return tuple(_final(x1, y, norm_final, B, Np, off) for (B, Np), off in zip(groups, offs))
```

```python
import functools
import math

import numpy as np
import jax
import jax.numpy as jnp
from jax import lax
from jax.experimental import pallas as pl
from jax.experimental.pallas import tpu as pltpu

F32 = jnp.float32
BF16 = jnp.bfloat16

N_META = 16
GRID_W = 64
HEAD_DIM = 128
N_Q_HEADS = 16
N_KV_HEADS = 4
Q_PER_KV = N_Q_HEADS // N_KV_HEADS
ATTN_WIDTH = N_Q_HEADS * HEAD_DIM
KV_WIDTH = N_KV_HEADS * HEAD_DIM
ROPE_THETA = 10000.0
RWKV_HEAD = 64
DECAY_LORA = 128
ICLR_LORA = 128
GATE_LORA = 480
N_EXPERTS = 16
CAPACITY_FACTOR = 2
NORM_EPS = 1e-6
GN_EPS = 64e-5

LANES = 128
FRONT = LANES - N_META
CHUNK = 64
PAIR = 2 * RWKV_HEAD
LORA_PAD = 768
GATE_PAD = LORA_PAD - DECAY_LORA - ICLR_LORA
VMEM_LIMIT = 56 * 1024 * 1024
NEG_BIG = -1e30


def _cparams(sem):
    return pltpu.CompilerParams(dimension_semantics=sem, vmem_limit_bytes=VMEM_LIMIT)


def _pick(n, cands):
    for c in cands:
        if n % c == 0:
            return c
    raise ValueError(f"no tile for {n} in {cands}")


def _split2(x):
    hi = x.astype(BF16)
    lo = (x - hi.astype(F32)).astype(BF16)
    return hi, lo


def _split3(x):
    hi = x.astype(BF16)
    r1 = x - hi.astype(F32)
    mid = r1.astype(BF16)
    lo = (r1 - mid.astype(F32)).astype(BF16)
    return hi, mid, lo


def _dot(a, b):
    return jnp.dot(a, b, preferred_element_type=F32)


def _dot_nt(a, b):
    return lax.dot_general(a, b, (((1,), (1,)), ((), ())), preferred_element_type=F32)


def _dot1(a, b):
    return _dot(a.astype(BF16), b.astype(BF16))


def _dot3(a, b):
    ah, al = _split2(a)
    bh, bl = _split2(b)
    return _dot(ah, bh) + (_dot(ah, bl) + _dot(al, bh))


def _dot_exact_lhs(a_bf16, b):
    hi, mid, lo = _split3(b)
    return _dot(a_bf16, hi) + (_dot(a_bf16, mid) + _dot(a_bf16, lo))


def _segsum(x, ones_bd):
    hi, mid, lo = _split3(x)
    return _dot(hi, ones_bd) + (_dot(mid, ones_bd) + _dot(lo, ones_bd))


def _sigmoid(x):
    return 1.0 / (1.0 + jnp.exp(-x))


def _rmsnorm_kernel(x_ref, g_ref, o_ref):
    x = x_ref[...]
    ms = jnp.mean(x * x, axis=-1, keepdims=True)
    o_ref[...] = (x * lax.rsqrt(ms + NORM_EPS) * g_ref[...]).astype(o_ref.dtype)


def _rmsnorm(x, g, tm, out_dtype):
    R, D = x.shape
    return pl.pallas_call(
        _rmsnorm_kernel,
        grid=(R // tm,),
        in_specs=[pl.BlockSpec((tm, D), lambda i: (i, 0)), pl.BlockSpec((1, D), lambda i: (0, 0))],
        out_specs=pl.BlockSpec((tm, D), lambda i: (i, 0)),
        out_shape=jax.ShapeDtypeStruct((R, D), out_dtype),
        compiler_params=_cparams(("parallel",)),
        name="rmsnorm",
    )(x, g.reshape(1, D))


def _mm_kernel(*refs, epilogue, n_extra):
    x_ref, w_ref = refs[0], refs[1]
    extras = refs[2:2 + n_extra]
    o_ref = refs[2 + n_extra]
    acc = _dot(x_ref[...], w_ref[...])
    if epilogue is not None:
        acc = epilogue(acc, *extras)
    o_ref[...] = acc.astype(o_ref.dtype)


def _matmul(x, w, *, tm, out_dtype, name, tn=None, epilogue=None, extras=()):
    M, K = x.shape
    N = w.shape[1]
    tn = tn or _pick(N, (512, 384, 256, 128))
    in_specs = [pl.BlockSpec((tm, K), lambda i, j: (i, 0)), pl.BlockSpec((K, tn), lambda i, j: (0, j))]
    in_specs += [spec for _, spec in extras]
    return pl.pallas_call(
        functools.partial(_mm_kernel, epilogue=epilogue, n_extra=len(extras)),
        grid=(M // tm, N // tn),
        in_specs=in_specs,
        out_specs=pl.BlockSpec((tm, tn), lambda i, j: (i, j)),
        out_shape=jax.ShapeDtypeStruct((M, N), out_dtype),
        compiler_params=_cparams(("parallel", "parallel")),
        name=name,
    )(x, w, *[a for a, _ in extras])


def _qk_epilogue(acc, g_ref, c_ref, s_ref, *, scale):
    g = g_ref[...]
    c = c_ref[...]
    s = s_ref[...]
    outs = []
    for h in range(acc.shape[1] // HEAD_DIM):
        y = acc[:, h * HEAD_DIM:(h + 1) * HEAD_DIM]
        y = y * lax.rsqrt(jnp.mean(y * y, axis=-1, keepdims=True) + NORM_EPS) * g
        y = y * c + pltpu.roll(y, HEAD_DIM // 2, 1) * s
        outs.append(y * scale if scale != 1.0 else y)
    return jnp.concatenate(outs, axis=1) if len(outs) > 1 else outs[0]


def _sigmoid_epilogue(acc):
    return _sigmoid(acc)


def _residual_epilogue(acc, x_ref):
    return x_ref[...] + acc


def _attn_kernel(q_ref, k_ref, vt_ref, bias_ref, o_ref, m_sc, l_sc, acc_sc, *, tq, nk):
    j = pl.program_id(3)
    M = Q_PER_KV * tq

    @pl.when(j == 0)
    def _():
        m_sc[...] = jnp.full(m_sc.shape, NEG_BIG, F32)
        l_sc[...] = jnp.zeros(l_sc.shape, F32)
        acc_sc[...] = jnp.zeros(acc_sc.shape, F32)

    q = q_ref[...]
    q4 = jnp.concatenate([q[:, h * HEAD_DIM:(h + 1) * HEAD_DIM] for h in range(Q_PER_KV)], axis=0)

    def step(use_bias):
        s = _dot_nt(k_ref[...], q4)
        if use_bias:
            s = s + jnp.tile(bias_ref[...], (1, M // LANES))
        m_prev = m_sc[...]
        m_new = jnp.maximum(m_prev, jnp.max(s, axis=0, keepdims=True))
        alpha = jnp.exp2(m_prev - m_new)
        p = jnp.exp2(s - m_new)
        l_sc[...] = alpha * l_sc[...] + jnp.sum(p, axis=0, keepdims=True)
        acc_sc[...] = alpha * acc_sc[...] + _dot(vt_ref[...], p.astype(BF16))
        m_sc[...] = m_new

    if nk == 1:
        step(True)
    else:
        pl.when(j == 0)(functools.partial(step, True))
        pl.when(j > 0)(functools.partial(step, False))

    @pl.when(j == nk - 1)
    def _():
        o = (acc_sc[...] / l_sc[...]).T
        o_ref[...] = jnp.concatenate([o[h * tq:(h + 1) * tq] for h in range(Q_PER_KV)], axis=1).astype(o_ref.dtype)


def _attention(q, k, vt, bias, B, Np):
    tq = _pick(Np, (640, 512, 384, 256, 128))
    tk = Np if Np <= 2304 else _pick(Np, (1024, 896, 768, 640, 512, 384, 256, 128))
    nq, nk = Np // tq, Np // tk
    M = Q_PER_KV * tq
    return pl.pallas_call(
        functools.partial(_attn_kernel, tq=tq, nk=nk),
        grid=(B, N_KV_HEADS, nq, nk),
        in_specs=[
            pl.BlockSpec((tq, Q_PER_KV * HEAD_DIM), lambda b, g, i, j: (b * nq + i, g)),
            pl.BlockSpec((tk, HEAD_DIM), lambda b, g, i, j: (b * nk + j, g)),
            pl.BlockSpec((HEAD_DIM, tk), lambda b, g, i, j: (g, b * nk + j)),
            pl.BlockSpec((tk, LANES), lambda b, g, i, j: (j, 0)),
        ],
        out_specs=pl.BlockSpec((tq, Q_PER_KV * HEAD_DIM), lambda b, g, i, j: (b * nq + i, g)),
        out_shape=jax.ShapeDtypeStruct((B * Np, ATTN_WIDTH), BF16),
        scratch_shapes=[
            pltpu.VMEM((1, M), F32),
            pltpu.VMEM((1, M), F32),
            pltpu.VMEM((HEAD_DIM, M), F32),
        ],
        compiler_params=_cparams(("parallel", "parallel", "parallel", "arbitrary")),
        name="attention",
    )(q, k, vt, bias)


V_SP, V_SN, V_DBASE, V_IBASE, V_KK, V_KA, V_RK, V_LNW, V_LNB = 0, 3, 6, 8, 10, 11, 12, 13, 14
VEC_ROWS = 16
HALO = 8


def _prep_kernel(r_ref, k_ref, v_ref, rp_ref, rn_ref, kp_ref, kn_ref, vp_ref, vn_ref, lora_ref,
                 dup0_ref, dup1_ref, iup0_ref, iup1_ref, gup_ref, vec_ref, ones_ref,
                 r_o, v_o, kk_o, ew0_o, kd0_o, b0_o, ew1_o, kd1_o, b1_o, bonus_o, g_o, *, tm, n_row_tiles):
    i = pl.program_id(0)
    has_prev = (i > 0).astype(F32)
    has_next = (i < n_row_tiles - 1).astype(F32)
    vec = vec_ref[...]
    row = lax.broadcasted_iota(jnp.int32, (tm, 1), 0)

    def shifted(x_ref, p_ref, n_ref, idx):
        x = x_ref[...]
        prev = jnp.where(row == 0, p_ref[HALO - 1:HALO, :] * has_prev, pltpu.roll(x, 1, 0))
        nxt = jnp.where(row == tm - 1, n_ref[0:1, :] * has_next, pltpu.roll(x, tm - 1, 0))
        return x + vec[V_SP + idx:V_SP + idx + 1] * (prev - x) + vec[V_SN + idx:V_SN + idx + 1] * (nxt - x)

    r = shifted(r_ref, rp_ref, rn_ref, 0)
    k = shifted(k_ref, kp_ref, kn_ref, 1)
    v = shifted(v_ref, vp_ref, vn_ref, 2)
    ones = ones_ref[...]

    lora = lora_ref[...]
    decay_h = jnp.tanh(lora[:, 0:DECAY_LORA])
    iclr_h = lora[:, DECAY_LORA:DECAY_LORA + ICLR_LORA]
    gate_h = _sigmoid(lora[:, DECAY_LORA + ICLR_LORA:LORA_PAD])

    kkr = k * vec[V_KK:V_KK + 1]
    kk = kkr / jnp.maximum(jnp.sqrt(_segsum(kkr * kkr, ones)), 1e-12)
    r_o[...] = r.astype(r_o.dtype)
    v_o[...] = v.astype(v_o.dtype)
    kk_o[...] = kk.astype(kk_o.dtype)
    bonus_o[...] = (_segsum(r * k * vec[V_RK:V_RK + 1], ones) * v).astype(bonus_o.dtype)
    g_o[...] = _dot(gate_h.astype(BF16), gup_ref[...]).astype(g_o.dtype)

    for d, (dup_ref, iup_ref, ew_o, kd_o, b_o) in enumerate(
            ((dup0_ref, iup0_ref, ew0_o, kd0_o, b0_o), (dup1_ref, iup1_ref, ew1_o, kd1_o, b1_o))):
        u = -(vec[V_DBASE + d:V_DBASE + d + 1] + _dot3(decay_h, dup_ref[...]))
        softplus = jnp.maximum(u, 0.0) + jnp.log(1.0 + jnp.exp(-jnp.abs(u)))
        ew_o[...] = jnp.exp(-softplus - 0.5)
        a = _sigmoid(vec[V_IBASE + d:V_IBASE + d + 1] + _dot3(iclr_h, iup_ref[...]))
        kd_o[...] = (k * (1.0 + (a - 1.0) * vec[V_KA:V_KA + 1])).astype(kd_o.dtype)
        b_o[...] = (kk * a).astype(b_o.dtype)


def _rwkv_prep(rkv, lora, dup, iup, gup, vec, ones_bd, tm, cn):
    R = rkv.shape[0]
    RW = rkv.shape[1] // 3
    J = RW // cn
    n_row_tiles = R // tm
    hb = tm // HALO
    last_hb = R // HALO - 1

    def main(c):
        return pl.BlockSpec((tm, cn), lambda i, j: (i, c * J + j))

    def prev(c):
        return pl.BlockSpec((HALO, cn), lambda i, j: (jnp.maximum(i * hb - 1, 0), c * J + j))

    def nxt(c):
        return pl.BlockSpec((HALO, cn), lambda i, j: (jnp.minimum((i + 1) * hb, last_hb), c * J + j))

    up = pl.BlockSpec((DECAY_LORA, cn), lambda i, j: (0, j))
    in_specs = [main(0), main(1), main(2), prev(0), nxt(0), prev(1), nxt(1), prev(2), nxt(2),
                pl.BlockSpec((tm, LORA_PAD), lambda i, j: (i, 0)),
                up, up, up, up,
                pl.BlockSpec((GATE_PAD, cn), lambda i, j: (0, j)),
                pl.BlockSpec((VEC_ROWS, cn), lambda i, j: (0, j)),
                pl.BlockSpec((cn, cn), lambda i, j: (0, 0))]
    out_spec = pl.BlockSpec((tm, cn), lambda i, j: (i, j))
    out_dtypes = [BF16, BF16, BF16, F32, BF16, BF16, F32, BF16, BF16, BF16, BF16]
    return pl.pallas_call(
        functools.partial(_prep_kernel, tm=tm, n_row_tiles=n_row_tiles),
        grid=(n_row_tiles, J),
        in_specs=in_specs,
        out_specs=[out_spec] * len(out_dtypes),
        out_shape=[jax.ShapeDtypeStruct((R, RW), dt) for dt in out_dtypes],
        compiler_params=_cparams(("parallel", "parallel")),
        name="rwkv_prep",
    )(rkv, rkv, rkv, rkv, rkv, rkv, rkv, rkv, rkv, lora, dup[0], dup[1], iup[0], iup[1], gup, vec, ones_bd)


def _wkv_kernel(first_ref, r_ref, v_ref, kk_ref, ew_ref, kd_ref, b_ref, tri_ref, y_ref, h_sc, *, reverse, npairs):
    c = pl.program_id(1)

    @pl.when(first_ref[c] == 1)
    def _():
        h_sc[...] = jnp.zeros(h_sc.shape, F32)

    L = CHUNK
    ew = ew_ref[...]
    cs = _dot_exact_lhs(tri_ref[...], ew)
    tot = cs[0:1] if reverse else cs[L - 1:L]
    e_neg = jnp.exp(-cs)
    e_prev = jnp.exp(ew - cs)
    e_pos = jnp.exp(cs)
    e_fin = jnp.exp(cs - tot)
    w_tot = jnp.exp(-tot)

    ri = lax.broadcasted_iota(jnp.int32, (PAIR, PAIR), 0)
    ci = lax.broadcasted_iota(jnp.int32, (PAIR, PAIR), 1)
    same = (ri // L) == (ci // L)
    t_i = ri % L
    s_i = ci % L
    if reverse:
        strict = same & (s_i > t_i)
        incl = same & (s_i >= t_i)
    else:
        strict = same & (s_i < t_i)
        incl = same & (s_i <= t_i)
    eye = ri == ci
    head0 = lax.broadcasted_iota(jnp.int32, (L, PAIR), 1) < RWKV_HEAD

    def stack(x):
        return jnp.concatenate([jnp.where(head0, x, 0.0), jnp.where(head0, 0.0, x)], axis=0)

    P2 = 2 * PAIR
    pairs = range(npairs)
    sls = [slice(p * PAIR, (p + 1) * PAIR) for p in pairs]

    r_st, a_st, v_st, bk2, kf_t, bf_t = [], [], [], [], [], []
    for sl in sls:
        kd = kd_ref[:, sl].astype(F32)
        b = b_ref[:, sl].astype(F32)
        bt = (b * e_pos[:, sl]).astype(BF16)
        kt = (kd * e_pos[:, sl]).astype(BF16)
        r_st.append(stack(r_ref[:, sl].astype(F32) * e_neg[:, sl]))
        a_st.append(stack(-kk_ref[:, sl].astype(F32) * e_prev[:, sl]))
        v_st.append(stack(v_ref[:, sl].astype(F32)).astype(BF16))
        bk2.append(jnp.concatenate([bt, bt, kt, kt], axis=0))
        kf_t.append(stack(kd * e_fin[:, sl]).T)
        bf_t.append(stack(b * e_fin[:, sl]).T)

    a_ab, a_ak, wlhs = [], [], []
    for p in pairs:
        sc = _dot_nt(jnp.concatenate([a_st[p], r_st[p]], axis=0).astype(BF16), bk2[p])
        a_ab.append(jnp.where(strict, sc[0:PAIR, 0:PAIR], 0.0))
        a_ak.append(jnp.where(strict, sc[0:PAIR, PAIR:P2], 0.0).astype(BF16))
        m_rb = jnp.where(incl, sc[PAIR:P2, 0:PAIR], 0.0)
        m_rk = jnp.where(incl, sc[PAIR:P2, PAIR:P2], 0.0)
        wlhs.append(jnp.concatenate([jnp.concatenate([m_rk, m_rb], axis=1),
                                     jnp.concatenate([kf_t[p], bf_t[p]], axis=1)], axis=0).astype(BF16))

    x = [jnp.concatenate([a_st[p], _dot(a_ak[p], v_st[p])], axis=1) for p in pairs]
    npow = a_ab
    n_sq = int(math.log2(L)) - 1
    for it in range(n_sq + 1):
        lhs = [npow[p].astype(BF16) for p in pairs]
        if it < n_sq:
            prod = [_dot(lhs[p], jnp.concatenate([lhs[p], x[p].astype(BF16)], axis=1)) for p in pairs]
            npow = [prod[p][:, 0:PAIR] for p in pairs]
            x = [x[p] + prod[p][:, PAIR:] for p in pairs]
        else:
            x = [x[p] + _dot(lhs[p], x[p].astype(BF16)) for p in pairs]

    zero = jnp.zeros((PAIR, PAIR), BF16)
    for p in pairs:
        sl = sls[p]
        rhs = jnp.concatenate([jnp.concatenate([zero, v_st[p]], axis=1), x[p].astype(BF16)], axis=0)
        w = _dot(wlhs[p], rhs)
        g_st = r_st[p] + w[0:PAIR, 0:PAIR]
        y0_st = w[0:PAIR, PAIR:P2]
        phi = jnp.where(eye, w_tot[:, sl], 0.0) + w[PAIR:P2, 0:PAIR]
        psi = w[PAIR:P2, PAIR:P2]
        out = _dot1(jnp.concatenate([g_st, phi], axis=0), h_sc[p])
        y_st = out[0:PAIR] + y0_st
        h_sc[p] = out[PAIR:P2] + psi
        y_ref[:, sl] = y_st[0:L] + y_st[L:2 * L]


def _wkv(first, r, v, kk, ew, kd, b, tri, *, reverse, npairs):
    R, RW = r.shape
    NC = R // CHUNK
    width = npairs * PAIR
    if reverse:
        blk = pl.BlockSpec((CHUNK, width), lambda g, c, f: (NC - 1 - c, g))
    else:
        blk = pl.BlockSpec((CHUNK, width), lambda g, c, f: (c, g))
    grid_spec = pltpu.PrefetchScalarGridSpec(
        num_scalar_prefetch=1,
        grid=(RW // width, NC),
        in_specs=[blk] * 6 + [pl.BlockSpec((CHUNK, CHUNK), lambda g, c, f: (0, 0))],
        out_specs=blk,
        scratch_shapes=[pltpu.VMEM((npairs, PAIR, PAIR), F32)],
    )
    return pl.pallas_call(
        functools.partial(_wkv_kernel, reverse=reverse, npairs=npairs),
        grid_spec=grid_spec,
        out_shape=jax.ShapeDtypeStruct((R, RW), F32),
        compiler_params=_cparams(("parallel", "arbitrary")),
        name="wkv_rev" if reverse else "wkv_fwd",
    )(first, r, v, kk, ew, kd, b, tri)


def _post_kernel(yf_ref, yb_ref, bonus_ref, g_ref, vec_ref, ones_ref, o_ref):
    ones = ones_ref[...]
    vec = vec_ref[...]
    y = yf_ref[...] + yb_ref[...]
    inv_n = 1.0 / RWKV_HEAD
    mu = _segsum(y, ones) * inv_n
    d = y - mu
    var = _segsum(d * d, ones) * inv_n
    yn = d * lax.rsqrt(var + GN_EPS) * vec[V_LNW:V_LNW + 1] + vec[V_LNB:V_LNB + 1]
    o_ref[...] = ((yn + bonus_ref[...]) * g_ref[...]).astype(o_ref.dtype)


def _rwkv_post(yf, yb, bonus, g, vec, ones_bd, tm, cn):
    R, RW = yf.shape
    blk = pl.BlockSpec((tm, cn), lambda i, j: (i, j))
    return pl.pallas_call(
        _post_kernel,
        grid=(R // tm, RW // cn),
        in_specs=[blk, blk, blk, blk, pl.BlockSpec((VEC_ROWS, cn), lambda i, j: (0, j)),
                  pl.BlockSpec((cn, cn), lambda i, j: (0, 0))],
        out_specs=blk,
        out_shape=jax.ShapeDtypeStruct((R, RW), BF16),
        compiler_params=_cparams(("parallel", "parallel")),
        name="rwkv_post",
    )(yf, yb, bonus, g, vec, ones_bd)


def _merge_kernel(a_ref, r_ref, wa_ref, wr_ref, ga_ref, gr_ref, o_ref):
    ya = _dot(a_ref[...], wa_ref[...])
    yr = _dot(r_ref[...], wr_ref[...])
    o_ref[...] = (ga_ref[...].astype(F32) * ya + gr_ref[...].astype(F32) * yr).astype(o_ref.dtype)


def _merge(attn, rwkv, wa, wr, gates, tm, tn):
    R = attn.shape[0]
    D = wa.shape[1]
    J = D // tn
    return pl.pallas_call(
        _merge_kernel,
        grid=(R // tm, J),
        in_specs=[
            pl.BlockSpec((tm, attn.shape[1]), lambda i, j: (i, 0)),
            pl.BlockSpec((tm, rwkv.shape[1]), lambda i, j: (i, 0)),
            pl.BlockSpec((wa.shape[0], tn), lambda i, j: (0, j)),
            pl.BlockSpec((wr.shape[0], tn), lambda i, j: (0, j)),
            pl.BlockSpec((tm, tn), lambda i, j: (i, j)),
            pl.BlockSpec((tm, tn), lambda i, j: (i, J + j)),
        ],
        out_specs=pl.BlockSpec((tm, tn), lambda i, j: (i, j)),
        out_shape=jax.ShapeDtypeStruct((R, D), BF16),
        compiler_params=_cparams(("parallel", "parallel")),
        name="merge",
    )(attn, rwkv, wa, wr, gates, gates)


def _router_kernel(x_ref, g_ref, w_ref, h_ref, aff_ref):
    x = x_ref[...]
    ms = jnp.mean(x * x, axis=-1, keepdims=True)
    h = x * lax.rsqrt(ms + NORM_EPS) * g_ref[...]
    h_ref[...] = h.astype(h_ref.dtype)
    logits = _dot3(h, w_ref[...])
    lane = lax.broadcasted_iota(jnp.int32, logits.shape, 1)
    logits = jnp.where(lane < N_EXPERTS, logits, NEG_BIG)
    e = jnp.exp(logits - jnp.max(logits, axis=-1, keepdims=True))
    aff_ref[...] = e / jnp.sum(e, axis=-1, keepdims=True)


def _router(x1, g, w_router_pad, tm):
    R, D = x1.shape
    return pl.pallas_call(
        _router_kernel,
        grid=(R // tm,),
        in_specs=[pl.BlockSpec((tm, D), lambda i: (i, 0)), pl.BlockSpec((1, D), lambda i: (0, 0)),
                  pl.BlockSpec((D, LANES), lambda i: (0, 0))],
        out_specs=[pl.BlockSpec((tm, D), lambda i: (i, 0)), pl.BlockSpec((tm, LANES), lambda i: (i, 0))],
        out_shape=[jax.ShapeDtypeStruct((R, D), BF16), jax.ShapeDtypeStruct((R, LANES), F32)],
        compiler_params=_cparams(("parallel",)),
        name="ffn_norm_router",
    )(x1, g.reshape(1, D), w_router_pad)


def _ffn_up_kernel(x_ref, wg_ref, wu_ref, o_ref):
    x = x_ref[...]
    hg = _dot(x, wg_ref[...].astype(BF16))
    hu = _dot(x, wu_ref[...].astype(BF16))
    o_ref[...] = (hg * _sigmoid(hg) * hu).astype(o_ref.dtype)


def _ffn_down_kernel(h_ref, wd_ref, gate_ref, o_ref):
    o_ref[...] = (_dot(h_ref[...], wd_ref[...].astype(BF16)) * gate_ref[...]).astype(o_ref.dtype)


def _expert_ffn(xs, gate, w_gate, w_up, w_down, tm, tf):
    E, D, F = w_gate.shape
    nt = xs.shape[0] // (E * tm)
    wspec = pl.BlockSpec((None, D, tf), lambda e, i, f: (e, 0, f))
    hid = pl.pallas_call(
        _ffn_up_kernel,
        grid=(E, nt, F // tf),
        in_specs=[pl.BlockSpec((tm, D), lambda e, i, f: (e * nt + i, 0)), wspec, wspec],
        out_specs=pl.BlockSpec((tm, tf), lambda e, i, f: (e * nt + i, f)),
        out_shape=jax.ShapeDtypeStruct((xs.shape[0], F), BF16),
        compiler_params=_cparams(("parallel", "parallel", "parallel")),
        name="ffn_up",
    )(xs, w_gate, w_up)
    return pl.pallas_call(
        _ffn_down_kernel,
        grid=(E, nt, D // tf),
        in_specs=[pl.BlockSpec((tm, F), lambda e, i, f: (e * nt + i, 0)),
                  pl.BlockSpec((None, F, tf), lambda e, i, f: (e, 0, f)),
                  pl.BlockSpec((tm, 1), lambda e, i, f: (e * nt + i, 0))],
        out_specs=pl.BlockSpec((tm, tf), lambda e, i, f: (e * nt + i, f)),
        out_shape=jax.ShapeDtypeStruct((xs.shape[0], D), BF16),
        compiler_params=_cparams(("parallel", "parallel", "parallel")),
        name="ffn_down",
    )(hid, w_down, gate)


COMBINE_CHUNK = 64
SLOT_ALIGN = 16
COMBINE_SPAN = COMBINE_CHUNK - SLOT_ALIGN


def _combine_kernel(lo_ref, hi_ref, npass_ref, out_hbm, tok_hbm, y_ref, buf, tbuf, sem, *, ns, tmc):
    i = pl.program_id(0)
    E, CH = N_EXPERTS, COMBINE_CHUNK
    base = i * tmc
    lane = lax.broadcasted_iota(jnp.int32, (CH, tmc), 1)
    srow = lax.broadcasted_iota(jnp.int32, (CH, 1), 0)
    y_ref[...] = jnp.zeros(y_ref.shape, F32)

    def one_pass(p, carry):
        copies, bounds = [], []
        for e in range(E):
            a = lo_ref[i * E + e] + p * COMBINE_SPAN
            b = jnp.minimum(a + COMBINE_SPAN, hi_ref[i * E + e])
            start = pl.multiple_of(jnp.clip((a // SLOT_ALIGN) * SLOT_ALIGN, 0, ns - CH), SLOT_ALIGN)
            rows = pl.ds(e * CH, CH)
            pair = (pltpu.make_async_copy(out_hbm.at[pl.ds(start, CH), :], buf.at[rows, :], sem.at[0, e]),
                    pltpu.make_async_copy(tok_hbm.at[pl.ds(start, CH), :], tbuf.at[rows, :], sem.at[1, e]))
            pair[0].start()
            pair[1].start()
            copies.append(pair)
            bounds.append((start, a, b))
        blocks = []
        for e in range(E):
            copies[e][0].wait()
            copies[e][1].wait()
            start, a, b = bounds[e]
            slot = start + srow
            hit = (tbuf[pl.ds(e * CH, CH), :] - base == lane) & (slot >= a) & (slot < b)
            blocks.append(jnp.where(hit, 1.0, 0.0))
        onehot = jnp.concatenate(blocks, axis=0).T.astype(BF16)
        y_ref[...] += _dot(onehot, buf[...])
        return carry

    lax.fori_loop(0, npass_ref[i], one_pass, 0)


def _combine(out, tok, lo, hi, npass, R, tmc):
    ns, D = out.shape
    grid_spec = pltpu.PrefetchScalarGridSpec(
        num_scalar_prefetch=3,
        grid=(R // tmc,),
        in_specs=[pl.BlockSpec(memory_space=pl.ANY), pl.BlockSpec(memory_space=pl.ANY)],
        out_specs=pl.BlockSpec((tmc, D), lambda i, lo, hi, n: (i, 0)),
        scratch_shapes=[pltpu.VMEM((N_EXPERTS * COMBINE_CHUNK, D), BF16),
                        pltpu.VMEM((N_EXPERTS * COMBINE_CHUNK, 1), jnp.int32),
                        pltpu.SemaphoreType.DMA((2, N_EXPERTS))],
    )
    return pl.pallas_call(
        functools.partial(_combine_kernel, ns=ns, tmc=tmc),
        grid_spec=grid_spec,
        out_shape=jax.ShapeDtypeStruct((R, D), F32),
        compiler_params=_cparams(("arbitrary",)),
        name="expert_combine",
    )(lo, hi, npass, out, tok)


def _final_kernel(x_ref, y_ref, g_ref, o_ref):
    x = x_ref[...] + y_ref[...]
    ms = jnp.mean(x * x, axis=-1, keepdims=True)
    o_ref[...] = x * lax.rsqrt(ms + NORM_EPS) * g_ref[...]


def _final(x1, y, g, B, Np, off):
    D = x1.shape[1]
    nb = Np // LANES
    base = off // LANES
    blk = pl.BlockSpec((LANES, D), lambda b, j: (base + b * nb + 1 + j, 0))
    return pl.pallas_call(
        _final_kernel,
        grid=(B, nb - 1),
        in_specs=[blk, blk, pl.BlockSpec((1, D), lambda b, j: (0, 0))],
        out_specs=pl.BlockSpec((None, LANES, D), lambda b, j: (b, j, 0)),
        out_shape=jax.ShapeDtypeStruct((B, Np - LANES, D), F32),
        compiler_params=_cparams(("parallel", "parallel")),
        name="final_norm",
    )(x1, y, g.reshape(1, D))


def _pack_group(x, meta):
    B, S, D = x.shape
    z = jnp.zeros((B, FRONT, D), x.dtype)
    m = jnp.broadcast_to(meta.astype(x.dtype)[None], (B, N_META, D))
    return jnp.concatenate([z, m, x], axis=1).reshape(B * (S + LANES), D)


def _rope_tables(S, B):
    rows = S // GRID_W
    row_ids = jnp.repeat(jnp.arange(rows, dtype=F32), GRID_W)
    col_ids = jnp.tile(jnp.arange(GRID_W, dtype=F32), rows)
    half = HEAD_DIM // 2
    inv_freq = 1.0 / (ROPE_THETA ** (jnp.arange(0, half, 2, dtype=F32) / half))
    ang = jnp.concatenate([row_ids[:, None] * inv_freq, col_ids[:, None] * inv_freq], axis=-1)
    ang = jnp.concatenate([jnp.zeros((LANES, half), F32), ang], axis=0)
    c, s = jnp.cos(ang), jnp.sin(ang)
    return (jnp.tile(jnp.concatenate([c, c], axis=-1), (B, 1)),
            jnp.tile(jnp.concatenate([-s, s], axis=-1), (B, 1)))


def _key_bias(Np):
    col = np.where(np.arange(Np) < FRONT, NEG_BIG, 0.0).astype(np.float32)
    return jnp.asarray(np.repeat(col[:, None], LANES, axis=1))


def _seq_flags(groups):
    fwd = []
    for B, Np in groups:
        nc = Np // CHUNK
        for _ in range(B):
            fwd += [1] + [0] * (nc - 1)
    fwd = np.asarray(fwd, np.int32)
    last = np.roll(fwd, -1)
    return jnp.asarray(fwd), jnp.asarray(last[::-1].copy())


def kernel(x_prompt, x_sample, meta_tokens, norm_mix, w_in, q_norm, k_norm, shift_prev, shift_next, decay_up, decay_base, iclr_up, iclr_base, gate_up, k_k, k_a, r_k, ln_x_w, ln_x_b, w_branch_attn, w_branch_rwkv, w_out, norm_ffn, w_router, w_gate, w_up, w_down, norm_final):
    assert norm_mix.shape[0] == 1, "one layer"
    D = x_prompt.shape[-1]
    RW = D // 2
    groups = [(x.shape[0], x.shape[1] + LANES) for x in (x_prompt, x_sample)]
    seqs = [x.shape[1] for x in (x_prompt, x_sample)]
    assert all(s % LANES == 0 for s in seqs)
    offs = [0, groups[0][0] * groups[0][1]]
    r_used = offs[1] + groups[1][0] * groups[1][1]
    tm = 768 if r_used >= 8 * 768 else LANES
    tme = tm // 3 if tm % 3 == 0 else tm
    tmr = tm // 2 if tm % 256 == 0 else tm
    R = -(-r_used // tm) * tm
    tail = R - r_used

    x = jnp.concatenate([_pack_group(x_prompt, meta_tokens), _pack_group(x_sample, meta_tokens),
                         jnp.zeros((tail, D), x_prompt.dtype)], axis=0)
    tabs = [_rope_tables(S, B) for S, (B, _) in zip(seqs, groups)]
    cos_t = jnp.concatenate([t[0] for t in tabs] + [jnp.zeros((tail, HEAD_DIM), F32)], axis=0)
    sin_t = jnp.concatenate([t[1] for t in tabs] + [jnp.zeros((tail, HEAD_DIM), F32)], axis=0)

    w_in0 = w_in[0]
    c0 = 0
    cols = {}
    for name, width in (("q", ATTN_WIDTH), ("k", KV_WIDTH), ("v", KV_WIDTH), ("rkv", 3 * RW),
                        ("lora", DECAY_LORA + ICLR_LORA + GATE_LORA), ("merge", 2 * D)):
        cols[name] = w_in0[:, c0:c0 + width].astype(BF16)
        c0 += width
    w_lora = jnp.pad(cols["lora"], ((0, 0), (0, LORA_PAD - cols["lora"].shape[1])))

    h = _rmsnorm(x, norm_mix[0], tme, BF16)
    rope_specs = lambda g: [(g.reshape(1, HEAD_DIM), pl.BlockSpec((1, HEAD_DIM), lambda i, j: (0, 0))),
                            (cos_t, pl.BlockSpec((tm, HEAD_DIM), lambda i, j: (i, 0))),
                            (sin_t, pl.BlockSpec((tm, HEAD_DIM), lambda i, j: (i, 0)))]
    q_scale = HEAD_DIM ** -0.5 * math.log2(math.e)
    q = _matmul(h, cols["q"], tm=tm, out_dtype=BF16, name="proj_q",
                epilogue=functools.partial(_qk_epilogue, scale=q_scale), extras=rope_specs(q_norm[0]))
    k = _matmul(h, cols["k"], tm=tm, out_dtype=BF16, name="proj_k",
                epilogue=functools.partial(_qk_epilogue, scale=1.0), extras=rope_specs(k_norm[0]))
    v = _matmul(h, cols["v"], tm=tm, out_dtype=BF16, name="proj_v")
    rkv = _matmul(h, cols["rkv"], tm=tm, out_dtype=F32, name="proj_rkv")
    lora = _matmul(h, w_lora, tm=tm, tn=LORA_PAD, out_dtype=F32, name="proj_lora")
    gates = _matmul(h, cols["merge"], tm=tm, out_dtype=BF16, name="proj_merge_gates",
                    epilogue=_sigmoid_epilogue)

    attn_parts = []
    for (B, Np), off in zip(groups, offs):
        sl = slice(off, off + B * Np)
        attn_parts.append(_attention(q[sl], k[sl], v[sl].T, _key_bias(Np), B, Np))
    attn = jnp.concatenate(attn_parts + [jnp.zeros((tail, ATTN_WIDTH), BF16)], axis=0)

    cn = _pick(RW, (256, 128))
    vec = jnp.concatenate([shift_prev[0], shift_next[0], decay_base[0], iclr_base[0], k_k, k_a, r_k,
                           ln_x_w, ln_x_b, jnp.zeros((VEC_ROWS - 15, RW), F32)], axis=0).astype(F32)
    hid_idx = np.arange(cn) // RWKV_HEAD
    ones_bd = jnp.asarray((hid_idx[:, None] == hid_idx[None, :]).astype(np.float32)).astype(BF16)
    gup = jnp.pad(gate_up[0], ((0, GATE_PAD - GATE_LORA), (0, 0))).astype(BF16)
    r_s, v_s, kk, ew0, kd0, b0, ew1, kd1, b1, bonus, g_rwkv = _rwkv_prep(
        rkv, lora, decay_up[0], iclr_up[0], gup, vec, ones_bd, tmr, cn)
    first_fwd, first_rev = _seq_flags(groups + ([(1, tail)] if tail else []))
    t_idx = np.arange(CHUNK)
    tri_f = jnp.asarray((t_idx[None, :] <= t_idx[:, None]).astype(np.float32)).astype(BF16)
    tri_r = jnp.asarray((t_idx[None, :] >= t_idx[:, None]).astype(np.float32)).astype(BF16)
    npairs = _pick(RW // PAIR, (16, 8, 4, 2, 1))
    y_f = _wkv(first_fwd, r_s, v_s, kk, ew0, kd0, b0, tri_f, reverse=False, npairs=npairs)
    y_b = _wkv(first_rev, r_s, v_s, kk, ew1, kd1, b1, tri_r, reverse=True, npairs=npairs)
    rwkv = _rwkv_post(y_f, y_b, bonus, g_rwkv, vec, ones_bd, tmr, cn)

    merged = _merge(attn, rwkv, w_branch_attn[0].astype(BF16), w_branch_rwkv[0].astype(BF16), gates, tm, _pick(D, (512, 256, 128)))
    x1 = _matmul(merged, w_out[0].astype(BF16), tm=tm, out_dtype=F32, name="out_proj",
                 epilogue=_residual_epilogue, extras=[(x, pl.BlockSpec((tm, _pick(D, (512, 384, 256, 128))), lambda i, j: (i, j)))])

    w_router_pad = jnp.pad(w_router[0], ((0, 0), (0, LANES - N_EXPERTS)))
    h2, aff = _router(x1, norm_ffn[0], w_router_pad, tme)
    aff = aff[:, :N_EXPERTS]
    idx_parts, gate_parts = [], []
    for (B, Np), off in zip(groups, offs):
        valid = jnp.asarray((np.arange(B * Np) % Np) >= FRONT)
        a_g = jnp.where(valid[:, None], aff[off:off + B * Np], -1.0)
        cap = CAPACITY_FACTOR * (B * (Np - FRONT)) // N_EXPERTS
        gate_g, idx_g = lax.top_k(a_g.T, cap)
        idx_g, gate_g = lax.sort((idx_g, gate_g), dimension=1, num_keys=1)
        idx_parts.append(idx_g + off)
        gate_parts.append(gate_g)
    idx = jnp.concatenate(idx_parts, axis=1)
    gate = jnp.concatenate(gate_parts, axis=1)
    c_tot = idx.shape[1]
    nt = -(-c_tot // 1040)
    tmx = -(-(-(-c_tot // nt)) // SLOT_ALIGN) * SLOT_ALIGN
    ct = nt * tmx
    tok = jnp.pad(idx, ((0, 0), (0, ct - c_tot)), constant_values=R)
    gate = jnp.pad(gate, ((0, 0), (0, ct - c_tot))).reshape(-1, 1)
    xs = jnp.take(h2, jnp.where(tok < R, tok, 0).reshape(-1), axis=0)
    tf = _pick(D, (256, 128))
    out = _expert_ffn(xs, gate, w_gate[0], w_up[0], w_down[0], tmx, tf)

    tmc = _pick(R, (256, LANES))
    edges = jnp.arange(R // tmc + 1, dtype=jnp.int32) * tmc
    pos = jax.vmap(lambda t: jnp.searchsorted(t, edges, side="left"))(tok).astype(jnp.int32)
    pos = pos + (jnp.arange(N_EXPERTS, dtype=jnp.int32) * ct)[:, None]
    lo, hi = pos[:, :-1].T, pos[:, 1:].T
    npass = jnp.max(-(-(hi - lo) // COMBINE_SPAN), axis=1).astype(jnp.int32)
    y = _combine(out, tok.reshape(-1, 1).astype(jnp.int32), lo.reshape(-1), hi.reshape(-1), npass, R, tmc)

    return tuple(_final(x1, y, norm_final, B, Np, off) for (B, Np), off in zip(groups, offs))
```

```python
import functools
import math

import numpy as np
import jax
import jax.numpy as jnp
from jax import lax
from jax.experimental import pallas as pl
from jax.experimental.pallas import tpu as pltpu

F32 = jnp.float32
BF16 = jnp.bfloat16

N_META = 16
GRID_W = 64
HEAD_DIM = 128
N_Q_HEADS = 16
N_KV_HEADS = 4
Q_PER_KV = N_Q_HEADS // N_KV_HEADS
ATTN_WIDTH = N_Q_HEADS * HEAD_DIM
KV_WIDTH = N_KV_HEADS * HEAD_DIM
ROPE_THETA = 10000.0
RWKV_HEAD = 64
DECAY_LORA = 128
ICLR_LORA = 128
GATE_LORA = 480
N_EXPERTS = 16
CAPACITY_FACTOR = 2
NORM_EPS = 1e-6
GN_EPS = 64e-5

LANES = 128
FRONT = LANES - N_META
CHUNK = 64
PAIR = 2 * RWKV_HEAD
LORA_PAD = 768
GATE_PAD = LORA_PAD - DECAY_LORA - ICLR_LORA
VMEM_LIMIT = 56 * 1024 * 1024
NEG_BIG = -1e30


def _cparams(sem):
    return pltpu.CompilerParams(dimension_semantics=sem, vmem_limit_bytes=VMEM_LIMIT)


def _pick(n, cands):
    for c in cands:
        if n % c == 0:
            return c
    raise ValueError(f"no tile for {n} in {cands}")


def _split2(x):
    hi = x.astype(BF16)
    lo = (x - hi.astype(F32)).astype(BF16)
    return hi, lo


def _split3(x):
    hi = x.astype(BF16)
    r1 = x - hi.astype(F32)
    mid = r1.astype(BF16)
    lo = (r1 - mid.astype(F32)).astype(BF16)
    return hi, mid, lo


def _dot(a, b):
    return jnp.dot(a, b, preferred_element_type=F32)


def _dot_nt(a, b):
    return lax.dot_general(a, b, (((1,), (1,)), ((), ())), preferred_element_type=F32)


def _dot1(a, b):
    return _dot(a.astype(BF16), b.astype(BF16))


def _dot3(a, b):
    ah, al = _split2(a)
    bh, bl = _split2(b)
    return _dot(ah, bh) + (_dot(ah, bl) + _dot(al, bh))


def _dot_exact_lhs(a_bf16, b):
    hi, mid, lo = _split3(b)
    return _dot(a_bf16, hi) + (_dot(a_bf16, mid) + _dot(a_bf16, lo))


def _segsum(x, ones_bd):
    hi, lo = _split2(x)
    return _dot(hi, ones_bd) + _dot(lo, ones_bd)


def _sigmoid(x):
    return 1.0 / (1.0 + jnp.exp(-x))


BLK_TAIL, BLK_FRONT, BLK_PROMPT, BLK_SAMPLE = 0, 1, 2, 3


def _pack_norm_kernel(kind_ref, pb_ref, sb_ref, xp_ref, xs_ref, meta_ref, g_ref, x_o, h_o):
    kind = kind_ref[pl.program_id(0)]

    def emit(x):
        x_o[...] = x
        ms = jnp.mean(x * x, axis=-1, keepdims=True)
        h_o[...] = (x * lax.rsqrt(ms + NORM_EPS) * g_ref[...]).astype(h_o.dtype)

    @pl.when(kind == BLK_TAIL)
    def _():
        x_o[...] = jnp.zeros(x_o.shape, x_o.dtype)
        h_o[...] = jnp.zeros(h_o.shape, h_o.dtype)

    @pl.when(kind == BLK_FRONT)
    def _():
        emit(jnp.concatenate([jnp.zeros((FRONT, x_o.shape[1]), F32), meta_ref[...]], axis=0))

    pl.when(kind == BLK_PROMPT)(lambda: emit(xp_ref[...]))
    pl.when(kind == BLK_SAMPLE)(lambda: emit(xs_ref[...]))


def _pack_norm(x_prompt, x_sample, meta, g, R):
    D = x_prompt.shape[-1]
    kind, pb, sb = [], [], []
    for which, x in ((BLK_PROMPT, x_prompt), (BLK_SAMPLE, x_sample)):
        B, S = x.shape[0], x.shape[1]
        for b in range(B):
            for j in range(S // LANES + 1):
                kind.append(BLK_FRONT if j == 0 else which)
                src = b * (S // LANES) + max(j - 1, 0)
                pb.append(src if which == BLK_PROMPT else (pb[-1] if pb else 0))
                sb.append(src if which == BLK_SAMPLE else 0)
    n_blk = R // LANES
    kind += [BLK_TAIL] * (n_blk - len(kind))
    pb += [pb[-1]] * (n_blk - len(pb))
    sb += [sb[-1]] * (n_blk - len(sb))
    tables = [jnp.asarray(np.asarray(t, np.int32)) for t in (kind, pb, sb)]
    blk = pl.BlockSpec((LANES, D), lambda i, kind, pb, sb: (i, 0))
    grid_spec = pltpu.PrefetchScalarGridSpec(
        num_scalar_prefetch=3,
        grid=(n_blk,),
        in_specs=[pl.BlockSpec((LANES, D), lambda i, kind, pb, sb: (pb[i], 0)),
                  pl.BlockSpec((LANES, D), lambda i, kind, pb, sb: (sb[i], 0)),
                  pl.BlockSpec((N_META, D), lambda i, kind, pb, sb: (0, 0)),
                  pl.BlockSpec((1, D), lambda i, kind, pb, sb: (0, 0))],
        out_specs=[blk, blk],
    )
    return pl.pallas_call(
        _pack_norm_kernel,
        grid_spec=grid_spec,
        out_shape=[jax.ShapeDtypeStruct((R, D), F32), jax.ShapeDtypeStruct((R, D), BF16)],
        compiler_params=_cparams(("arbitrary",)),
        name="pack_rmsnorm",
    )(*tables, x_prompt.reshape(-1, D), x_sample.reshape(-1, D), meta.astype(F32), g.reshape(1, D))


def _mm_kernel(*refs, epilogue, n_extra):
    x_ref, w_ref = refs[0], refs[1]
    extras = refs[2:2 + n_extra]
    o_ref = refs[2 + n_extra]
    acc = _dot(x_ref[...], w_ref[...])
    if epilogue is not None:
        acc = epilogue(acc, *extras)
    o_ref[...] = acc.astype(o_ref.dtype)


def _matmul(x, w, *, tm, out_dtype, name, tn=None, epilogue=None, extras=()):
    M, K = x.shape
    N = w.shape[1]
    tn = tn or _pick(N, (512, 384, 256, 128))
    in_specs = [pl.BlockSpec((tm, K), lambda i, j: (i, 0)), pl.BlockSpec((K, tn), lambda i, j: (0, j))]
    in_specs += [spec for _, spec in extras]
    return pl.pallas_call(
        functools.partial(_mm_kernel, epilogue=epilogue, n_extra=len(extras)),
        grid=(M // tm, N // tn),
        in_specs=in_specs,
        out_specs=pl.BlockSpec((tm, tn), lambda i, j: (i, j)),
        out_shape=jax.ShapeDtypeStruct((M, N), out_dtype),
        compiler_params=_cparams(("parallel", "parallel")),
        name=name,
    )(x, w, *[a for a, _ in extras])


def _qk_epilogue(acc, g_ref, c_ref, s_ref, *, scale):
    g = g_ref[...]
    c = c_ref[...]
    s = s_ref[...]
    outs = []
    for h in range(acc.shape[1] // HEAD_DIM):
        y = acc[:, h * HEAD_DIM:(h + 1) * HEAD_DIM]
        y = y * lax.rsqrt(jnp.mean(y * y, axis=-1, keepdims=True) + NORM_EPS) * g
        y = y * c + pltpu.roll(y, HEAD_DIM // 2, 1) * s
        outs.append(y * scale if scale != 1.0 else y)
    return jnp.concatenate(outs, axis=1) if len(outs) > 1 else outs[0]


def _sigmoid_epilogue(acc):
    return _sigmoid(acc)


def _residual_epilogue(acc, x_ref):
    return x_ref[...] + acc


def _attn_kernel(q_ref, k_ref, vt_ref, bias_ref, o_ref, m_sc, l_sc, acc_sc, *, tq, nk):
    j = pl.program_id(3)
    M = Q_PER_KV * tq

    @pl.when(j == 0)
    def _():
        m_sc[...] = jnp.full(m_sc.shape, NEG_BIG, F32)
        l_sc[...] = jnp.zeros(l_sc.shape, F32)
        acc_sc[...] = jnp.zeros(acc_sc.shape, F32)

    q = q_ref[...]
    q4 = jnp.concatenate([q[:, h * HEAD_DIM:(h + 1) * HEAD_DIM] for h in range(Q_PER_KV)], axis=0)

    def step(use_bias):
        s = _dot_nt(k_ref[...], q4)
        if use_bias:
            s = s + jnp.tile(bias_ref[...], (1, M // LANES))
        m_prev = m_sc[...]
        m_new = jnp.maximum(m_prev, jnp.max(s, axis=0, keepdims=True))
        alpha = jnp.exp2(m_prev - m_new)
        p = jnp.exp2(s - m_new)
        l_sc[...] = alpha * l_sc[...] + jnp.sum(p, axis=0, keepdims=True)
        acc_sc[...] = alpha * acc_sc[...] + _dot(vt_ref[...], p.astype(BF16))
        m_sc[...] = m_new

    if nk == 1:
        step(True)
    else:
        pl.when(j == 0)(functools.partial(step, True))
        pl.when(j > 0)(functools.partial(step, False))

    @pl.when(j == nk - 1)
    def _():
        o = (acc_sc[...] / l_sc[...]).T
        o_ref[...] = jnp.concatenate([o[h * tq:(h + 1) * tq] for h in range(Q_PER_KV)], axis=1).astype(o_ref.dtype)


def _attention(q, k, vt, bias, B, Np):
    tq = _pick(Np, (640, 512, 384, 256, 128))
    tk = Np if Np <= 2304 else _pick(Np, (1024, 896, 768, 640, 512, 384, 256, 128))
    nq, nk = Np // tq, Np // tk
    M = Q_PER_KV * tq
    return pl.pallas_call(
        functools.partial(_attn_kernel, tq=tq, nk=nk),
        grid=(B, N_KV_HEADS, nq, nk),
        in_specs=[
            pl.BlockSpec((tq, Q_PER_KV * HEAD_DIM), lambda b, g, i, j: (b * nq + i, g)),
            pl.BlockSpec((tk, HEAD_DIM), lambda b, g, i, j: (b * nk + j, g)),
            pl.BlockSpec((HEAD_DIM, tk), lambda b, g, i, j: (g, b * nk + j)),
            pl.BlockSpec((tk, LANES), lambda b, g, i, j: (j, 0)),
        ],
        out_specs=pl.BlockSpec((tq, Q_PER_KV * HEAD_DIM), lambda b, g, i, j: (b * nq + i, g)),
        out_shape=jax.ShapeDtypeStruct((B * Np, ATTN_WIDTH), BF16),
        scratch_shapes=[
            pltpu.VMEM((1, M), F32),
            pltpu.VMEM((1, M), F32),
            pltpu.VMEM((HEAD_DIM, M), F32),
        ],
        compiler_params=_cparams(("parallel", "parallel", "parallel", "arbitrary")),
        name="attention",
    )(q, k, vt, bias)


V_SP, V_SN, V_DBASE, V_IBASE, V_KK, V_KA, V_RK, V_LNW, V_LNB = 0, 3, 6, 8, 10, 11, 12, 13, 14
VEC_ROWS = 16
HALO = 8


def _prep_kernel(r_ref, k_ref, v_ref, rp_ref, rn_ref, kp_ref, kn_ref, vp_ref, vn_ref, lora_ref,
                 dup0_ref, dup1_ref, iup0_ref, iup1_ref, gup_ref, vec_ref, ones_ref,
                 r_o, v_o, kk_o, ew0_o, kd0_o, b0_o, ew1_o, kd1_o, b1_o, bonus_o, g_o, *, tm, n_row_tiles):
    i = pl.program_id(0)
    has_prev = (i > 0).astype(F32)
    has_next = (i < n_row_tiles - 1).astype(F32)
    vec = vec_ref[...]
    row = lax.broadcasted_iota(jnp.int32, (tm, 1), 0)

    def shifted(x_ref, p_ref, n_ref, idx):
        x = x_ref[...]
        prev = jnp.where(row == 0, p_ref[HALO - 1:HALO, :] * has_prev, pltpu.roll(x, 1, 0))
        nxt = jnp.where(row == tm - 1, n_ref[0:1, :] * has_next, pltpu.roll(x, tm - 1, 0))
        return x + vec[V_SP + idx:V_SP + idx + 1] * (prev - x) + vec[V_SN + idx:V_SN + idx + 1] * (nxt - x)

    r = shifted(r_ref, rp_ref, rn_ref, 0)
    k = shifted(k_ref, kp_ref, kn_ref, 1)
    v = shifted(v_ref, vp_ref, vn_ref, 2)
    ones = ones_ref[...]

    lora = lora_ref[...]
    decay_h = jnp.tanh(lora[:, 0:DECAY_LORA])
    iclr_h = lora[:, DECAY_LORA:DECAY_LORA + ICLR_LORA]
    gate_h = _sigmoid(lora[:, DECAY_LORA + ICLR_LORA:LORA_PAD])

    kkr = k * vec[V_KK:V_KK + 1]
    kk = kkr / jnp.maximum(jnp.sqrt(_segsum(kkr * kkr, ones)), 1e-12)
    r_o[...] = r.astype(r_o.dtype)
    v_o[...] = v.astype(v_o.dtype)
    kk_o[...] = kk.astype(kk_o.dtype)
    bonus_o[...] = (_segsum(r * k * vec[V_RK:V_RK + 1], ones) * v).astype(bonus_o.dtype)
    g_o[...] = _dot(gate_h.astype(BF16), gup_ref[...]).astype(g_o.dtype)

    for d, (dup_ref, iup_ref, ew_o, kd_o, b_o) in enumerate(
            ((dup0_ref, iup0_ref, ew0_o, kd0_o, b0_o), (dup1_ref, iup1_ref, ew1_o, kd1_o, b1_o))):
        u = -(vec[V_DBASE + d:V_DBASE + d + 1] + _dot1(decay_h, dup_ref[...]))
        softplus = jnp.maximum(u, 0.0) + jnp.log(1.0 + jnp.exp(-jnp.abs(u)))
        ew_o[...] = jnp.exp(-softplus - 0.5)
        a = _sigmoid(vec[V_IBASE + d:V_IBASE + d + 1] + _dot1(iclr_h, iup_ref[...]))
        kd_o[...] = (k * (1.0 + (a - 1.0) * vec[V_KA:V_KA + 1])).astype(kd_o.dtype)
        b_o[...] = (kk * a).astype(b_o.dtype)


def _rwkv_prep(rkv, lora, dup, iup, gup, vec, ones_bd, tm, cn):
    R = rkv.shape[0]
    RW = rkv.shape[1] // 3
    J = RW // cn
    n_row_tiles = R // tm
    hb = tm // HALO
    last_hb = R // HALO - 1

    def main(c):
        return pl.BlockSpec((tm, cn), lambda i, j: (i, c * J + j))

    def prev(c):
        return pl.BlockSpec((HALO, cn), lambda i, j: (jnp.maximum(i * hb - 1, 0), c * J + j))

    def nxt(c):
        return pl.BlockSpec((HALO, cn), lambda i, j: (jnp.minimum((i + 1) * hb, last_hb), c * J + j))

    up = pl.BlockSpec((DECAY_LORA, cn), lambda i, j: (0, j))
    in_specs = [main(0), main(1), main(2), prev(0), nxt(0), prev(1), nxt(1), prev(2), nxt(2),
                pl.BlockSpec((tm, LORA_PAD), lambda i, j: (i, 0)),
                up, up, up, up,
                pl.BlockSpec((GATE_PAD, cn), lambda i, j: (0, j)),
                pl.BlockSpec((VEC_ROWS, cn), lambda i, j: (0, j)),
                pl.BlockSpec((cn, cn), lambda i, j: (0, 0))]
    out_spec = pl.BlockSpec((tm, cn), lambda i, j: (i, j))
    out_dtypes = [BF16, BF16, BF16, F32, BF16, BF16, F32, BF16, BF16, BF16, BF16]
    return pl.pallas_call(
        functools.partial(_prep_kernel, tm=tm, n_row_tiles=n_row_tiles),
        grid=(n_row_tiles, J),
        in_specs=in_specs,
        out_specs=[out_spec] * len(out_dtypes),
        out_shape=[jax.ShapeDtypeStruct((R, RW), dt) for dt in out_dtypes],
        compiler_params=_cparams(("parallel", "parallel")),
        name="rwkv_prep",
    )(rkv, rkv, rkv, rkv, rkv, rkv, rkv, rkv, rkv, lora, dup[0], dup[1], iup[0], iup[1], gup, vec, ones_bd)


def _wkv_kernel(first_ref, r_ref, v_ref, kk_ref, ew_ref, kd_ref, b_ref, tri_ref, y_ref, h_sc, *, reverse, npairs):
    c = pl.program_id(1)

    @pl.when(first_ref[c] == 1)
    def _():
        h_sc[...] = jnp.zeros(h_sc.shape, F32)

    L = CHUNK
    ew = ew_ref[...]
    cs = _dot_exact_lhs(tri_ref[...], ew)
    tot = cs[0:1] if reverse else cs[L - 1:L]
    e_neg = jnp.exp(-cs)
    e_prev = jnp.exp(ew - cs)
    e_pos = jnp.exp(cs)
    e_fin = jnp.exp(cs - tot)
    w_tot = jnp.exp(-tot)

    ri = lax.broadcasted_iota(jnp.int32, (PAIR, PAIR), 0)
    ci = lax.broadcasted_iota(jnp.int32, (PAIR, PAIR), 1)
    same = (ri // L) == (ci // L)
    t_i = ri % L
    s_i = ci % L
    if reverse:
        strict = same & (s_i > t_i)
        incl = same & (s_i >= t_i)
    else:
        strict = same & (s_i < t_i)
        incl = same & (s_i <= t_i)
    eye = ri == ci
    head0 = lax.broadcasted_iota(jnp.int32, (L, PAIR), 1) < RWKV_HEAD

    def stack(x):
        return jnp.concatenate([jnp.where(head0, x, 0.0), jnp.where(head0, 0.0, x)], axis=0)

    P2 = 2 * PAIR
    pairs = range(npairs)
    sls = [slice(p * PAIR, (p + 1) * PAIR) for p in pairs]

    r_st, a_st, v_st, bk2, kf_t, bf_t = [], [], [], [], [], []
    for sl in sls:
        kd = kd_ref[:, sl].astype(F32)
        b = b_ref[:, sl].astype(F32)
        bt = (b * e_pos[:, sl]).astype(BF16)
        kt = (kd * e_pos[:, sl]).astype(BF16)
        r_st.append(stack(r_ref[:, sl].astype(F32) * e_neg[:, sl]))
        a_st.append(stack(-kk_ref[:, sl].astype(F32) * e_prev[:, sl]))
        v_st.append(stack(v_ref[:, sl].astype(F32)).astype(BF16))
        bk2.append(jnp.concatenate([bt, bt, kt, kt], axis=0))
        kf_t.append(stack(kd * e_fin[:, sl]).T)
        bf_t.append(stack(b * e_fin[:, sl]).T)

    a_ab, a_ak, wlhs = [], [], []
    for p in pairs:
        sc = _dot_nt(jnp.concatenate([a_st[p], r_st[p]], axis=0).astype(BF16), bk2[p])
        a_ab.append(jnp.where(strict, sc[0:PAIR, 0:PAIR], 0.0))
        a_ak.append(jnp.where(strict, sc[0:PAIR, PAIR:P2], 0.0).astype(BF16))
        m_rb = jnp.where(incl, sc[PAIR:P2, 0:PAIR], 0.0)
        m_rk = jnp.where(incl, sc[PAIR:P2, PAIR:P2], 0.0)
        wlhs.append(jnp.concatenate([jnp.concatenate([m_rk, m_rb], axis=1),
                                     jnp.concatenate([kf_t[p], bf_t[p]], axis=1)], axis=0).astype(BF16))

    x = [jnp.concatenate([a_st[p], _dot(a_ak[p], v_st[p])], axis=1) for p in pairs]
    npow = a_ab
    n_sq = int(math.log2(L)) - 1
    for it in range(n_sq + 1):
        lhs = [npow[p].astype(BF16) for p in pairs]
        if it < n_sq:
            prod = [_dot(lhs[p], jnp.concatenate([lhs[p], x[p].astype(BF16)], axis=1)) for p in pairs]
            npow = [prod[p][:, 0:PAIR] for p in pairs]
            x = [x[p] + prod[p][:, PAIR:] for p in pairs]
        else:
            x = [x[p] + _dot(lhs[p], x[p].astype(BF16)) for p in pairs]

    zero = jnp.zeros((PAIR, PAIR), BF16)
    for p in pairs:
        sl = sls[p]
        rhs = jnp.concatenate([jnp.concatenate([zero, v_st[p]], axis=1), x[p].astype(BF16)], axis=0)
        w = _dot(wlhs[p], rhs)
        g_st = r_st[p] + w[0:PAIR, 0:PAIR]
        y0_st = w[0:PAIR, PAIR:P2]
        phi = jnp.where(eye, w_tot[:, sl], 0.0) + w[PAIR:P2, 0:PAIR]
        psi = w[PAIR:P2, PAIR:P2]
        out = _dot1(jnp.concatenate([g_st, phi], axis=0), h_sc[p])
        y_st = out[0:PAIR] + y0_st
        h_sc[p] = out[PAIR:P2] + psi
        y_ref[:, sl] = y_st[0:L] + y_st[L:2 * L]


def _wkv(first, r, v, kk, ew, kd, b, tri, *, reverse, npairs):
    R, RW = r.shape
    NC = R // CHUNK
    width = npairs * PAIR
    if reverse:
        blk = pl.BlockSpec((CHUNK, width), lambda g, c, f: (NC - 1 - c, g))
    else:
        blk = pl.BlockSpec((CHUNK, width), lambda g, c, f: (c, g))
    grid_spec = pltpu.PrefetchScalarGridSpec(
        num_scalar_prefetch=1,
        grid=(RW // width, NC),
        in_specs=[blk] * 6 + [pl.BlockSpec((CHUNK, CHUNK), lambda g, c, f: (0, 0))],
        out_specs=blk,
        scratch_shapes=[pltpu.VMEM((npairs, PAIR, PAIR), F32)],
    )
    return pl.pallas_call(
        functools.partial(_wkv_kernel, reverse=reverse, npairs=npairs),
        grid_spec=grid_spec,
        out_shape=jax.ShapeDtypeStruct((R, RW), F32),
        compiler_params=_cparams(("parallel", "arbitrary")),
        name="wkv_rev" if reverse else "wkv_fwd",
    )(first, r, v, kk, ew, kd, b, tri)


def _post_kernel(yf_ref, yb_ref, bonus_ref, g_ref, vec_ref, ones_ref, o_ref):
    ones = ones_ref[...]
    vec = vec_ref[...]
    y = yf_ref[...] + yb_ref[...]
    inv_n = 1.0 / RWKV_HEAD
    mu = _segsum(y, ones) * inv_n
    d = y - mu
    var = _segsum(d * d, ones) * inv_n
    yn = d * lax.rsqrt(var + GN_EPS) * vec[V_LNW:V_LNW + 1] + vec[V_LNB:V_LNB + 1]
    o_ref[...] = ((yn + bonus_ref[...]) * g_ref[...]).astype(o_ref.dtype)


def _rwkv_post(yf, yb, bonus, g, vec, ones_bd, tm, cn):
    R, RW = yf.shape
    blk = pl.BlockSpec((tm, cn), lambda i, j: (i, j))
    return pl.pallas_call(
        _post_kernel,
        grid=(R // tm, RW // cn),
        in_specs=[blk, blk, blk, blk, pl.BlockSpec((VEC_ROWS, cn), lambda i, j: (0, j)),
                  pl.BlockSpec((cn, cn), lambda i, j: (0, 0))],
        out_specs=blk,
        out_shape=jax.ShapeDtypeStruct((R, RW), BF16),
        compiler_params=_cparams(("parallel", "parallel")),
        name="rwkv_post",
    )(yf, yb, bonus, g, vec, ones_bd)


def _merge_kernel(a_ref, r_ref, wa_ref, wr_ref, ga_ref, gr_ref, o_ref):
    ya = _dot(a_ref[...], wa_ref[...])
    yr = _dot(r_ref[...], wr_ref[...])
    o_ref[...] = (ga_ref[...].astype(F32) * ya + gr_ref[...].astype(F32) * yr).astype(o_ref.dtype)


def _merge(attn, rwkv, wa, wr, gates, tm, tn):
    R = attn.shape[0]
    D = wa.shape[1]
    J = D // tn
    return pl.pallas_call(
        _merge_kernel,
        grid=(R // tm, J),
        in_specs=[
            pl.BlockSpec((tm, attn.shape[1]), lambda i, j: (i, 0)),
            pl.BlockSpec((tm, rwkv.shape[1]), lambda i, j: (i, 0)),
            pl.BlockSpec((wa.shape[0], tn), lambda i, j: (0, j)),
            pl.BlockSpec((wr.shape[0], tn), lambda i, j: (0, j)),
            pl.BlockSpec((tm, tn), lambda i, j: (i, j)),
            pl.BlockSpec((tm, tn), lambda i, j: (i, J + j)),
        ],
        out_specs=pl.BlockSpec((tm, tn), lambda i, j: (i, j)),
        out_shape=jax.ShapeDtypeStruct((R, D), BF16),
        compiler_params=_cparams(("parallel", "parallel")),
        name="merge",
    )(attn, rwkv, wa, wr, gates, gates)


def _router_kernel(x_ref, g_ref, w_ref, h_ref, aff_ref):
    x = x_ref[...]
    ms = jnp.mean(x * x, axis=-1, keepdims=True)
    h = x * lax.rsqrt(ms + NORM_EPS) * g_ref[...]
    h_ref[...] = h.astype(h_ref.dtype)
    logits = _dot3(h, w_ref[...])
    lane = lax.broadcasted_iota(jnp.int32, logits.shape, 1)
    logits = jnp.where(lane < N_EXPERTS, logits, NEG_BIG)
    e = jnp.exp(logits - jnp.max(logits, axis=-1, keepdims=True))
    aff_ref[...] = e / jnp.sum(e, axis=-1, keepdims=True)


def _router(x1, g, w_router_pad, tm):
    R, D = x1.shape
    return pl.pallas_call(
        _router_kernel,
        grid=(R // tm,),
        in_specs=[pl.BlockSpec((tm, D), lambda i: (i, 0)), pl.BlockSpec((1, D), lambda i: (0, 0)),
                  pl.BlockSpec((D, LANES), lambda i: (0, 0))],
        out_specs=[pl.BlockSpec((tm, D), lambda i: (i, 0)), pl.BlockSpec((tm, LANES), lambda i: (i, 0))],
        out_shape=[jax.ShapeDtypeStruct((R, D), BF16), jax.ShapeDtypeStruct((R, LANES), F32)],
        compiler_params=_cparams(("parallel",)),
        name="ffn_norm_router",
    )(x1, g.reshape(1, D), w_router_pad)


def _ffn_up_kernel(x_ref, wg_ref, wu_ref, o_ref):
    x = x_ref[...]
    hg = _dot(x, wg_ref[...].astype(BF16))
    hu = _dot(x, wu_ref[...].astype(BF16))
    o_ref[...] = (hg * _sigmoid(hg) * hu).astype(o_ref.dtype)


def _ffn_down_kernel(h_ref, wd_ref, gate_ref, o_ref):
    o_ref[...] = (_dot(h_ref[...], wd_ref[...].astype(BF16)) * gate_ref[...]).astype(o_ref.dtype)


def _expert_ffn(xs, gate, w_gate, w_up, w_down, tm, tf):
    E, D, F = w_gate.shape
    nt = xs.shape[0] // (E * tm)
    wspec = pl.BlockSpec((None, D, tf), lambda e, i, f: (e, 0, f))
    hid = pl.pallas_call(
        _ffn_up_kernel,
        grid=(E, nt, F // tf),
        in_specs=[pl.BlockSpec((tm, D), lambda e, i, f: (e * nt + i, 0)), wspec, wspec],
        out_specs=pl.BlockSpec((tm, tf), lambda e, i, f: (e * nt + i, f)),
        out_shape=jax.ShapeDtypeStruct((xs.shape[0], F), BF16),
        compiler_params=_cparams(("parallel", "parallel", "parallel")),
        name="ffn_up",
    )(xs, w_gate, w_up)
    return pl.pallas_call(
        _ffn_down_kernel,
        grid=(E, nt, D // tf),
        in_specs=[pl.BlockSpec((tm, F), lambda e, i, f: (e * nt + i, 0)),
                  pl.BlockSpec((None, F, tf), lambda e, i, f: (e, 0, f)),
                  pl.BlockSpec((tm, 1), lambda e, i, f: (e * nt + i, 0))],
        out_specs=pl.BlockSpec((tm, tf), lambda e, i, f: (e * nt + i, f)),
        out_shape=jax.ShapeDtypeStruct((xs.shape[0], D), BF16),
        compiler_params=_cparams(("parallel", "parallel", "parallel")),
        name="ffn_down",
    )(hid, w_down, gate)


COMBINE_CHUNK = 64
SLOT_ALIGN = 16
COMBINE_SPAN = COMBINE_CHUNK - SLOT_ALIGN


def _combine_kernel(lo_ref, hi_ref, npass_ref, out_hbm, tok_hbm, y_ref, buf, tbuf, sem, *, ns, tmc):
    i = pl.program_id(0)
    E, CH = N_EXPERTS, COMBINE_CHUNK
    base = i * tmc
    lane = lax.broadcasted_iota(jnp.int32, (CH, tmc), 1)
    srow = lax.broadcasted_iota(jnp.int32, (CH, 1), 0)
    y_ref[...] = jnp.zeros(y_ref.shape, F32)

    def one_pass(p, carry):
        copies, bounds = [], []
        for e in range(E):
            a = lo_ref[i * E + e] + p * COMBINE_SPAN
            b = jnp.minimum(a + COMBINE_SPAN, hi_ref[i * E + e])
            start = pl.multiple_of(jnp.clip((a // SLOT_ALIGN) * SLOT_ALIGN, 0, ns - CH), SLOT_ALIGN)
            rows = pl.ds(e * CH, CH)
            pair = (pltpu.make_async_copy(out_hbm.at[pl.ds(start, CH), :], buf.at[rows, :], sem.at[0, e]),
                    pltpu.make_async_copy(tok_hbm.at[pl.ds(start, CH), :], tbuf.at[rows, :], sem.at[1, e]))
            pair[0].start()
            pair[1].start()
            copies.append(pair)
            bounds.append((start, a, b))
        blocks = []
        for e in range(E):
            copies[e][0].wait()
            copies[e][1].wait()
            start, a, b = bounds[e]
            slot = start + srow
            hit = (tbuf[pl.ds(e * CH, CH), :] - base == lane) & (slot >= a) & (slot < b)
            blocks.append(jnp.where(hit, 1.0, 0.0))
        onehot = jnp.concatenate(blocks, axis=0).T.astype(BF16)
        y_ref[...] += _dot(onehot, buf[...])
        return carry

    lax.fori_loop(0, npass_ref[i], one_pass, 0)


def _combine(out, tok, lo, hi, npass, R, tmc):
    ns, D = out.shape
    grid_spec = pltpu.PrefetchScalarGridSpec(
        num_scalar_prefetch=3,
        grid=(R // tmc,),
        in_specs=[pl.BlockSpec(memory_space=pl.ANY), pl.BlockSpec(memory_space=pl.ANY)],
        out_specs=pl.BlockSpec((tmc, D), lambda i, lo, hi, n: (i, 0)),
        scratch_shapes=[pltpu.VMEM((N_EXPERTS * COMBINE_CHUNK, D), BF16),
                        pltpu.VMEM((N_EXPERTS * COMBINE_CHUNK, 1), jnp.int32),
                        pltpu.SemaphoreType.DMA((2, N_EXPERTS))],
    )
    return pl.pallas_call(
        functools.partial(_combine_kernel, ns=ns, tmc=tmc),
        grid_spec=grid_spec,
        out_shape=jax.ShapeDtypeStruct((R, D), F32),
        compiler_params=_cparams(("arbitrary",)),
        name="expert_combine",
    )(lo, hi, npass, out, tok)


def _final_kernel(x_ref, y_ref, g_ref, o_ref):
    x = x_ref[...] + y_ref[...]
    ms = jnp.mean(x * x, axis=-1, keepdims=True)
    o_ref[...] = x * lax.rsqrt(ms + NORM_EPS) * g_ref[...]


def _final(x1, y, g, B, Np, off):
    D = x1.shape[1]
    nb = Np // LANES
    base = off // LANES
    blk = pl.BlockSpec((LANES, D), lambda b, j: (base + b * nb + 1 + j, 0))
    return pl.pallas_call(
        _final_kernel,
        grid=(B, nb - 1),
        in_specs=[blk, blk, pl.BlockSpec((1, D), lambda b, j: (0, 0))],
        out_specs=pl.BlockSpec((None, LANES, D), lambda b, j: (b, j, 0)),
        out_shape=jax.ShapeDtypeStruct((B, Np - LANES, D), F32),
        compiler_params=_cparams(("parallel", "parallel")),
        name="final_norm",
    )(x1, y, g.reshape(1, D))


def _rope_tables(S, B):
    rows = S // GRID_W
    row_ids = jnp.repeat(jnp.arange(rows, dtype=F32), GRID_W)
    col_ids = jnp.tile(jnp.arange(GRID_W, dtype=F32), rows)
    half = HEAD_DIM // 2
    inv_freq = 1.0 / (ROPE_THETA ** (jnp.arange(0, half, 2, dtype=F32) / half))
    ang = jnp.concatenate([row_ids[:, None] * inv_freq, col_ids[:, None] * inv_freq], axis=-1)
    ang = jnp.concatenate([jnp.zeros((LANES, half), F32), ang], axis=0)
    c, s = jnp.cos(ang), jnp.sin(ang)
    return (jnp.tile(jnp.concatenate([c, c], axis=-1), (B, 1)),
            jnp.tile(jnp.concatenate([-s, s], axis=-1), (B, 1)))


def _key_bias(Np):
    col = np.where(np.arange(Np) < FRONT, NEG_BIG, 0.0).astype(np.float32)
    return jnp.asarray(np.repeat(col[:, None], LANES, axis=1))


def _seq_flags(groups):
    fwd = []
    for B, Np in groups:
        nc = Np // CHUNK
        for _ in range(B):
            fwd += [1] + [0] * (nc - 1)
    fwd = np.asarray(fwd, np.int32)
    last = np.roll(fwd, -1)
    return jnp.asarray(fwd), jnp.asarray(last[::-1].copy())


def kernel(x_prompt, x_sample, meta_tokens, norm_mix, w_in, q_norm, k_norm, shift_prev, shift_next, decay_up, decay_base, iclr_up, iclr_base, gate_up, k_k, k_a, r_k, ln_x_w, ln_x_b, w_branch_attn, w_branch_rwkv, w_out, norm_ffn, w_router, w_gate, w_up, w_down, norm_final):
    assert norm_mix.shape[0] == 1, "one layer"
    D = x_prompt.shape[-1]
    RW = D // 2
    groups = [(x.shape[0], x.shape[1] + LANES) for x in (x_prompt, x_sample)]
    seqs = [x.shape[1] for x in (x_prompt, x_sample)]
    assert all(s % LANES == 0 for s in seqs)
    offs = [0, groups[0][0] * groups[0][1]]
    r_used = offs[1] + groups[1][0] * groups[1][1]
    tm = 768 if r_used >= 8 * 768 else LANES
    tme = tm // 3 if tm % 3 == 0 else tm
    tmr = tm // 2 if tm % 256 == 0 else tm
    R = -(-r_used // tm) * tm
    tail = R - r_used

    x, h = _pack_norm(x_prompt, x_sample, meta_tokens, norm_mix[0], R)
    tabs = [_rope_tables(S, B) for S, (B, _) in zip(seqs, groups)]
    cos_t = jnp.concatenate([t[0] for t in tabs] + [jnp.zeros((tail, HEAD_DIM), F32)], axis=0)
    sin_t = jnp.concatenate([t[1] for t in tabs] + [jnp.zeros((tail, HEAD_DIM), F32)], axis=0)

    w_in0 = w_in[0]
    c0 = 0
    cols = {}
    for name, width in (("q", ATTN_WIDTH), ("k", KV_WIDTH), ("v", KV_WIDTH), ("rkv", 3 * RW),
                        ("lora", DECAY_LORA + ICLR_LORA + GATE_LORA), ("merge", 2 * D)):
        cols[name] = w_in0[:, c0:c0 + width].astype(BF16)
        c0 += width
    w_lora = jnp.pad(cols["lora"], ((0, 0), (0, LORA_PAD - cols["lora"].shape[1])))

    rope_specs = lambda g: [(g.reshape(1, HEAD_DIM), pl.BlockSpec((1, HEAD_DIM), lambda i, j: (0, 0))),
                            (cos_t, pl.BlockSpec((tm, HEAD_DIM), lambda i, j: (i, 0))),
                            (sin_t, pl.BlockSpec((tm, HEAD_DIM), lambda i, j: (i, 0)))]
    q_scale = HEAD_DIM ** -0.5 * math.log2(math.e)
    q = _matmul(h, cols["q"], tm=tm, out_dtype=BF16, name="proj_q",
                epilogue=functools.partial(_qk_epilogue, scale=q_scale), extras=rope_specs(q_norm[0]))
    k = _matmul(h, cols["k"], tm=tm, out_dtype=BF16, name="proj_k",
                epilogue=functools.partial(_qk_epilogue, scale=1.0), extras=rope_specs(k_norm[0]))
    v = _matmul(h, cols["v"], tm=tm, out_dtype=BF16, name="proj_v")
    rkv = _matmul(h, cols["rkv"], tm=tm, out_dtype=F32, name="proj_rkv")
    lora = _matmul(h, w_lora, tm=tm, tn=LORA_PAD, out_dtype=F32, name="proj_lora")
    gates = _matmul(h, cols["merge"], tm=tm, out_dtype=BF16, name="proj_merge_gates",
                    epilogue=_sigmoid_epilogue)

    attn_parts = []
    for (B, Np), off in zip(groups, offs):
        sl = slice(off, off + B * Np)
        attn_parts.append(_attention(q[sl], k[sl], v[sl].T, _key_bias(Np), B, Np))
    attn = jnp.concatenate(attn_parts + [jnp.zeros((tail, ATTN_WIDTH), BF16)], axis=0)

    cn = _pick(RW, (256, 128))
    vec = jnp.concatenate([shift_prev[0], shift_next[0], decay_base[0], iclr_base[0], k_k, k_a, r_k,
                           ln_x_w, ln_x_b, jnp.zeros((VEC_ROWS - 15, RW), F32)], axis=0).astype(F32)
    hid_idx = np.arange(cn) // RWKV_HEAD
    ones_bd = jnp.asarray((hid_idx[:, None] == hid_idx[None, :]).astype(np.float32)).astype(BF16)
    gup = jnp.pad(gate_up[0], ((0, GATE_PAD - GATE_LORA), (0, 0))).astype(BF16)
    r_s, v_s, kk, ew0, kd0, b0, ew1, kd1, b1, bonus, g_rwkv = _rwkv_prep(
        rkv, lora, decay_up[0], iclr_up[0], gup, vec, ones_bd, tmr, cn)
    first_fwd, first_rev = _seq_flags(groups + ([(1, tail)] if tail else []))
    t_idx = np.arange(CHUNK)
    tri_f = jnp.asarray((t_idx[None, :] <= t_idx[:, None]).astype(np.float32)).astype(BF16)
    tri_r = jnp.asarray((t_idx[None, :] >= t_idx[:, None]).astype(np.float32)).astype(BF16)
    npairs = _pick(RW // PAIR, (16, 8, 4, 2, 1))
    y_f = _wkv(first_fwd, r_s, v_s, kk, ew0, kd0, b0, tri_f, reverse=False, npairs=npairs)
    y_b = _wkv(first_rev, r_s, v_s, kk, ew1, kd1, b1, tri_r, reverse=True, npairs=npairs)
    rwkv = _rwkv_post(y_f, y_b, bonus, g_rwkv, vec, ones_bd, tmr, cn)

    merged = _merge(attn, rwkv, w_branch_attn[0].astype(BF16), w_branch_rwkv[0].astype(BF16), gates, tm, _pick(D, (512, 256, 128)))
    x1 = _matmul(merged, w_out[0].astype(BF16), tm=tm, out_dtype=F32, name="out_proj",
                 epilogue=_residual_epilogue, extras=[(x, pl.BlockSpec((tm, _pick(D, (512, 384, 256, 128))), lambda i, j: (i, j)))])

    w_router_pad = jnp.pad(w_router[0], ((0, 0), (0, LANES - N_EXPERTS)))
    h2, aff = _router(x1, norm_ffn[0], w_router_pad, tme)
    aff = aff[:, :N_EXPERTS]
    idx_parts, gate_parts = [], []
    for (B, Np), off in zip(groups, offs):
        valid = jnp.asarray((np.arange(B * Np) % Np) >= FRONT)
        a_g = jnp.where(valid[:, None], aff[off:off + B * Np], -1.0)
        cap = CAPACITY_FACTOR * (B * (Np - FRONT)) // N_EXPERTS
        gate_g, idx_g = lax.top_k(a_g.T, cap)
        idx_g, gate_g = lax.sort((idx_g, gate_g), dimension=1, num_keys=1)
        idx_parts.append(idx_g + off)
        gate_parts.append(gate_g)
    idx = jnp.concatenate(idx_parts, axis=1)
    gate = jnp.concatenate(gate_parts, axis=1)
    c_tot = idx.shape[1]
    nt = -(-c_tot // 1040)
    tmx = -(-(-(-c_tot // nt)) // SLOT_ALIGN) * SLOT_ALIGN
    ct = nt * tmx
    tok = jnp.pad(idx, ((0, 0), (0, ct - c_tot)), constant_values=R)
    gate = jnp.pad(gate, ((0, 0), (0, ct - c_tot))).reshape(-1, 1)
    xs = jnp.take(h2, jnp.where(tok < R, tok, 0).reshape(-1), axis=0)
    tf = _pick(D, (256, 128))
    out = _expert_ffn(xs, gate, w_gate[0], w_up[0], w_down[0], tmx, tf)

    tmc = _pick(R, (256, LANES))
    edges = jnp.arange(R // tmc + 1, dtype=jnp.int32) * tmc
    pos = jnp.sum((tok[:, None, :] < edges[None, :, None]).astype(jnp.int32), axis=-1)
    pos = pos + (jnp.arange(N_EXPERTS, dtype=jnp.int32) * ct)[:, None]
    lo, hi = pos[:, :-1].T, pos[:, 1:].T
    npass = jnp.max(-(-(hi - lo) // COMBINE_SPAN), axis=1).astype(jnp.int32)
    y = _combine(out, tok.reshape(-1, 1).astype(jnp.int32), lo.reshape(-1), hi.reshape(-1), npass, R, tmc)

    return tuple(_final(x1, y, norm_final, B, Np, off) for (B, Np), off in zip(groups, offs))
```

```python
import functools
import math

import numpy as np
import jax
import jax.numpy as jnp
from jax import lax
from jax.experimental import pallas as pl
from jax.experimental.pallas import tpu as pltpu

F32 = jnp.float32
BF16 = jnp.bfloat16

N_META = 16
GRID_W = 64
HEAD_DIM = 128
N_Q_HEADS = 16
N_KV_HEADS = 4
Q_PER_KV = N_Q_HEADS // N_KV_HEADS
ATTN_WIDTH = N_Q_HEADS * HEAD_DIM
KV_WIDTH = N_KV_HEADS * HEAD_DIM
ROPE_THETA = 10000.0
RWKV_HEAD = 64
DECAY_LORA = 128
ICLR_LORA = 128
GATE_LORA = 480
N_EXPERTS = 16
CAPACITY_FACTOR = 2
NORM_EPS = 1e-6
GN_EPS = 64e-5

LANES = 128
FRONT = LANES - N_META
CHUNK = 64
PAIR = 2 * RWKV_HEAD
LORA_PAD = 768
GATE_PAD = LORA_PAD - DECAY_LORA - ICLR_LORA
VMEM_LIMIT = 56 * 1024 * 1024
NEG_BIG = -1e30


def _cparams(sem):
    return pltpu.CompilerParams(dimension_semantics=sem, vmem_limit_bytes=VMEM_LIMIT)


def _pick(n, cands):
    for c in cands:
        if n % c == 0:
            return c
    raise ValueError(f"no tile for {n} in {cands}")


def _split2(x):
    hi = x.astype(BF16)
    lo = (x - hi.astype(F32)).astype(BF16)
    return hi, lo


def _split3(x):
    hi = x.astype(BF16)
    r1 = x - hi.astype(F32)
    mid = r1.astype(BF16)
    lo = (r1 - mid.astype(F32)).astype(BF16)
    return hi, mid, lo


def _dot(a, b):
    return jnp.dot(a, b, preferred_element_type=F32)


def _dot_nt(a, b):
    return lax.dot_general(a, b, (((1,), (1,)), ((), ())), preferred_element_type=F32)


def _dot1(a, b):
    return _dot(a.astype(BF16), b.astype(BF16))


def _dot3(a, b):
    ah, al = _split2(a)
    bh, bl = _split2(b)
    return _dot(ah, bh) + (_dot(ah, bl) + _dot(al, bh))


def _dot_exact_lhs(a_bf16, b):
    hi, mid, lo = _split3(b)
    return _dot(a_bf16, hi) + (_dot(a_bf16, mid) + _dot(a_bf16, lo))


def _segsum(x, ones_bd):
    hi, lo = _split2(x)
    return _dot(hi, ones_bd) + _dot(lo, ones_bd)


def _sigmoid(x):
    return 1.0 / (1.0 + jnp.exp(-x))


BLK_TAIL, BLK_FRONT, BLK_PROMPT, BLK_SAMPLE = 0, 1, 2, 3


def _pack_norm_kernel(kind_ref, pb_ref, sb_ref, xp_ref, xs_ref, meta_ref, g_ref, x_o, h_o):
    kind = kind_ref[pl.program_id(0)]

    def emit(x):
        x_o[...] = x
        ms = jnp.mean(x * x, axis=-1, keepdims=True)
        h_o[...] = (x * lax.rsqrt(ms + NORM_EPS) * g_ref[...]).astype(h_o.dtype)

    @pl.when(kind == BLK_TAIL)
    def _():
        x_o[...] = jnp.zeros(x_o.shape, x_o.dtype)
        h_o[...] = jnp.zeros(h_o.shape, h_o.dtype)

    @pl.when(kind == BLK_FRONT)
    def _():
        emit(jnp.concatenate([jnp.zeros((FRONT, x_o.shape[1]), F32), meta_ref[...]], axis=0))

    pl.when(kind == BLK_PROMPT)(lambda: emit(xp_ref[...]))
    pl.when(kind == BLK_SAMPLE)(lambda: emit(xs_ref[...]))


def _pack_norm(x_prompt, x_sample, meta, g, R):
    D = x_prompt.shape[-1]
    kind, pb, sb = [], [], []
    for which, x in ((BLK_PROMPT, x_prompt), (BLK_SAMPLE, x_sample)):
        B, S = x.shape[0], x.shape[1]
        for b in range(B):
            for j in range(S // LANES + 1):
                kind.append(BLK_FRONT if j == 0 else which)
                src = b * (S // LANES) + max(j - 1, 0)
                pb.append(src if which == BLK_PROMPT else (pb[-1] if pb else 0))
                sb.append(src if which == BLK_SAMPLE else 0)
    n_blk = R // LANES
    kind += [BLK_TAIL] * (n_blk - len(kind))
    pb += [pb[-1]] * (n_blk - len(pb))
    sb += [sb[-1]] * (n_blk - len(sb))
    tables = [jnp.asarray(np.asarray(t, np.int32)) for t in (kind, pb, sb)]
    blk = pl.BlockSpec((LANES, D), lambda i, kind, pb, sb: (i, 0))
    grid_spec = pltpu.PrefetchScalarGridSpec(
        num_scalar_prefetch=3,
        grid=(n_blk,),
        in_specs=[pl.BlockSpec((LANES, D), lambda i, kind, pb, sb: (pb[i], 0)),
                  pl.BlockSpec((LANES, D), lambda i, kind, pb, sb: (sb[i], 0)),
                  pl.BlockSpec((N_META, D), lambda i, kind, pb, sb: (0, 0)),
                  pl.BlockSpec((1, D), lambda i, kind, pb, sb: (0, 0))],
        out_specs=[blk, blk],
    )
    return pl.pallas_call(
        _pack_norm_kernel,
        grid_spec=grid_spec,
        out_shape=[jax.ShapeDtypeStruct((R, D), F32), jax.ShapeDtypeStruct((R, D), BF16)],
        compiler_params=_cparams(("arbitrary",)),
        name="pack_rmsnorm",
    )(*tables, x_prompt.reshape(-1, D), x_sample.reshape(-1, D), meta.astype(F32), g.reshape(1, D))


def _mm_kernel(*refs, epilogue, n_extra):
    x_ref, w_ref = refs[0], refs[1]
    extras = refs[2:2 + n_extra]
    o_ref, w_sc = refs[2 + n_extra], refs[3 + n_extra]

    @pl.when(pl.program_id(1) == 0)
    def _():
        w_sc[...] = w_ref[...].astype(w_sc.dtype)

    acc = _dot(x_ref[...], w_sc[...])
    if epilogue is not None:
        acc = epilogue(acc, *extras)
    o_ref[...] = acc.astype(o_ref.dtype)


def _matmul(x, w, *, tm, out_dtype, name, col0=0, n_out=None, tn=None, epilogue=None, extras=()):
    M, K = x.shape
    N = n_out or w.shape[1]
    tn = tn or _pick(N, (512, 384, 256, 128))
    assert col0 % tn == 0 and N % tn == 0
    c0 = col0 // tn
    in_specs = [pl.BlockSpec((tm, K), lambda j, i: (i, 0)), pl.BlockSpec((K, tn), lambda j, i: (0, c0 + j))]
    in_specs += [pl.BlockSpec(shape, functools.partial(lambda j, i, fn: fn(i, j), fn=fn)) for _, shape, fn in extras]
    return pl.pallas_call(
        functools.partial(_mm_kernel, epilogue=epilogue, n_extra=len(extras)),
        grid=(N // tn, M // tm),
        in_specs=in_specs,
        out_specs=pl.BlockSpec((tm, tn), lambda j, i: (i, j)),
        out_shape=jax.ShapeDtypeStruct((M, N), out_dtype),
        scratch_shapes=[pltpu.VMEM((K, tn), BF16)],
        compiler_params=_cparams(("parallel", "arbitrary")),
        name=name,
    )(x, w, *[a for a, _, _ in extras])


def _qk_epilogue(acc, g_ref, c_ref, s_ref, *, scale):
    g = g_ref[...]
    c = c_ref[...]
    s = s_ref[...]
    outs = []
    for h in range(acc.shape[1] // HEAD_DIM):
        y = acc[:, h * HEAD_DIM:(h + 1) * HEAD_DIM]
        y = y * lax.rsqrt(jnp.mean(y * y, axis=-1, keepdims=True) + NORM_EPS) * g
        y = y * c + pltpu.roll(y, HEAD_DIM // 2, 1) * s
        outs.append(y * scale if scale != 1.0 else y)
    return jnp.concatenate(outs, axis=1) if len(outs) > 1 else outs[0]


def _sigmoid_epilogue(acc):
    return _sigmoid(acc)


def _residual_epilogue(acc, x_ref):
    return x_ref[...] + acc


def _attn_kernel(q_ref, k_ref, vt_ref, bias_ref, o_ref, m_sc, l_sc, acc_sc, *, tq, nk):
    j = pl.program_id(3)
    M = Q_PER_KV * tq

    @pl.when(j == 0)
    def _():
        m_sc[...] = jnp.full(m_sc.shape, NEG_BIG, F32)
        l_sc[...] = jnp.zeros(l_sc.shape, F32)
        acc_sc[...] = jnp.zeros(acc_sc.shape, F32)

    q = q_ref[...]
    q4 = jnp.concatenate([q[:, h * HEAD_DIM:(h + 1) * HEAD_DIM] for h in range(Q_PER_KV)], axis=0)

    def step(use_bias):
        s = _dot_nt(k_ref[...], q4)
        if use_bias:
            s = s + jnp.tile(bias_ref[...], (1, M // LANES))
        m_prev = m_sc[...]
        m_new = jnp.maximum(m_prev, jnp.max(s, axis=0, keepdims=True))
        alpha = jnp.exp2(m_prev - m_new)
        p = jnp.exp2(s - m_new)
        l_sc[...] = alpha * l_sc[...] + jnp.sum(p, axis=0, keepdims=True)
        acc_sc[...] = alpha * acc_sc[...] + _dot(vt_ref[...], p.astype(BF16))
        m_sc[...] = m_new

    if nk == 1:
        step(True)
    else:
        pl.when(j == 0)(functools.partial(step, True))
        pl.when(j > 0)(functools.partial(step, False))

    @pl.when(j == nk - 1)
    def _():
        o = (acc_sc[...] / l_sc[...]).T
        o_ref[...] = jnp.concatenate([o[h * tq:(h + 1) * tq] for h in range(Q_PER_KV)], axis=1).astype(o_ref.dtype)


def _attention(q, k, vt, bias, B, Np):
    tq = _pick(Np, (640, 512, 384, 256, 128))
    tk = Np if Np <= 2304 else _pick(Np, (1024, 896, 768, 640, 512, 384, 256, 128))
    nq, nk = Np // tq, Np // tk
    M = Q_PER_KV * tq
    return pl.pallas_call(
        functools.partial(_attn_kernel, tq=tq, nk=nk),
        grid=(B, N_KV_HEADS, nq, nk),
        in_specs=[
            pl.BlockSpec((tq, Q_PER_KV * HEAD_DIM), lambda b, g, i, j: (b * nq + i, g)),
            pl.BlockSpec((tk, HEAD_DIM), lambda b, g, i, j: (b * nk + j, g)),
            pl.BlockSpec((HEAD_DIM, tk), lambda b, g, i, j: (g, b * nk + j)),
            pl.BlockSpec((tk, LANES), lambda b, g, i, j: (j, 0)),
        ],
        out_specs=pl.BlockSpec((tq, Q_PER_KV * HEAD_DIM), lambda b, g, i, j: (b * nq + i, g)),
        out_shape=jax.ShapeDtypeStruct((B * Np, ATTN_WIDTH), BF16),
        scratch_shapes=[
            pltpu.VMEM((1, M), F32),
            pltpu.VMEM((1, M), F32),
            pltpu.VMEM((HEAD_DIM, M), F32),
        ],
        compiler_params=_cparams(("parallel", "parallel", "parallel", "arbitrary")),
        name="attention",
    )(q, k, vt, bias)


V_SP, V_SN, V_DBASE, V_IBASE, V_KK, V_KA, V_RK, V_LNW, V_LNB = 0, 3, 6, 8, 10, 11, 12, 13, 14
VEC_ROWS = 16
HALO = 8


def _prep_kernel(r_ref, k_ref, v_ref, rp_ref, rn_ref, kp_ref, kn_ref, vp_ref, vn_ref, lora_ref,
                 dup0_ref, dup1_ref, iup0_ref, iup1_ref, gup_ref, vec_ref, ones_ref,
                 r_o, v_o, kk_o, ew0_o, kd0_o, b0_o, ew1_o, kd1_o, b1_o, bonus_o, g_o, *, tm, n_row_tiles):
    i = pl.program_id(0)
    has_prev = (i > 0).astype(F32)
    has_next = (i < n_row_tiles - 1).astype(F32)
    vec = vec_ref[...]
    row = lax.broadcasted_iota(jnp.int32, (tm, 1), 0)

    def shifted(x_ref, p_ref, n_ref, idx):
        x = x_ref[...]
        prev = jnp.where(row == 0, p_ref[HALO - 1:HALO, :] * has_prev, pltpu.roll(x, 1, 0))
        nxt = jnp.where(row == tm - 1, n_ref[0:1, :] * has_next, pltpu.roll(x, tm - 1, 0))
        return x + vec[V_SP + idx:V_SP + idx + 1] * (prev - x) + vec[V_SN + idx:V_SN + idx + 1] * (nxt - x)

    r = shifted(r_ref, rp_ref, rn_ref, 0)
    k = shifted(k_ref, kp_ref, kn_ref, 1)
    v = shifted(v_ref, vp_ref, vn_ref, 2)
    ones = ones_ref[...]

    lora = lora_ref[...]
    decay_h = jnp.tanh(lora[:, 0:DECAY_LORA])
    iclr_h = lora[:, DECAY_LORA:DECAY_LORA + ICLR_LORA]
    gate_h = _sigmoid(lora[:, DECAY_LORA + ICLR_LORA:LORA_PAD])

    kkr = k * vec[V_KK:V_KK + 1]
    kk = kkr / jnp.maximum(jnp.sqrt(_segsum(kkr * kkr, ones)), 1e-12)
    r_o[...] = r.astype(r_o.dtype)
    v_o[...] = v.astype(v_o.dtype)
    kk_o[...] = kk.astype(kk_o.dtype)
    bonus_o[...] = (_segsum(r * k * vec[V_RK:V_RK + 1], ones) * v).astype(bonus_o.dtype)
    g_o[...] = _dot(gate_h.astype(BF16), gup_ref[...]).astype(g_o.dtype)

    for d, (dup_ref, iup_ref, ew_o, kd_o, b_o) in enumerate(
            ((dup0_ref, iup0_ref, ew0_o, kd0_o, b0_o), (dup1_ref, iup1_ref, ew1_o, kd1_o, b1_o))):
        u = -(vec[V_DBASE + d:V_DBASE + d + 1] + _dot1(decay_h, dup_ref[...]))
        softplus = jnp.maximum(u, 0.0) + jnp.log(1.0 + jnp.exp(-jnp.abs(u)))
        ew_o[...] = jnp.exp(-softplus - 0.5)
        a = _sigmoid(vec[V_IBASE + d:V_IBASE + d + 1] + _dot1(iclr_h, iup_ref[...]))
        kd_o[...] = (k * (1.0 + (a - 1.0) * vec[V_KA:V_KA + 1])).astype(kd_o.dtype)
        b_o[...] = (kk * a).astype(b_o.dtype)


def _rwkv_prep(rkv, lora, dup, iup, gup, vec, ones_bd, tm, cn):
    R = rkv.shape[0]
    RW = rkv.shape[1] // 3
    J = RW // cn
    n_row_tiles = R // tm
    hb = tm // HALO
    last_hb = R // HALO - 1

    def main(c):
        return pl.BlockSpec((tm, cn), lambda i, j: (i, c * J + j))

    def prev(c):
        return pl.BlockSpec((HALO, cn), lambda i, j: (jnp.maximum(i * hb - 1, 0), c * J + j))

    def nxt(c):
        return pl.BlockSpec((HALO, cn), lambda i, j: (jnp.minimum((i + 1) * hb, last_hb), c * J + j))

    up = pl.BlockSpec((DECAY_LORA, cn), lambda i, j: (0, j))
    in_specs = [main(0), main(1), main(2), prev(0), nxt(0), prev(1), nxt(1), prev(2), nxt(2),
                pl.BlockSpec((tm, LORA_PAD), lambda i, j: (i, 0)),
                up, up, up, up,
                pl.BlockSpec((GATE_PAD, cn), lambda i, j: (0, j)),
                pl.BlockSpec((VEC_ROWS, cn), lambda i, j: (0, j)),
                pl.BlockSpec((cn, cn), lambda i, j: (0, 0))]
    out_spec = pl.BlockSpec((tm, cn), lambda i, j: (i, j))
    out_dtypes = [BF16, BF16, BF16, F32, BF16, BF16, F32, BF16, BF16, BF16, BF16]
    return pl.pallas_call(
        functools.partial(_prep_kernel, tm=tm, n_row_tiles=n_row_tiles),
        grid=(n_row_tiles, J),
        in_specs=in_specs,
        out_specs=[out_spec] * len(out_dtypes),
        out_shape=[jax.ShapeDtypeStruct((R, RW), dt) for dt in out_dtypes],
        compiler_params=_cparams(("parallel", "parallel")),
        name="rwkv_prep",
    )(rkv, rkv, rkv, rkv, rkv, rkv, rkv, rkv, rkv, lora, dup[0], dup[1], iup[0], iup[1], gup, vec, ones_bd)


def _wkv_kernel(first_ref, r_ref, v_ref, kk_ref, ew_ref, kd_ref, b_ref, tri_ref, y_ref, h_sc, *, reverse, npairs):
    c = pl.program_id(1)

    @pl.when(first_ref[c] == 1)
    def _():
        h_sc[...] = jnp.zeros(h_sc.shape, F32)

    L = CHUNK
    ew = ew_ref[...]
    cs = _dot_exact_lhs(tri_ref[...], ew)
    tot = cs[0:1] if reverse else cs[L - 1:L]
    e_neg = jnp.exp(-cs)
    e_prev = jnp.exp(ew - cs)
    e_pos = jnp.exp(cs)
    e_fin = jnp.exp(cs - tot)
    w_tot = jnp.exp(-tot)

    ri = lax.broadcasted_iota(jnp.int32, (PAIR, PAIR), 0)
    ci = lax.broadcasted_iota(jnp.int32, (PAIR, PAIR), 1)
    same = (ri // L) == (ci // L)
    t_i = ri % L
    s_i = ci % L
    if reverse:
        strict = same & (s_i > t_i)
        incl = same & (s_i >= t_i)
    else:
        strict = same & (s_i < t_i)
        incl = same & (s_i <= t_i)
    eye = ri == ci
    head0 = lax.broadcasted_iota(jnp.int32, (L, PAIR), 1) < RWKV_HEAD

    def stack(x):
        return jnp.concatenate([jnp.where(head0, x, 0.0), jnp.where(head0, 0.0, x)], axis=0)

    P2 = 2 * PAIR
    pairs = range(npairs)
    sls = [slice(p * PAIR, (p + 1) * PAIR) for p in pairs]

    r_st, a_st, v_st, bk2, kf_t, bf_t = [], [], [], [], [], []
    for sl in sls:
        kd = kd_ref[:, sl].astype(F32)
        b = b_ref[:, sl].astype(F32)
        bt = (b * e_pos[:, sl]).astype(BF16)
        kt = (kd * e_pos[:, sl]).astype(BF16)
        r_st.append(stack(r_ref[:, sl].astype(F32) * e_neg[:, sl]))
        a_st.append(stack(-kk_ref[:, sl].astype(F32) * e_prev[:, sl]))
        v_st.append(stack(v_ref[:, sl].astype(F32)).astype(BF16))
        bk2.append(jnp.concatenate([bt, bt, kt, kt], axis=0))
        kf_t.append(stack(kd * e_fin[:, sl]).T)
        bf_t.append(stack(b * e_fin[:, sl]).T)

    a_ab, a_ak, wlhs = [], [], []
    for p in pairs:
        sc = _dot_nt(jnp.concatenate([a_st[p], r_st[p]], axis=0).astype(BF16), bk2[p])
        a_ab.append(jnp.where(strict, sc[0:PAIR, 0:PAIR], 0.0))
        a_ak.append(jnp.where(strict, sc[0:PAIR, PAIR:P2], 0.0).astype(BF16))
        m_rb = jnp.where(incl, sc[PAIR:P2, 0:PAIR], 0.0)
        m_rk = jnp.where(incl, sc[PAIR:P2, PAIR:P2], 0.0)
        wlhs.append(jnp.concatenate([jnp.concatenate([m_rk, m_rb], axis=1),
                                     jnp.concatenate([kf_t[p], bf_t[p]], axis=1)], axis=0).astype(BF16))

    x = [jnp.concatenate([a_st[p], _dot(a_ak[p], v_st[p])], axis=1) for p in pairs]
    npow = a_ab
    n_sq = int(math.log2(L)) - 1
    for it in range(n_sq + 1):
        lhs = [npow[p].astype(BF16) for p in pairs]
        if it < n_sq:
            prod = [_dot(lhs[p], jnp.concatenate([lhs[p], x[p].astype(BF16)], axis=1)) for p in pairs]
            npow = [prod[p][:, 0:PAIR] for p in pairs]
            x = [x[p] + prod[p][:, PAIR:] for p in pairs]
        else:
            x = [x[p] + _dot(lhs[p], x[p].astype(BF16)) for p in pairs]

    zero = jnp.zeros((PAIR, PAIR), BF16)
    for p in pairs:
        sl = sls[p]
        rhs = jnp.concatenate([jnp.concatenate([zero, v_st[p]], axis=1), x[p].astype(BF16)], axis=0)
        w = _dot(wlhs[p], rhs)
        g_st = r_st[p] + w[0:PAIR, 0:PAIR]
        y0_st = w[0:PAIR, PAIR:P2]
        phi = jnp.where(eye, w_tot[:, sl], 0.0) + w[PAIR:P2, 0:PAIR]
        psi = w[PAIR:P2, PAIR:P2]
        out = _dot1(jnp.concatenate([g_st, phi], axis=0), h_sc[p])
        y_st = out[0:PAIR] + y0_st
        h_sc[p] = out[PAIR:P2] + psi
        y_ref[:, sl] = y_st[0:L] + y_st[L:2 * L]


def _wkv(first, r, v, kk, ew, kd, b, tri, *, reverse, npairs):
    R, RW = r.shape
    NC = R // CHUNK
    width = npairs * PAIR
    if reverse:
        blk = pl.BlockSpec((CHUNK, width), lambda g, c, f: (NC - 1 - c, g))
    else:
        blk = pl.BlockSpec((CHUNK, width), lambda g, c, f: (c, g))
    grid_spec = pltpu.PrefetchScalarGridSpec(
        num_scalar_prefetch=1,
        grid=(RW // width, NC),
        in_specs=[blk] * 6 + [pl.BlockSpec((CHUNK, CHUNK), lambda g, c, f: (0, 0))],
        out_specs=blk,
        scratch_shapes=[pltpu.VMEM((npairs, PAIR, PAIR), F32)],
    )
    return pl.pallas_call(
        functools.partial(_wkv_kernel, reverse=reverse, npairs=npairs),
        grid_spec=grid_spec,
        out_shape=jax.ShapeDtypeStruct((R, RW), F32),
        compiler_params=_cparams(("parallel", "arbitrary")),
        name="wkv_rev" if reverse else "wkv_fwd",
    )(first, r, v, kk, ew, kd, b, tri)


def _post_kernel(yf_ref, yb_ref, bonus_ref, g_ref, vec_ref, ones_ref, o_ref):
    ones = ones_ref[...]
    vec = vec_ref[...]
    y = yf_ref[...] + yb_ref[...]
    inv_n = 1.0 / RWKV_HEAD
    mu = _segsum(y, ones) * inv_n
    d = y - mu
    var = _segsum(d * d, ones) * inv_n
    yn = d * lax.rsqrt(var + GN_EPS) * vec[V_LNW:V_LNW + 1] + vec[V_LNB:V_LNB + 1]
    o_ref[...] = ((yn + bonus_ref[...]) * g_ref[...]).astype(o_ref.dtype)


def _rwkv_post(yf, yb, bonus, g, vec, ones_bd, tm, cn):
    R, RW = yf.shape
    blk = pl.BlockSpec((tm, cn), lambda i, j: (i, j))
    return pl.pallas_call(
        _post_kernel,
        grid=(R // tm, RW // cn),
        in_specs=[blk, blk, blk, blk, pl.BlockSpec((VEC_ROWS, cn), lambda i, j: (0, j)),
                  pl.BlockSpec((cn, cn), lambda i, j: (0, 0))],
        out_specs=blk,
        out_shape=jax.ShapeDtypeStruct((R, RW), BF16),
        compiler_params=_cparams(("parallel", "parallel")),
        name="rwkv_post",
    )(yf, yb, bonus, g, vec, ones_bd)


def _merge_kernel(a_ref, r_ref, wa_ref, wr_ref, ga_ref, gr_ref, o_ref, wa_sc, wr_sc):
    @pl.when(pl.program_id(1) == 0)
    def _():
        wa_sc[...] = wa_ref[...].astype(wa_sc.dtype)
        wr_sc[...] = wr_ref[...].astype(wr_sc.dtype)

    ya = _dot(a_ref[...], wa_sc[...])
    yr = _dot(r_ref[...], wr_sc[...])
    o_ref[...] = (ga_ref[...].astype(F32) * ya + gr_ref[...].astype(F32) * yr).astype(o_ref.dtype)


def _merge(attn, rwkv, wa, wr, gates, tm, tn):
    R = attn.shape[0]
    D = wa.shape[1]
    J = D // tn
    return pl.pallas_call(
        _merge_kernel,
        grid=(J, R // tm),
        in_specs=[
            pl.BlockSpec((tm, attn.shape[1]), lambda j, i: (i, 0)),
            pl.BlockSpec((tm, rwkv.shape[1]), lambda j, i: (i, 0)),
            pl.BlockSpec((wa.shape[0], tn), lambda j, i: (0, j)),
            pl.BlockSpec((wr.shape[0], tn), lambda j, i: (0, j)),
            pl.BlockSpec((tm, tn), lambda j, i: (i, j)),
            pl.BlockSpec((tm, tn), lambda j, i: (i, J + j)),
        ],
        out_specs=pl.BlockSpec((tm, tn), lambda j, i: (i, j)),
        out_shape=jax.ShapeDtypeStruct((R, D), BF16),
        scratch_shapes=[pltpu.VMEM((wa.shape[0], tn), BF16), pltpu.VMEM((wr.shape[0], tn), BF16)],
        compiler_params=_cparams(("parallel", "arbitrary")),
        name="merge",
    )(attn, rwkv, wa, wr, gates, gates)


def _router_kernel(x_ref, g_ref, w_ref, h_ref, aff_ref):
    x = x_ref[...]
    ms = jnp.mean(x * x, axis=-1, keepdims=True)
    h = x * lax.rsqrt(ms + NORM_EPS) * g_ref[...]
    h_ref[...] = h.astype(h_ref.dtype)
    logits = _dot3(h, w_ref[...])
    lane = lax.broadcasted_iota(jnp.int32, logits.shape, 1)
    logits = jnp.where(lane < N_EXPERTS, logits, NEG_BIG)
    e = jnp.exp(logits - jnp.max(logits, axis=-1, keepdims=True))
    aff_ref[...] = e / jnp.sum(e, axis=-1, keepdims=True)


def _router(x1, g, w_router_pad, tm):
    R, D = x1.shape
    return pl.pallas_call(
        _router_kernel,
        grid=(R // tm,),
        in_specs=[pl.BlockSpec((tm, D), lambda i: (i, 0)), pl.BlockSpec((1, D), lambda i: (0, 0)),
                  pl.BlockSpec((D, LANES), lambda i: (0, 0))],
        out_specs=[pl.BlockSpec((tm, D), lambda i: (i, 0)), pl.BlockSpec((tm, LANES), lambda i: (i, 0))],
        out_shape=[jax.ShapeDtypeStruct((R, D), BF16), jax.ShapeDtypeStruct((R, LANES), F32)],
        compiler_params=_cparams(("parallel",)),
        name="ffn_norm_router",
    )(x1, g.reshape(1, D), w_router_pad)


def _ffn_up_kernel(x_ref, wg_ref, wu_ref, o_ref):
    x = x_ref[...]
    hg = _dot(x, wg_ref[...].astype(BF16))
    hu = _dot(x, wu_ref[...].astype(BF16))
    o_ref[...] = (hg * _sigmoid(hg) * hu).astype(o_ref.dtype)


def _ffn_down_kernel(h_ref, wd_ref, gate_ref, o_ref):
    o_ref[...] = (_dot(h_ref[...], wd_ref[...].astype(BF16)) * gate_ref[...]).astype(o_ref.dtype)


def _expert_ffn(xs, gate, w_gate, w_up, w_down, tm, tf):
    E, D, F = w_gate.shape
    nt = xs.shape[0] // (E * tm)
    wspec = pl.BlockSpec((None, D, tf), lambda e, i, f: (e, 0, f))
    hid = pl.pallas_call(
        _ffn_up_kernel,
        grid=(E, nt, F // tf),
        in_specs=[pl.BlockSpec((tm, D), lambda e, i, f: (e * nt + i, 0)), wspec, wspec],
        out_specs=pl.BlockSpec((tm, tf), lambda e, i, f: (e * nt + i, f)),
        out_shape=jax.ShapeDtypeStruct((xs.shape[0], F), BF16),
        compiler_params=_cparams(("parallel", "parallel", "parallel")),
        name="ffn_up",
    )(xs, w_gate, w_up)
    td = _pick(D, (2 * tf, tf))
    return pl.pallas_call(
        _ffn_down_kernel,
        grid=(E, nt, D // td),
        in_specs=[pl.BlockSpec((tm, F), lambda e, i, f: (e * nt + i, 0)),
                  pl.BlockSpec((None, F, td), lambda e, i, f: (e, 0, f)),
                  pl.BlockSpec((tm, 1), lambda e, i, f: (e * nt + i, 0))],
        out_specs=pl.BlockSpec((tm, td), lambda e, i, f: (e * nt + i, f)),
        out_shape=jax.ShapeDtypeStruct((xs.shape[0], D), BF16),
        compiler_params=_cparams(("parallel", "parallel", "parallel")),
        name="ffn_down",
    )(hid, w_down, gate)


COMBINE_CHUNK = 64
SLOT_ALIGN = 16
COMBINE_SPAN = COMBINE_CHUNK - SLOT_ALIGN


def _combine_kernel(lo_ref, hi_ref, npass_ref, out_hbm, tok_hbm, y_ref, buf, tbuf, sem, *, ns, tmc):
    i = pl.program_id(0)
    E, CH = N_EXPERTS, COMBINE_CHUNK
    base = i * tmc
    lane = lax.broadcasted_iota(jnp.int32, (CH, tmc), 1)
    srow = lax.broadcasted_iota(jnp.int32, (CH, 1), 0)
    y_ref[...] = jnp.zeros(y_ref.shape, F32)

    def one_pass(p, carry):
        copies, bounds = [], []
        for e in range(E):
            a = lo_ref[i * E + e] + p * COMBINE_SPAN
            b = jnp.minimum(a + COMBINE_SPAN, hi_ref[i * E + e])
            start = pl.multiple_of(jnp.clip((a // SLOT_ALIGN) * SLOT_ALIGN, 0, ns - CH), SLOT_ALIGN)
            rows = pl.ds(e * CH, CH)
            pair = (pltpu.make_async_copy(out_hbm.at[pl.ds(start, CH), :], buf.at[rows, :], sem.at[0, e]),
                    pltpu.make_async_copy(tok_hbm.at[pl.ds(start, CH), :], tbuf.at[rows, :], sem.at[1, e]))
            pair[0].start()
            pair[1].start()
            copies.append(pair)
            bounds.append((start, a, b))
        blocks = []
        for e in range(E):
            copies[e][0].wait()
            copies[e][1].wait()
            start, a, b = bounds[e]
            slot = start + srow
            hit = (tbuf[pl.ds(e * CH, CH), :] - base == lane) & (slot >= a) & (slot < b)
            blocks.append(jnp.where(hit, 1.0, 0.0))
        onehot = jnp.concatenate(blocks, axis=0).T.astype(BF16)
        y_ref[...] += _dot(onehot, buf[...])
        return carry

    lax.fori_loop(0, npass_ref[i], one_pass, 0)


def _combine(out, tok, lo, hi, npass, R, tmc):
    ns, D = out.shape
    grid_spec = pltpu.PrefetchScalarGridSpec(
        num_scalar_prefetch=3,
        grid=(R // tmc,),
        in_specs=[pl.BlockSpec(memory_space=pl.ANY), pl.BlockSpec(memory_space=pl.ANY)],
        out_specs=pl.BlockSpec((tmc, D), lambda i, lo, hi, n: (i, 0)),
        scratch_shapes=[pltpu.VMEM((N_EXPERTS * COMBINE_CHUNK, D), BF16),
                        pltpu.VMEM((N_EXPERTS * COMBINE_CHUNK, 1), jnp.int32),
                        pltpu.SemaphoreType.DMA((2, N_EXPERTS))],
    )
    return pl.pallas_call(
        functools.partial(_combine_kernel, ns=ns, tmc=tmc),
        grid_spec=grid_spec,
        out_shape=jax.ShapeDtypeStruct((R, D), F32),
        compiler_params=_cparams(("arbitrary",)),
        name="expert_combine",
    )(lo, hi, npass, out, tok)


def _final_kernel(x_ref, y_ref, g_ref, o_ref):
    x = x_ref[...] + y_ref[...]
    ms = jnp.mean(x * x, axis=-1, keepdims=True)
    o_ref[...] = x * lax.rsqrt(ms + NORM_EPS) * g_ref[...]


def _final(x1, y, g, B, Np, off):
    D = x1.shape[1]
    nb = Np // LANES
    base = off // LANES
    blk = pl.BlockSpec((LANES, D), lambda b, j: (base + b * nb + 1 + j, 0))
    return pl.pallas_call(
        _final_kernel,
        grid=(B, nb - 1),
        in_specs=[blk, blk, pl.BlockSpec((1, D), lambda b, j: (0, 0))],
        out_specs=pl.BlockSpec((None, LANES, D), lambda b, j: (b, j, 0)),
        out_shape=jax.ShapeDtypeStruct((B, Np - LANES, D), F32),
        compiler_params=_cparams(("parallel", "parallel")),
        name="final_norm",
    )(x1, y, g.reshape(1, D))


def _rope_tables(S, B):
    rows = S // GRID_W
    row_ids = jnp.repeat(jnp.arange(rows, dtype=F32), GRID_W)
    col_ids = jnp.tile(jnp.arange(GRID_W, dtype=F32), rows)
    half = HEAD_DIM // 2
    inv_freq = 1.0 / (ROPE_THETA ** (jnp.arange(0, half, 2, dtype=F32) / half))
    ang = jnp.concatenate([row_ids[:, None] * inv_freq, col_ids[:, None] * inv_freq], axis=-1)
    ang = jnp.concatenate([jnp.zeros((LANES, half), F32), ang], axis=0)
    c, s = jnp.cos(ang), jnp.sin(ang)
    return (jnp.tile(jnp.concatenate([c, c], axis=-1), (B, 1)),
            jnp.tile(jnp.concatenate([-s, s], axis=-1), (B, 1)))


def _key_bias(Np):
    col = np.where(np.arange(Np) < FRONT, NEG_BIG, 0.0).astype(np.float32)
    return jnp.asarray(np.repeat(col[:, None], LANES, axis=1))


def _seq_flags(groups):
    fwd = []
    for B, Np in groups:
        nc = Np // CHUNK
        for _ in range(B):
            fwd += [1] + [0] * (nc - 1)
    fwd = np.asarray(fwd, np.int32)
    last = np.roll(fwd, -1)
    return jnp.asarray(fwd), jnp.asarray(last[::-1].copy())


def kernel(x_prompt, x_sample, meta_tokens, norm_mix, w_in, q_norm, k_norm, shift_prev, shift_next, decay_up, decay_base, iclr_up, iclr_base, gate_up, k_k, k_a, r_k, ln_x_w, ln_x_b, w_branch_attn, w_branch_rwkv, w_out, norm_ffn, w_router, w_gate, w_up, w_down, norm_final):
    assert norm_mix.shape[0] == 1, "one layer"
    D = x_prompt.shape[-1]
    RW = D // 2
    groups = [(x.shape[0], x.shape[1] + LANES) for x in (x_prompt, x_sample)]
    seqs = [x.shape[1] for x in (x_prompt, x_sample)]
    assert all(s % LANES == 0 for s in seqs)
    offs = [0, groups[0][0] * groups[0][1]]
    r_used = offs[1] + groups[1][0] * groups[1][1]
    tm = 768 if r_used >= 8 * 768 else LANES
    tme = tm // 3 if tm % 3 == 0 else tm
    tmr = tm // 2 if tm % 256 == 0 else tm
    R = -(-r_used // tm) * tm
    tail = R - r_used

    x, h = _pack_norm(x_prompt, x_sample, meta_tokens, norm_mix[0], R)
    tabs = [_rope_tables(S, B) for S, (B, _) in zip(seqs, groups)]
    cos_t = jnp.concatenate([t[0] for t in tabs] + [jnp.zeros((tail, HEAD_DIM), F32)], axis=0)
    sin_t = jnp.concatenate([t[1] for t in tabs] + [jnp.zeros((tail, HEAD_DIM), F32)], axis=0)

    w_in0 = w_in[0]
    n_in = w_in0.shape[1]

    def in_proj(col0, n_out, **kw):
        tn = _pick(n_out, (512, 384, 256, 128))
        if col0 % tn == 0 and col0 + n_out <= n_in:
            return _matmul(h, w_in0, col0=col0, n_out=n_out, tn=tn, tm=tm, **kw)
        w_cols = w_in0[:, col0:min(col0 + n_out, n_in)]
        w_cols = jnp.pad(w_cols, ((0, 0), (0, n_out - w_cols.shape[1])))
        return _matmul(h, w_cols, tn=tn, tm=tm, **kw)

    rope_specs = lambda g: [(g.reshape(1, HEAD_DIM), (1, HEAD_DIM), lambda i, j: (0, 0)),
                            (cos_t, (tm, HEAD_DIM), lambda i, j: (i, 0)),
                            (sin_t, (tm, HEAD_DIM), lambda i, j: (i, 0))]
    q_scale = HEAD_DIM ** -0.5 * math.log2(math.e)
    c_k = ATTN_WIDTH
    c_v = c_k + KV_WIDTH
    c_rkv = c_v + KV_WIDTH
    c_lora = c_rkv + 3 * RW
    c_merge = c_lora + DECAY_LORA + ICLR_LORA + GATE_LORA
    q = in_proj(0, ATTN_WIDTH, out_dtype=BF16, name="proj_q",
                epilogue=functools.partial(_qk_epilogue, scale=q_scale), extras=rope_specs(q_norm[0]))
    k = in_proj(c_k, KV_WIDTH, out_dtype=BF16, name="proj_k",
                epilogue=functools.partial(_qk_epilogue, scale=1.0), extras=rope_specs(k_norm[0]))
    v = in_proj(c_v, KV_WIDTH, out_dtype=BF16, name="proj_v")
    rkv = in_proj(c_rkv, 3 * RW, out_dtype=F32, name="proj_rkv")
    lora = in_proj(c_lora, LORA_PAD, out_dtype=F32, name="proj_lora")
    gates = in_proj(c_merge, 2 * D, out_dtype=BF16, name="proj_merge_gates", epilogue=_sigmoid_epilogue)

    attn_parts = []
    for (B, Np), off in zip(groups, offs):
        sl = slice(off, off + B * Np)
        attn_parts.append(_attention(q[sl], k[sl], v[sl].T, _key_bias(Np), B, Np))
    attn = jnp.concatenate(attn_parts + [jnp.zeros((tail, ATTN_WIDTH), BF16)], axis=0)

    cn = _pick(RW, (256, 128))
    vec = jnp.concatenate([shift_prev[0], shift_next[0], decay_base[0], iclr_base[0], k_k, k_a, r_k,
                           ln_x_w, ln_x_b, jnp.zeros((VEC_ROWS - 15, RW), F32)], axis=0).astype(F32)
    hid_idx = np.arange(cn) // RWKV_HEAD
    ones_bd = jnp.asarray((hid_idx[:, None] == hid_idx[None, :]).astype(np.float32)).astype(BF16)
    gup = jnp.pad(gate_up[0], ((0, GATE_PAD - GATE_LORA), (0, 0))).astype(BF16)
    r_s, v_s, kk, ew0, kd0, b0, ew1, kd1, b1, bonus, g_rwkv = _rwkv_prep(
        rkv, lora, decay_up[0], iclr_up[0], gup, vec, ones_bd, tmr, cn)
    first_fwd, first_rev = _seq_flags(groups + ([(1, tail)] if tail else []))
    t_idx = np.arange(CHUNK)
    tri_f = jnp.asarray((t_idx[None, :] <= t_idx[:, None]).astype(np.float32)).astype(BF16)
    tri_r = jnp.asarray((t_idx[None, :] >= t_idx[:, None]).astype(np.float32)).astype(BF16)
    npairs = _pick(RW // PAIR, (16, 8, 4, 2, 1))
    y_f = _wkv(first_fwd, r_s, v_s, kk, ew0, kd0, b0, tri_f, reverse=False, npairs=npairs)
    y_b = _wkv(first_rev, r_s, v_s, kk, ew1, kd1, b1, tri_r, reverse=True, npairs=npairs)
    rwkv = _rwkv_post(y_f, y_b, bonus, g_rwkv, vec, ones_bd, tmr, cn)

    tn_d = _pick(D, (512, 384, 256, 128))
    merged = _merge(attn, rwkv, w_branch_attn[0], w_branch_rwkv[0], gates, tm, tn_d)
    x1 = _matmul(merged, w_out[0], tm=tm, tn=tn_d, out_dtype=F32, name="out_proj",
                 epilogue=_residual_epilogue, extras=[(x, (tm, tn_d), lambda i, j: (i, j))])

    w_router_pad = jnp.pad(w_router[0], ((0, 0), (0, LANES - N_EXPERTS)))
    h2, aff = _router(x1, norm_ffn[0], w_router_pad, tme)
    aff = aff[:, :N_EXPERTS]
    idx_parts, gate_parts = [], []
    for (B, Np), off in zip(groups, offs):
        valid = jnp.asarray((np.arange(B * Np) % Np) >= FRONT)
        a_g = jnp.where(valid[:, None], aff[off:off + B * Np], -1.0)
        cap = CAPACITY_FACTOR * (B * (Np - FRONT)) // N_EXPERTS
        gate_g, idx_g = lax.top_k(a_g.T, cap)
        idx_g, gate_g = lax.sort((idx_g, gate_g), dimension=1, num_keys=1)
        idx_parts.append(idx_g + off)
        gate_parts.append(gate_g)
    idx = jnp.concatenate(idx_parts, axis=1)
    gate = jnp.concatenate(gate_parts, axis=1)
    c_tot = idx.shape[1]
    nt = -(-c_tot // 1040)
    tmx = -(-(-(-c_tot // nt)) // SLOT_ALIGN) * SLOT_ALIGN
    ct = nt * tmx
    tok = jnp.pad(idx, ((0, 0), (0, ct - c_tot)), constant_values=R)
    gate = jnp.pad(gate, ((0, 0), (0, ct - c_tot))).reshape(-1, 1)
    xs = jnp.take(h2, jnp.where(tok < R, tok, 0).reshape(-1), axis=0)
    tf = _pick(D, (256, 128))
    out = _expert_ffn(xs, gate, w_gate[0], w_up[0], w_down[0], tmx, tf)

    tmc = _pick(R, (256, LANES))
    edges = jnp.arange(R // tmc + 1, dtype=jnp.int32) * tmc
    pos = jnp.sum((tok[:, None, :] < edges[None, :, None]).astype(jnp.int32), axis=-1)
    pos = pos + (jnp.arange(N_EXPERTS, dtype=jnp.int32) * ct)[:, None]
    lo, hi = pos[:, :-1].T, pos[:, 1:].T
    npass = jnp.max(-(-(hi - lo) // COMBINE_SPAN), axis=1).astype(jnp.int32)
    y = _combine(out, tok.reshape(-1, 1).astype(jnp.int32), lo.reshape(-1), hi.reshape(-1), npass, R, tmc)

    return tuple(_final(x1, y, norm_final, B, Np, off) for (B, Np), off in zip(groups, offs))
```

```python
import functools
import math

import numpy as np
import jax
import jax.numpy as jnp
from jax import lax
from jax.experimental import pallas as pl
from jax.experimental.pallas import tpu as pltpu

F32 = jnp.float32
BF16 = jnp.bfloat16

N_META = 16
GRID_W = 64
HEAD_DIM = 128
N_Q_HEADS = 16
N_KV_HEADS = 4
Q_PER_KV = N_Q_HEADS // N_KV_HEADS
ATTN_WIDTH = N_Q_HEADS * HEAD_DIM
KV_WIDTH = N_KV_HEADS * HEAD_DIM
ROPE_THETA = 10000.0
RWKV_HEAD = 64
DECAY_LORA = 128
ICLR_LORA = 128
GATE_LORA = 480
N_EXPERTS = 16
CAPACITY_FACTOR = 2
NORM_EPS = 1e-6
GN_EPS = 64e-5

LANES = 128
FRONT = LANES - N_META
CHUNK = 64
PAIR = 2 * RWKV_HEAD
LORA_PAD = 768
GATE_PAD = LORA_PAD - DECAY_LORA - ICLR_LORA
VMEM_LIMIT = 56 * 1024 * 1024
NEG_BIG = -1e30
MATMUL_ROW_TILE = 768
EXPERT_ROW_TILE_MAX = 1040
ROW_TILES = (640, 512, 384, 256, 128)
KEY_TILES = (1024, 896, 768) + ROW_TILES
SINGLE_KEY_TILE_MAX = 2304


def _cparams(sem):
    return pltpu.CompilerParams(dimension_semantics=sem, vmem_limit_bytes=VMEM_LIMIT)


def _pick(n, cands):
    for c in cands:
        if n % c == 0:
            return c
    raise ValueError(f"no tile for {n} in {cands}")


def _split2(x):
    hi = x.astype(BF16)
    lo = (x - hi.astype(F32)).astype(BF16)
    return hi, lo


def _split3(x):
    hi = x.astype(BF16)
    r1 = x - hi.astype(F32)
    mid = r1.astype(BF16)
    lo = (r1 - mid.astype(F32)).astype(BF16)
    return hi, mid, lo


def _dot(a, b):
    return jnp.dot(a, b, preferred_element_type=F32)


def _dot_nt(a, b):
    return lax.dot_general(a, b, (((1,), (1,)), ((), ())), preferred_element_type=F32)


def _dot1(a, b):
    return _dot(a.astype(BF16), b.astype(BF16))


def _dot3(a, b):
    ah, al = _split2(a)
    bh, bl = _split2(b)
    return _dot(ah, bh) + (_dot(ah, bl) + _dot(al, bh))


def _dot_exact_lhs(a_bf16, b):
    hi, mid, lo = _split3(b)
    return _dot(a_bf16, hi) + (_dot(a_bf16, mid) + _dot(a_bf16, lo))


def _segsum(x, ones_bd):
    hi, lo = _split2(x)
    return _dot(hi, ones_bd) + _dot(lo, ones_bd)


def _sigmoid(x):
    return 1.0 / (1.0 + jnp.exp(-x))


BLK_TAIL, BLK_FRONT, BLK_PROMPT, BLK_SAMPLE = 0, 1, 2, 3


def _pack_norm_kernel(kind_ref, pb_ref, sb_ref, xp_ref, xs_ref, meta_ref, g_ref, x_o, h_o):
    kind = kind_ref[pl.program_id(0)]

    def emit(x):
        x_o[...] = x
        ms = jnp.mean(x * x, axis=-1, keepdims=True)
        h_o[...] = (x * lax.rsqrt(ms + NORM_EPS) * g_ref[...]).astype(h_o.dtype)

    @pl.when(kind == BLK_TAIL)
    def _():
        x_o[...] = jnp.zeros(x_o.shape, x_o.dtype)
        h_o[...] = jnp.zeros(h_o.shape, h_o.dtype)

    @pl.when(kind == BLK_FRONT)
    def _():
        emit(jnp.concatenate([jnp.zeros((FRONT, x_o.shape[1]), F32), meta_ref[...]], axis=0))

    pl.when(kind == BLK_PROMPT)(lambda: emit(xp_ref[...]))
    pl.when(kind == BLK_SAMPLE)(lambda: emit(xs_ref[...]))


def _pack_norm(x_prompt, x_sample, meta, g, R):
    D = x_prompt.shape[-1]
    kind, pb, sb = [], [], []
    for which, x in ((BLK_PROMPT, x_prompt), (BLK_SAMPLE, x_sample)):
        B, S = x.shape[0], x.shape[1]
        for b in range(B):
            for j in range(S // LANES + 1):
                kind.append(BLK_FRONT if j == 0 else which)
                src = b * (S // LANES) + max(j - 1, 0)
                pb.append(src if which == BLK_PROMPT else (pb[-1] if pb else 0))
                sb.append(src if which == BLK_SAMPLE else 0)
    n_blk = R // LANES
    kind += [BLK_TAIL] * (n_blk - len(kind))
    pb += [pb[-1]] * (n_blk - len(pb))
    sb += [sb[-1]] * (n_blk - len(sb))
    tables = [jnp.asarray(np.asarray(t, np.int32)) for t in (kind, pb, sb)]
    blk = pl.BlockSpec((LANES, D), lambda i, kind, pb, sb: (i, 0))
    grid_spec = pltpu.PrefetchScalarGridSpec(
        num_scalar_prefetch=3,
        grid=(n_blk,),
        in_specs=[pl.BlockSpec((LANES, D), lambda i, kind, pb, sb: (pb[i], 0)),
                  pl.BlockSpec((LANES, D), lambda i, kind, pb, sb: (sb[i], 0)),
                  pl.BlockSpec((N_META, D), lambda i, kind, pb, sb: (0, 0)),
                  pl.BlockSpec((1, D), lambda i, kind, pb, sb: (0, 0))],
        out_specs=[blk, blk],
    )
    return pl.pallas_call(
        _pack_norm_kernel,
        grid_spec=grid_spec,
        out_shape=[jax.ShapeDtypeStruct((R, D), F32), jax.ShapeDtypeStruct((R, D), BF16)],
        compiler_params=_cparams(("arbitrary",)),
        name="pack_rmsnorm",
    )(*tables, x_prompt.reshape(-1, D), x_sample.reshape(-1, D), meta.astype(F32), g.reshape(1, D))


def _mm_kernel(*refs, epilogue, n_extra):
    x_ref, w_ref = refs[0], refs[1]
    extras = refs[2:2 + n_extra]
    o_ref, w_sc = refs[2 + n_extra], refs[3 + n_extra]

    @pl.when(pl.program_id(1) == 0)
    def _():
        w_sc[...] = w_ref[...].astype(w_sc.dtype)

    acc = _dot(x_ref[...], w_sc[...])
    if epilogue is not None:
        acc = epilogue(acc, *extras)
    o_ref[...] = acc.astype(o_ref.dtype)


def _matmul(x, w, *, tm, out_dtype, name, col0=0, n_out=None, tn=None, epilogue=None, extras=()):
    M, K = x.shape
    N = n_out or w.shape[1]
    tn = tn or _pick(N, (512, 384, 256, 128))
    assert col0 % tn == 0 and N % tn == 0
    c0 = col0 // tn
    in_specs = [pl.BlockSpec((tm, K), lambda j, i: (i, 0)), pl.BlockSpec((K, tn), lambda j, i: (0, c0 + j))]
    in_specs += [pl.BlockSpec(shape, functools.partial(lambda j, i, fn: fn(i, j), fn=fn)) for _, shape, fn in extras]
    return pl.pallas_call(
        functools.partial(_mm_kernel, epilogue=epilogue, n_extra=len(extras)),
        grid=(N // tn, M // tm),
        in_specs=in_specs,
        out_specs=pl.BlockSpec((tm, tn), lambda j, i: (i, j)),
        out_shape=jax.ShapeDtypeStruct((M, N), out_dtype),
        scratch_shapes=[pltpu.VMEM((K, tn), BF16)],
        compiler_params=_cparams(("parallel", "arbitrary")),
        name=name,
    )(x, w, *[a for a, _, _ in extras])


def _qk_epilogue(acc, g_ref, c_ref, s_ref, *, scale):
    g = g_ref[...]
    c = c_ref[...]
    s = s_ref[...]
    outs = []
    for h in range(acc.shape[1] // HEAD_DIM):
        y = acc[:, h * HEAD_DIM:(h + 1) * HEAD_DIM]
        y = y * lax.rsqrt(jnp.mean(y * y, axis=-1, keepdims=True) + NORM_EPS) * g
        y = y * c + pltpu.roll(y, HEAD_DIM // 2, 1) * s
        outs.append(y * scale if scale != 1.0 else y)
    return jnp.concatenate(outs, axis=1) if len(outs) > 1 else outs[0]


def _sigmoid_epilogue(acc):
    return _sigmoid(acc)


def _residual_epilogue(acc, x_ref):
    return x_ref[...] + acc


def _attn_kernel(q_ref, k_ref, vt_ref, bias_ref, _prev_ref, o_ref, m_sc, l_sc, acc_sc, *, tq, nk):
    j = pl.program_id(3)
    M = Q_PER_KV * tq

    @pl.when(j == 0)
    def _():
        m_sc[...] = jnp.full(m_sc.shape, NEG_BIG, F32)
        l_sc[...] = jnp.zeros(l_sc.shape, F32)
        acc_sc[...] = jnp.zeros(acc_sc.shape, F32)

    q = q_ref[...]
    q4 = jnp.concatenate([q[:, h * HEAD_DIM:(h + 1) * HEAD_DIM] for h in range(Q_PER_KV)], axis=0)

    def step(use_bias):
        s = _dot_nt(k_ref[...], q4)
        if use_bias:
            s = s + jnp.tile(bias_ref[...], (1, M // LANES))
        m_prev = m_sc[...]
        m_new = jnp.maximum(m_prev, jnp.max(s, axis=0, keepdims=True))
        alpha = jnp.exp2(m_prev - m_new)
        p = jnp.exp2(s - m_new)
        l_sc[...] = alpha * l_sc[...] + jnp.sum(p, axis=0, keepdims=True)
        acc_sc[...] = alpha * acc_sc[...] + _dot(vt_ref[...], p.astype(BF16))
        m_sc[...] = m_new

    if nk == 1:
        step(True)
    else:
        pl.when(j == 0)(functools.partial(step, True))
        pl.when(j > 0)(functools.partial(step, False))

    @pl.when(j == nk - 1)
    def _():
        o = (acc_sc[...] / l_sc[...]).T
        o_ref[...] = jnp.concatenate([o[h * tq:(h + 1) * tq] for h in range(Q_PER_KV)], axis=1).astype(o_ref.dtype)


def _attention(q, k, vt, bias, B, Np, row0, out_prev):
    tq = _pick(math.gcd(Np, row0), ROW_TILES)
    tk = Np if Np <= SINGLE_KEY_TILE_MAX else _pick(Np, KEY_TILES)
    nq, nk = Np // tq, Np // tk
    q0 = row0 // tq
    M = Q_PER_KV * tq
    return pl.pallas_call(
        functools.partial(_attn_kernel, tq=tq, nk=nk),
        grid=(B, N_KV_HEADS, nq, nk),
        in_specs=[
            pl.BlockSpec((tq, Q_PER_KV * HEAD_DIM), lambda b, g, i, j: (q0 + b * nq + i, g)),
            pl.BlockSpec((tk, HEAD_DIM), lambda b, g, i, j: (b * nk + j, g)),
            pl.BlockSpec((HEAD_DIM, tk), lambda b, g, i, j: (g, b * nk + j)),
            pl.BlockSpec((tk, LANES), lambda b, g, i, j: (j, 0)),
            pl.BlockSpec(memory_space=pl.ANY),
        ],
        out_specs=pl.BlockSpec((tq, Q_PER_KV * HEAD_DIM), lambda b, g, i, j: (q0 + b * nq + i, g)),
        out_shape=jax.ShapeDtypeStruct(out_prev.shape, out_prev.dtype),
        input_output_aliases={4: 0},
        scratch_shapes=[
            pltpu.VMEM((1, M), F32),
            pltpu.VMEM((1, M), F32),
            pltpu.VMEM((HEAD_DIM, M), F32),
        ],
        compiler_params=_cparams(("parallel", "parallel", "parallel", "arbitrary")),
        name="attention",
    )(q, k, vt, bias, out_prev)


V_SP, V_SN, V_DBASE, V_IBASE, V_KK, V_KA, V_RK, V_LNW, V_LNB = 0, 3, 6, 8, 10, 11, 12, 13, 14
VEC_ROWS = 16
HALO = 8


def _prep_kernel(r_ref, k_ref, v_ref, rp_ref, rn_ref, kp_ref, kn_ref, vp_ref, vn_ref, lora_ref,
                 dup0_ref, dup1_ref, iup0_ref, iup1_ref, gup_ref, vec_ref, ones_ref,
                 r_o, v_o, kk_o, ew0_o, kd0_o, b0_o, ew1_o, kd1_o, b1_o, bonus_o, g_o, *, tm, n_row_tiles):
    i = pl.program_id(0)
    has_prev = (i > 0).astype(F32)
    has_next = (i < n_row_tiles - 1).astype(F32)
    vec = vec_ref[...]
    row = lax.broadcasted_iota(jnp.int32, (tm, 1), 0)

    def shifted(x_ref, p_ref, n_ref, idx):
        x = x_ref[...]
        prev = jnp.where(row == 0, p_ref[HALO - 1:HALO, :] * has_prev, pltpu.roll(x, 1, 0))
        nxt = jnp.where(row == tm - 1, n_ref[0:1, :] * has_next, pltpu.roll(x, tm - 1, 0))
        return x + vec[V_SP + idx:V_SP + idx + 1] * (prev - x) + vec[V_SN + idx:V_SN + idx + 1] * (nxt - x)

    r = shifted(r_ref, rp_ref, rn_ref, 0)
    k = shifted(k_ref, kp_ref, kn_ref, 1)
    v = shifted(v_ref, vp_ref, vn_ref, 2)
    ones = ones_ref[...]

    lora = lora_ref[...]
    decay_h = jnp.tanh(lora[:, 0:DECAY_LORA])
    iclr_h = lora[:, DECAY_LORA:DECAY_LORA + ICLR_LORA]
    gate_h = _sigmoid(lora[:, DECAY_LORA + ICLR_LORA:LORA_PAD])

    kkr = k * vec[V_KK:V_KK + 1]
    kk = kkr / jnp.maximum(jnp.sqrt(_segsum(kkr * kkr, ones)), 1e-12)
    r_o[...] = r.astype(r_o.dtype)
    v_o[...] = v.astype(v_o.dtype)
    kk_o[...] = kk.astype(kk_o.dtype)
    bonus_o[...] = (_segsum(r * k * vec[V_RK:V_RK + 1], ones) * v).astype(bonus_o.dtype)
    g_o[...] = _dot(gate_h.astype(BF16), gup_ref[...]).astype(g_o.dtype)

    for d, (dup_ref, iup_ref, ew_o, kd_o, b_o) in enumerate(
            ((dup0_ref, iup0_ref, ew0_o, kd0_o, b0_o), (dup1_ref, iup1_ref, ew1_o, kd1_o, b1_o))):
        u = -(vec[V_DBASE + d:V_DBASE + d + 1] + _dot1(decay_h, dup_ref[...]))
        softplus = jnp.maximum(u, 0.0) + jnp.log(1.0 + jnp.exp(-jnp.abs(u)))
        ew_o[...] = jnp.exp(-softplus - 0.5)
        a = _sigmoid(vec[V_IBASE + d:V_IBASE + d + 1] + _dot1(iclr_h, iup_ref[...]))
        kd_o[...] = (k * (1.0 + (a - 1.0) * vec[V_KA:V_KA + 1])).astype(kd_o.dtype)
        b_o[...] = (kk * a).astype(b_o.dtype)


def _rwkv_prep(rkv, lora, dup, iup, gup, vec, ones_bd, tm, cn):
    R = rkv.shape[0]
    RW = rkv.shape[1] // 3
    J = RW // cn
    n_row_tiles = R // tm
    hb = tm // HALO
    last_hb = R // HALO - 1

    def main(c):
        return pl.BlockSpec((tm, cn), lambda i, j: (i, c * J + j))

    def prev(c):
        return pl.BlockSpec((HALO, cn), lambda i, j: (jnp.maximum(i * hb - 1, 0), c * J + j))

    def nxt(c):
        return pl.BlockSpec((HALO, cn), lambda i, j: (jnp.minimum((i + 1) * hb, last_hb), c * J + j))

    up = pl.BlockSpec((DECAY_LORA, cn), lambda i, j: (0, j))
    in_specs = [main(0), main(1), main(2), prev(0), nxt(0), prev(1), nxt(1), prev(2), nxt(2),
                pl.BlockSpec((tm, LORA_PAD), lambda i, j: (i, 0)),
                up, up, up, up,
                pl.BlockSpec((GATE_PAD, cn), lambda i, j: (0, j)),
                pl.BlockSpec((VEC_ROWS, cn), lambda i, j: (0, j)),
                pl.BlockSpec((cn, cn), lambda i, j: (0, 0))]
    out_spec = pl.BlockSpec((tm, cn), lambda i, j: (i, j))
    out_dtypes = [BF16, BF16, BF16, F32, BF16, BF16, F32, BF16, BF16, BF16, BF16]
    return pl.pallas_call(
        functools.partial(_prep_kernel, tm=tm, n_row_tiles=n_row_tiles),
        grid=(n_row_tiles, J),
        in_specs=in_specs,
        out_specs=[out_spec] * len(out_dtypes),
        out_shape=[jax.ShapeDtypeStruct((R, RW), dt) for dt in out_dtypes],
        compiler_params=_cparams(("parallel", "parallel")),
        name="rwkv_prep",
    )(rkv, rkv, rkv, rkv, rkv, rkv, rkv, rkv, rkv, lora, dup[0], dup[1], iup[0], iup[1], gup, vec, ones_bd)


def _wkv_kernel(first_ref, r_ref, v_ref, kk_ref, ew_ref, kd_ref, b_ref, tri_ref, y_ref, h_sc, *, reverse, npairs):
    c = pl.program_id(1)

    @pl.when(first_ref[c] == 1)
    def _():
        h_sc[...] = jnp.zeros(h_sc.shape, F32)

    L = CHUNK
    ew = ew_ref[...]
    cs = _dot_exact_lhs(tri_ref[...], ew)
    tot = cs[0:1] if reverse else cs[L - 1:L]
    e_neg = jnp.exp(-cs)
    e_prev = jnp.exp(ew - cs)
    e_pos = jnp.exp(cs)
    e_fin = jnp.exp(cs - tot)
    w_tot = jnp.exp(-tot)

    ri = lax.broadcasted_iota(jnp.int32, (PAIR, PAIR), 0)
    ci = lax.broadcasted_iota(jnp.int32, (PAIR, PAIR), 1)
    same = (ri // L) == (ci // L)
    t_i = ri % L
    s_i = ci % L
    if reverse:
        strict = same & (s_i > t_i)
        incl = same & (s_i >= t_i)
    else:
        strict = same & (s_i < t_i)
        incl = same & (s_i <= t_i)
    eye = ri == ci
    head0 = lax.broadcasted_iota(jnp.int32, (L, PAIR), 1) < RWKV_HEAD

    def stack(x):
        return jnp.concatenate([jnp.where(head0, x, 0.0), jnp.where(head0, 0.0, x)], axis=0)

    P2 = 2 * PAIR
    pairs = range(npairs)
    sls = [slice(p * PAIR, (p + 1) * PAIR) for p in pairs]

    r_st, a_st, v_st, bk2, kf_t, bf_t = [], [], [], [], [], []
    for sl in sls:
        kd = kd_ref[:, sl].astype(F32)
        b = b_ref[:, sl].astype(F32)
        bt = (b * e_pos[:, sl]).astype(BF16)
        kt = (kd * e_pos[:, sl]).astype(BF16)
        r_st.append(stack(r_ref[:, sl].astype(F32) * e_neg[:, sl]))
        a_st.append(stack(-kk_ref[:, sl].astype(F32) * e_prev[:, sl]))
        v_st.append(stack(v_ref[:, sl].astype(F32)).astype(BF16))
        bk2.append(jnp.concatenate([bt, bt, kt, kt], axis=0))
        kf_t.append(stack(kd * e_fin[:, sl]).T)
        bf_t.append(stack(b * e_fin[:, sl]).T)

    a_ab, a_ak, wlhs = [], [], []
    for p in pairs:
        sc = _dot_nt(jnp.concatenate([a_st[p], r_st[p]], axis=0).astype(BF16), bk2[p])
        a_ab.append(jnp.where(strict, sc[0:PAIR, 0:PAIR], 0.0))
        a_ak.append(jnp.where(strict, sc[0:PAIR, PAIR:P2], 0.0).astype(BF16))
        m_rb = jnp.where(incl, sc[PAIR:P2, 0:PAIR], 0.0)
        m_rk = jnp.where(incl, sc[PAIR:P2, PAIR:P2], 0.0)
        wlhs.append(jnp.concatenate([jnp.concatenate([m_rk, m_rb], axis=1),
                                     jnp.concatenate([kf_t[p], bf_t[p]], axis=1)], axis=0).astype(BF16))

    x = [jnp.concatenate([a_st[p], _dot(a_ak[p], v_st[p])], axis=1) for p in pairs]
    npow = a_ab
    n_sq = int(math.log2(L)) - 1
    for it in range(n_sq + 1):
        lhs = [npow[p].astype(BF16) for p in pairs]
        if it < n_sq:
            prod = [_dot(lhs[p], jnp.concatenate([lhs[p], x[p].astype(BF16)], axis=1)) for p in pairs]
            npow = [prod[p][:, 0:PAIR] for p in pairs]
            x = [x[p] + prod[p][:, PAIR:] for p in pairs]
        else:
            x = [x[p] + _dot(lhs[p], x[p].astype(BF16)) for p in pairs]

    zero = jnp.zeros((PAIR, PAIR), BF16)
    for p in pairs:
        sl = sls[p]
        rhs = jnp.concatenate([jnp.concatenate([zero, v_st[p]], axis=1), x[p].astype(BF16)], axis=0)
        w = _dot(wlhs[p], rhs)
        g_st = r_st[p] + w[0:PAIR, 0:PAIR]
        y0_st = w[0:PAIR, PAIR:P2]
        phi = jnp.where(eye, w_tot[:, sl], 0.0) + w[PAIR:P2, 0:PAIR]
        psi = w[PAIR:P2, PAIR:P2]
        out = _dot1(jnp.concatenate([g_st, phi], axis=0), h_sc[p])
        y_st = out[0:PAIR] + y0_st
        h_sc[p] = out[PAIR:P2] + psi
        y_ref[:, sl] = y_st[0:L] + y_st[L:2 * L]


def _wkv(first, r, v, kk, ew, kd, b, tri, *, reverse, npairs):
    R, RW = r.shape
    NC = R // CHUNK
    width = npairs * PAIR
    if reverse:
        blk = pl.BlockSpec((CHUNK, width), lambda g, c, f: (NC - 1 - c, g))
    else:
        blk = pl.BlockSpec((CHUNK, width), lambda g, c, f: (c, g))
    grid_spec = pltpu.PrefetchScalarGridSpec(
        num_scalar_prefetch=1,
        grid=(RW // width, NC),
        in_specs=[blk] * 6 + [pl.BlockSpec((CHUNK, CHUNK), lambda g, c, f: (0, 0))],
        out_specs=blk,
        scratch_shapes=[pltpu.VMEM((npairs, PAIR, PAIR), F32)],
    )
    return pl.pallas_call(
        functools.partial(_wkv_kernel, reverse=reverse, npairs=npairs),
        grid_spec=grid_spec,
        out_shape=jax.ShapeDtypeStruct((R, RW), F32),
        compiler_params=_cparams(("parallel", "arbitrary")),
        name="wkv_rev" if reverse else "wkv_fwd",
    )(first, r, v, kk, ew, kd, b, tri)


def _post_kernel(yf_ref, yb_ref, bonus_ref, g_ref, vec_ref, ones_ref, o_ref):
    ones = ones_ref[...]
    vec = vec_ref[...]
    y = yf_ref[...] + yb_ref[...]
    inv_n = 1.0 / RWKV_HEAD
    mu = _segsum(y, ones) * inv_n
    d = y - mu
    var = _segsum(d * d, ones) * inv_n
    yn = d * lax.rsqrt(var + GN_EPS) * vec[V_LNW:V_LNW + 1] + vec[V_LNB:V_LNB + 1]
    o_ref[...] = ((yn + bonus_ref[...]) * g_ref[...]).astype(o_ref.dtype)


def _rwkv_post(yf, yb, bonus, g, vec, ones_bd, tm, cn):
    R, RW = yf.shape
    blk = pl.BlockSpec((tm, cn), lambda i, j: (i, j))
    return pl.pallas_call(
        _post_kernel,
        grid=(R // tm, RW // cn),
        in_specs=[blk, blk, blk, blk, pl.BlockSpec((VEC_ROWS, cn), lambda i, j: (0, j)),
                  pl.BlockSpec((cn, cn), lambda i, j: (0, 0))],
        out_specs=blk,
        out_shape=jax.ShapeDtypeStruct((R, RW), BF16),
        compiler_params=_cparams(("parallel", "parallel")),
        name="rwkv_post",
    )(yf, yb, bonus, g, vec, ones_bd)


def _merge_kernel(a_ref, r_ref, wa_ref, wr_ref, ga_ref, gr_ref, o_ref, wa_sc, wr_sc):
    @pl.when(pl.program_id(1) == 0)
    def _():
        wa_sc[...] = wa_ref[...].astype(wa_sc.dtype)
        wr_sc[...] = wr_ref[...].astype(wr_sc.dtype)

    ya = _dot(a_ref[...], wa_sc[...])
    yr = _dot(r_ref[...], wr_sc[...])
    o_ref[...] = (ga_ref[...].astype(F32) * ya + gr_ref[...].astype(F32) * yr).astype(o_ref.dtype)


def _merge(attn, rwkv, wa, wr, gates, tm, tn):
    R = attn.shape[0]
    D = wa.shape[1]
    J = D // tn
    return pl.pallas_call(
        _merge_kernel,
        grid=(J, R // tm),
        in_specs=[
            pl.BlockSpec((tm, attn.shape[1]), lambda j, i: (i, 0)),
            pl.BlockSpec((tm, rwkv.shape[1]), lambda j, i: (i, 0)),
            pl.BlockSpec((wa.shape[0], tn), lambda j, i: (0, j)),
            pl.BlockSpec((wr.shape[0], tn), lambda j, i: (0, j)),
            pl.BlockSpec((tm, tn), lambda j, i: (i, j)),
            pl.BlockSpec((tm, tn), lambda j, i: (i, J + j)),
        ],
        out_specs=pl.BlockSpec((tm, tn), lambda j, i: (i, j)),
        out_shape=jax.ShapeDtypeStruct((R, D), BF16),
        scratch_shapes=[pltpu.VMEM((wa.shape[0], tn), BF16), pltpu.VMEM((wr.shape[0], tn), BF16)],
        compiler_params=_cparams(("parallel", "arbitrary")),
        name="merge",
    )(attn, rwkv, wa, wr, gates, gates)


def _router_kernel(x_ref, g_ref, w_ref, h_ref, aff_ref):
    x = x_ref[...]
    ms = jnp.mean(x * x, axis=-1, keepdims=True)
    h = x * lax.rsqrt(ms + NORM_EPS) * g_ref[...]
    h_ref[...] = h.astype(h_ref.dtype)
    logits = _dot3(h, w_ref[...])
    lane = lax.broadcasted_iota(jnp.int32, logits.shape, 1)
    logits = jnp.where(lane < N_EXPERTS, logits, NEG_BIG)
    e = jnp.exp(logits - jnp.max(logits, axis=-1, keepdims=True))
    aff_ref[...] = e / jnp.sum(e, axis=-1, keepdims=True)


def _router(x1, g, w_router_pad, tm):
    R, D = x1.shape
    return pl.pallas_call(
        _router_kernel,
        grid=(R // tm,),
        in_specs=[pl.BlockSpec((tm, D), lambda i: (i, 0)), pl.BlockSpec((1, D), lambda i: (0, 0)),
                  pl.BlockSpec((D, LANES), lambda i: (0, 0))],
        out_specs=[pl.BlockSpec((tm, D), lambda i: (i, 0)), pl.BlockSpec((tm, LANES), lambda i: (i, 0))],
        out_shape=[jax.ShapeDtypeStruct((R, D), BF16), jax.ShapeDtypeStruct((R, LANES), F32)],
        compiler_params=_cparams(("parallel",)),
        name="ffn_norm_router",
    )(x1, g.reshape(1, D), w_router_pad)


def _ffn_up_kernel(x_ref, wg_ref, wu_ref, o_ref):
    x = x_ref[...]
    hg = _dot(x, wg_ref[...].astype(BF16))
    hu = _dot(x, wu_ref[...].astype(BF16))
    o_ref[...] = (hg * _sigmoid(hg) * hu).astype(o_ref.dtype)


def _ffn_down_kernel(h_ref, wd_ref, gate_ref, o_ref):
    o_ref[...] = (_dot(h_ref[...], wd_ref[...].astype(BF16)) * gate_ref[...]).astype(o_ref.dtype)


def _expert_ffn(xs, gate, w_gate, w_up, w_down, tm, tf):
    E, D, F = w_gate.shape
    nt = xs.shape[0] // (E * tm)
    wspec = pl.BlockSpec((None, D, tf), lambda e, i, f: (e, 0, f))
    hid = pl.pallas_call(
        _ffn_up_kernel,
        grid=(E, nt, F // tf),
        in_specs=[pl.BlockSpec((tm, D), lambda e, i, f: (e * nt + i, 0)), wspec, wspec],
        out_specs=pl.BlockSpec((tm, tf), lambda e, i, f: (e * nt + i, f)),
        out_shape=jax.ShapeDtypeStruct((xs.shape[0], F), BF16),
        compiler_params=_cparams(("parallel", "parallel", "parallel")),
        name="ffn_up",
    )(xs, w_gate, w_up)
    td = _pick(D, (2 * tf, tf))
    return pl.pallas_call(
        _ffn_down_kernel,
        grid=(E, nt, D // td),
        in_specs=[pl.BlockSpec((tm, F), lambda e, i, f: (e * nt + i, 0)),
                  pl.BlockSpec((None, F, td), lambda e, i, f: (e, 0, f)),
                  pl.BlockSpec((tm, 1), lambda e, i, f: (e * nt + i, 0))],
        out_specs=pl.BlockSpec((tm, td), lambda e, i, f: (e * nt + i, f)),
        out_shape=jax.ShapeDtypeStruct((xs.shape[0], D), BF16),
        compiler_params=_cparams(("parallel", "parallel", "parallel")),
        name="ffn_down",
    )(hid, w_down, gate)


COMBINE_CHUNK = 64
SLOT_ALIGN = 16
COMBINE_SPAN = COMBINE_CHUNK - SLOT_ALIGN


def _combine_kernel(lo_ref, hi_ref, npass_ref, out_hbm, tok_hbm, y_ref, buf, tbuf, sem, *, ns, tmc):
    i = pl.program_id(0)
    E, CH = N_EXPERTS, COMBINE_CHUNK
    base = i * tmc
    lane = lax.broadcasted_iota(jnp.int32, (CH, tmc), 1)
    srow = lax.broadcasted_iota(jnp.int32, (CH, 1), 0)
    y_ref[...] = jnp.zeros(y_ref.shape, F32)

    def one_pass(p, carry):
        copies, bounds = [], []
        for e in range(E):
            a = lo_ref[i * E + e] + p * COMBINE_SPAN
            b = jnp.minimum(a + COMBINE_SPAN, hi_ref[i * E + e])
            start = pl.multiple_of(jnp.clip((a // SLOT_ALIGN) * SLOT_ALIGN, 0, ns - CH), SLOT_ALIGN)
            rows = pl.ds(e * CH, CH)
            pair = (pltpu.make_async_copy(out_hbm.at[pl.ds(start, CH), :], buf.at[rows, :], sem.at[0, e]),
                    pltpu.make_async_copy(tok_hbm.at[pl.ds(start, CH), :], tbuf.at[rows, :], sem.at[1, e]))
            pair[0].start()
            pair[1].start()
            copies.append(pair)
            bounds.append((start, a, b))
        blocks = []
        for e in range(E):
            copies[e][0].wait()
            copies[e][1].wait()
            start, a, b = bounds[e]
            slot = start + srow
            hit = (tbuf[pl.ds(e * CH, CH), :] - base == lane) & (slot >= a) & (slot < b)
            blocks.append(jnp.where(hit, 1.0, 0.0))
        onehot = jnp.concatenate(blocks, axis=0).T.astype(BF16)
        y_ref[...] += _dot(onehot, buf[...])
        return carry

    lax.fori_loop(0, npass_ref[i], one_pass, 0)


def _combine(out, tok, lo, hi, npass, R, tmc):
    ns, D = out.shape
    grid_spec = pltpu.PrefetchScalarGridSpec(
        num_scalar_prefetch=3,
        grid=(R // tmc,),
        in_specs=[pl.BlockSpec(memory_space=pl.ANY), pl.BlockSpec(memory_space=pl.ANY)],
        out_specs=pl.BlockSpec((tmc, D), lambda i, lo, hi, n: (i, 0)),
        scratch_shapes=[pltpu.VMEM((N_EXPERTS * COMBINE_CHUNK, D), BF16),
                        pltpu.VMEM((N_EXPERTS * COMBINE_CHUNK, 1), jnp.int32),
                        pltpu.SemaphoreType.DMA((2, N_EXPERTS))],
    )
    return pl.pallas_call(
        functools.partial(_combine_kernel, ns=ns, tmc=tmc),
        grid_spec=grid_spec,
        out_shape=jax.ShapeDtypeStruct((R, D), F32),
        compiler_params=_cparams(("arbitrary",)),
        name="expert_combine",
    )(lo, hi, npass, out, tok)


def _final_kernel(x_ref, y_ref, g_ref, o_ref):
    x = x_ref[...] + y_ref[...]
    ms = jnp.mean(x * x, axis=-1, keepdims=True)
    o_ref[...] = x * lax.rsqrt(ms + NORM_EPS) * g_ref[...]


def _final(x1, y, g, B, Np, off):
    D = x1.shape[1]
    nb = Np // LANES
    base = off // LANES
    blk = pl.BlockSpec((LANES, D), lambda b, j: (base + b * nb + 1 + j, 0))
    return pl.pallas_call(
        _final_kernel,
        grid=(B, nb - 1),
        in_specs=[blk, blk, pl.BlockSpec((1, D), lambda b, j: (0, 0))],
        out_specs=pl.BlockSpec((None, LANES, D), lambda b, j: (b, j, 0)),
        out_shape=jax.ShapeDtypeStruct((B, Np - LANES, D), F32),
        compiler_params=_cparams(("parallel", "parallel")),
        name="final_norm",
    )(x1, y, g.reshape(1, D))


def _rope_tables(S, B):
    rows = S // GRID_W
    row_ids = jnp.repeat(jnp.arange(rows, dtype=F32), GRID_W)
    col_ids = jnp.tile(jnp.arange(GRID_W, dtype=F32), rows)
    half = HEAD_DIM // 2
    inv_freq = 1.0 / (ROPE_THETA ** (jnp.arange(0, half, 2, dtype=F32) / half))
    ang = jnp.concatenate([row_ids[:, None] * inv_freq, col_ids[:, None] * inv_freq], axis=-1)
    ang = jnp.concatenate([jnp.zeros((LANES, half), F32), ang], axis=0)
    c, s = jnp.cos(ang), jnp.sin(ang)
    return (jnp.tile(jnp.concatenate([c, c], axis=-1), (B, 1)),
            jnp.tile(jnp.concatenate([-s, s], axis=-1), (B, 1)))


def _key_bias(Np):
    col = np.where(np.arange(Np) < FRONT, NEG_BIG, 0.0).astype(np.float32)
    return jnp.asarray(np.repeat(col[:, None], LANES, axis=1))


def _seq_flags(groups):
    fwd = []
    for B, Np in groups:
        nc = Np // CHUNK
        for _ in range(B):
            fwd += [1] + [0] * (nc - 1)
    fwd = np.asarray(fwd, np.int32)
    last = np.roll(fwd, -1)
    return jnp.asarray(fwd), jnp.asarray(last[::-1].copy())


def kernel(x_prompt, x_sample, meta_tokens, norm_mix, w_in, q_norm, k_norm, shift_prev, shift_next, decay_up, decay_base, iclr_up, iclr_base, gate_up, k_k, k_a, r_k, ln_x_w, ln_x_b, w_branch_attn, w_branch_rwkv, w_out, norm_ffn, w_router, w_gate, w_up, w_down, norm_final):
    assert norm_mix.shape[0] == 1, "one layer"
    D = x_prompt.shape[-1]
    RW = D // 2
    groups = [(x.shape[0], x.shape[1] + LANES) for x in (x_prompt, x_sample)]
    seqs = [x.shape[1] for x in (x_prompt, x_sample)]
    assert all(s % LANES == 0 for s in seqs)
    offs = [0, groups[0][0] * groups[0][1]]
    r_used = offs[1] + groups[1][0] * groups[1][1]
    tm = MATMUL_ROW_TILE if r_used >= 8 * MATMUL_ROW_TILE else LANES
    tme = tm // 3 if tm % 3 == 0 else tm
    tmr = tm // 2 if tm % 256 == 0 else tm
    R = -(-r_used // tm) * tm
    tail = R - r_used

    x, h = _pack_norm(x_prompt, x_sample, meta_tokens, norm_mix[0], R)
    tabs = [_rope_tables(S, B) for S, (B, _) in zip(seqs, groups)]
    cos_t = jnp.concatenate([t[0] for t in tabs] + [jnp.zeros((tail, HEAD_DIM), F32)], axis=0)
    sin_t = jnp.concatenate([t[1] for t in tabs] + [jnp.zeros((tail, HEAD_DIM), F32)], axis=0)

    w_in0 = w_in[0]
    n_in = w_in0.shape[1]

    def in_proj(col0, n_out, **kw):
        tn = _pick(n_out, (512, 384, 256, 128))
        if col0 % tn == 0 and col0 + n_out <= n_in:
            return _matmul(h, w_in0, col0=col0, n_out=n_out, tn=tn, tm=tm, **kw)
        w_cols = w_in0[:, col0:min(col0 + n_out, n_in)]
        w_cols = jnp.pad(w_cols, ((0, 0), (0, n_out - w_cols.shape[1])))
        return _matmul(h, w_cols, tn=tn, tm=tm, **kw)

    rope_specs = lambda g: [(g.reshape(1, HEAD_DIM), (1, HEAD_DIM), lambda i, j: (0, 0)),
                            (cos_t, (tm, HEAD_DIM), lambda i, j: (i, 0)),
                            (sin_t, (tm, HEAD_DIM), lambda i, j: (i, 0))]
    q_scale = HEAD_DIM ** -0.5 * math.log2(math.e)
    c_k = ATTN_WIDTH
    c_v = c_k + KV_WIDTH
    c_rkv = c_v + KV_WIDTH
    c_lora = c_rkv + 3 * RW
    c_merge = c_lora + DECAY_LORA + ICLR_LORA + GATE_LORA
    q = in_proj(0, ATTN_WIDTH, out_dtype=BF16, name="proj_q",
                epilogue=functools.partial(_qk_epilogue, scale=q_scale), extras=rope_specs(q_norm[0]))
    k = in_proj(c_k, KV_WIDTH, out_dtype=BF16, name="proj_k",
                epilogue=functools.partial(_qk_epilogue, scale=1.0), extras=rope_specs(k_norm[0]))
    v = in_proj(c_v, KV_WIDTH, out_dtype=BF16, name="proj_v")
    rkv = in_proj(c_rkv, 3 * RW, out_dtype=F32, name="proj_rkv")
    lora = in_proj(c_lora, LORA_PAD, out_dtype=F32, name="proj_lora")
    gates = in_proj(c_merge, 2 * D, out_dtype=BF16, name="proj_merge_gates", epilogue=_sigmoid_epilogue)

    attn = jnp.zeros((R, ATTN_WIDTH), BF16)
    for (B, Np), off in zip(groups, offs):
        sl = slice(off, off + B * Np)
        attn = _attention(q, k[sl], v[sl].T, _key_bias(Np), B, Np, off, attn)

    cn = _pick(RW, (256, 128))
    vec = jnp.concatenate([shift_prev[0], shift_next[0], decay_base[0], iclr_base[0], k_k, k_a, r_k,
                           ln_x_w, ln_x_b, jnp.zeros((VEC_ROWS - 15, RW), F32)], axis=0).astype(F32)
    hid_idx = np.arange(cn) // RWKV_HEAD
    ones_bd = jnp.asarray((hid_idx[:, None] == hid_idx[None, :]).astype(np.float32)).astype(BF16)
    gup = jnp.pad(gate_up[0], ((0, GATE_PAD - GATE_LORA), (0, 0))).astype(BF16)
    r_s, v_s, kk, ew0, kd0, b0, ew1, kd1, b1, bonus, g_rwkv = _rwkv_prep(
        rkv, lora, decay_up[0], iclr_up[0], gup, vec, ones_bd, tmr, cn)
    first_fwd, first_rev = _seq_flags(groups + ([(1, tail)] if tail else []))
    t_idx = np.arange(CHUNK)
    tri_f = jnp.asarray((t_idx[None, :] <= t_idx[:, None]).astype(np.float32)).astype(BF16)
    tri_r = jnp.asarray((t_idx[None, :] >= t_idx[:, None]).astype(np.float32)).astype(BF16)
    npairs = _pick(RW // PAIR, (16, 8, 4, 2, 1))
    y_f = _wkv(first_fwd, r_s, v_s, kk, ew0, kd0, b0, tri_f, reverse=False, npairs=npairs)
    y_b = _wkv(first_rev, r_s, v_s, kk, ew1, kd1, b1, tri_r, reverse=True, npairs=npairs)
    rwkv = _rwkv_post(y_f, y_b, bonus, g_rwkv, vec, ones_bd, tmr, cn)

    tn_d = _pick(D, (512, 384, 256, 128))
    merged = _merge(attn, rwkv, w_branch_attn[0], w_branch_rwkv[0], gates, tm, tn_d)
    x1 = _matmul(merged, w_out[0], tm=tm, tn=tn_d, out_dtype=F32, name="out_proj",
                 epilogue=_residual_epilogue, extras=[(x, (tm, tn_d), lambda i, j: (i, j))])

    w_router_pad = jnp.pad(w_router[0], ((0, 0), (0, LANES - N_EXPERTS)))
    h2, aff = _router(x1, norm_ffn[0], w_router_pad, tme)
    aff = aff[:, :N_EXPERTS]
    idx_parts, gate_parts = [], []
    for (B, Np), off in zip(groups, offs):
        valid = jnp.asarray((np.arange(B * Np) % Np) >= FRONT)
        a_g = jnp.where(valid[:, None], aff[off:off + B * Np], -1.0)
        cap = CAPACITY_FACTOR * (B * (Np - FRONT)) // N_EXPERTS
        gate_g, idx_g = lax.top_k(a_g.T, cap)
        idx_g, gate_g = lax.sort((idx_g, gate_g), dimension=1, num_keys=1)
        idx_parts.append(idx_g + off)
        gate_parts.append(gate_g)
    idx = jnp.concatenate(idx_parts, axis=1)
    gate = jnp.concatenate(gate_parts, axis=1)
    c_tot = idx.shape[1]
    nt = -(-c_tot // EXPERT_ROW_TILE_MAX)
    tmx = -(-(-(-c_tot // nt)) // SLOT_ALIGN) * SLOT_ALIGN
    ct = nt * tmx
    tok = jnp.pad(idx, ((0, 0), (0, ct - c_tot)), constant_values=R)
    gate = jnp.pad(gate, ((0, 0), (0, ct - c_tot))).reshape(-1, 1)
    xs = jnp.take(h2, jnp.where(tok < R, tok, 0).reshape(-1), axis=0)
    tf = _pick(D, (256, 128))
    out = _expert_ffn(xs, gate, w_gate[0], w_up[0], w_down[0], tmx, tf)

    tmc = _pick(R, (256, LANES))
    edges = jnp.arange(R // tmc + 1, dtype=jnp.int32) * tmc
    pos = jnp.sum((tok[:, None, :] < edges[None, :, None]).astype(jnp.int32), axis=-1)
    pos = pos + (jnp.arange(N_EXPERTS, dtype=jnp.int32) * ct)[:, None]
    lo, hi = pos[:, :-1].T, pos[:, 1:].T
    npass = jnp.max(-(-(hi - lo) // COMBINE_SPAN), axis=1).astype(jnp.int32)
    y = _combine(out, tok.reshape(-1, 1).astype(jnp.int32), lo.reshape(-1), hi.reshape(-1), npass, R, tmc)

    return tuple(_final(x1, y, norm_final, B, Np, off) for (B, Np), off in zip(groups, offs))
```

```python
import functools
import math

import numpy as np
import jax
import jax.numpy as jnp
from jax import lax
from jax.experimental import pallas as pl
from jax.experimental.pallas import tpu as pltpu

F32 = jnp.float32
BF16 = jnp.bfloat16

N_META = 16
GRID_W = 64
HEAD_DIM = 128
N_Q_HEADS = 16
N_KV_HEADS = 4
Q_PER_KV = N_Q_HEADS // N_KV_HEADS
ATTN_WIDTH = N_Q_HEADS * HEAD_DIM
KV_WIDTH = N_KV_HEADS * HEAD_DIM
ROPE_THETA = 10000.0
RWKV_HEAD = 64
DECAY_LORA = 128
ICLR_LORA = 128
GATE_LORA = 480
N_EXPERTS = 16
CAPACITY_FACTOR = 2
NORM_EPS = 1e-6
GN_EPS = 64e-5

LANES = 128
FRONT = LANES - N_META
CHUNK = 64
PAIR = 2 * RWKV_HEAD
LORA_PAD = 768
GATE_PAD = LORA_PAD - DECAY_LORA - ICLR_LORA
VMEM_LIMIT = 56 * 1024 * 1024
NEG_BIG = -1e30
MATMUL_ROW_TILE = 768
EXPERT_ROW_TILE_MAX = 1040
ROW_TILES = (640, 512, 384, 256, 128)
KEY_TILES = (1024, 896, 768) + ROW_TILES
SINGLE_KEY_TILE_MAX = 2304


def _cparams(sem):
    return pltpu.CompilerParams(dimension_semantics=sem, vmem_limit_bytes=VMEM_LIMIT)


def _pick(n, cands):
    for c in cands:
        if n % c == 0:
            return c
    raise ValueError(f"no tile for {n} in {cands}")


def _split2(x):
    hi = x.astype(BF16)
    lo = (x - hi.astype(F32)).astype(BF16)
    return hi, lo


def _split3(x):
    hi = x.astype(BF16)
    r1 = x - hi.astype(F32)
    mid = r1.astype(BF16)
    lo = (r1 - mid.astype(F32)).astype(BF16)
    return hi, mid, lo


def _dot(a, b):
    return jnp.dot(a, b, preferred_element_type=F32)


def _dot_nt(a, b):
    return lax.dot_general(a, b, (((1,), (1,)), ((), ())), preferred_element_type=F32)


def _dot1(a, b):
    return _dot(a.astype(BF16), b.astype(BF16))


def _dot3(a, b):
    ah, al = _split2(a)
    bh, bl = _split2(b)
    return _dot(ah, bh) + (_dot(ah, bl) + _dot(al, bh))


def _dot_exact_lhs(a_bf16, b):
    hi, mid, lo = _split3(b)
    return _dot(a_bf16, hi) + (_dot(a_bf16, mid) + _dot(a_bf16, lo))


def _segsum(x, ones_bd):
    hi, lo = _split2(x)
    return _dot(hi, ones_bd) + _dot(lo, ones_bd)


def _sigmoid(x):
    return 1.0 / (1.0 + jnp.exp(-x))


BLK_TAIL, BLK_FRONT, BLK_PROMPT, BLK_SAMPLE = 0, 1, 2, 3


def _pack_norm_kernel(kind_ref, pb_ref, sb_ref, xp_ref, xs_ref, meta_ref, g_ref, x_o, h_o):
    kind = kind_ref[pl.program_id(0)]

    def emit(x):
        x_o[...] = x
        ms = jnp.mean(x * x, axis=-1, keepdims=True)
        h_o[...] = (x * lax.rsqrt(ms + NORM_EPS) * g_ref[...]).astype(h_o.dtype)

    @pl.when(kind == BLK_TAIL)
    def _():
        x_o[...] = jnp.zeros(x_o.shape, x_o.dtype)
        h_o[...] = jnp.zeros(h_o.shape, h_o.dtype)

    @pl.when(kind == BLK_FRONT)
    def _():
        emit(jnp.concatenate([jnp.zeros((FRONT, x_o.shape[1]), F32), meta_ref[...]], axis=0))

    pl.when(kind == BLK_PROMPT)(lambda: emit(xp_ref[...]))
    pl.when(kind == BLK_SAMPLE)(lambda: emit(xs_ref[...]))


def _pack_norm(x_prompt, x_sample, meta, g, R):
    D = x_prompt.shape[-1]
    kind, pb, sb = [], [], []
    for which, x in ((BLK_PROMPT, x_prompt), (BLK_SAMPLE, x_sample)):
        B, S = x.shape[0], x.shape[1]
        for b in range(B):
            for j in range(S // LANES + 1):
                kind.append(BLK_FRONT if j == 0 else which)
                src = b * (S // LANES) + max(j - 1, 0)
                pb.append(src if which == BLK_PROMPT else (pb[-1] if pb else 0))
                sb.append(src if which == BLK_SAMPLE else 0)
    n_blk = R // LANES
    kind += [BLK_TAIL] * (n_blk - len(kind))
    pb += [pb[-1]] * (n_blk - len(pb))
    sb += [sb[-1]] * (n_blk - len(sb))
    tables = [jnp.asarray(np.asarray(t, np.int32)) for t in (kind, pb, sb)]
    blk = pl.BlockSpec((LANES, D), lambda i, kind, pb, sb: (i, 0))
    grid_spec = pltpu.PrefetchScalarGridSpec(
        num_scalar_prefetch=3,
        grid=(n_blk,),
        in_specs=[pl.BlockSpec((LANES, D), lambda i, kind, pb, sb: (pb[i], 0)),
                  pl.BlockSpec((LANES, D), lambda i, kind, pb, sb: (sb[i], 0)),
                  pl.BlockSpec((N_META, D), lambda i, kind, pb, sb: (0, 0)),
                  pl.BlockSpec((1, D), lambda i, kind, pb, sb: (0, 0))],
        out_specs=[blk, blk],
    )
    return pl.pallas_call(
        _pack_norm_kernel,
        grid_spec=grid_spec,
        out_shape=[jax.ShapeDtypeStruct((R, D), F32), jax.ShapeDtypeStruct((R, D), BF16)],
        compiler_params=_cparams(("arbitrary",)),
        name="pack_rmsnorm",
    )(*tables, x_prompt.reshape(-1, D), x_sample.reshape(-1, D), meta.astype(F32), g.reshape(1, D))


def _mm_kernel(*refs, epilogue, n_extra):
    x_ref, w_ref = refs[0], refs[1]
    extras = refs[2:2 + n_extra]
    o_ref, w_sc = refs[2 + n_extra], refs[3 + n_extra]

    @pl.when(pl.program_id(1) == 0)
    def _():
        w_sc[...] = w_ref[...].astype(w_sc.dtype)

    acc = _dot(x_ref[...], w_sc[...])
    if epilogue is not None:
        acc = epilogue(acc, *extras)
    o_ref[...] = acc.astype(o_ref.dtype)


def _matmul(x, w, *, tm, out_dtype, name, col0=0, n_out=None, tn=None, epilogue=None, extras=()):
    M, K = x.shape
    N = n_out or w.shape[-1]
    tn = tn or _pick(N, (512, 384, 256, 128))
    assert col0 % tn == 0 and N % tn == 0
    c0 = col0 // tn
    if w.ndim == 3:
        w_spec = pl.BlockSpec((None, K, tn), lambda j, i: (0, 0, c0 + j))
    else:
        w_spec = pl.BlockSpec((K, tn), lambda j, i: (0, c0 + j))
    in_specs = [pl.BlockSpec((tm, K), lambda j, i: (i, 0)), w_spec]
    in_specs += [pl.BlockSpec(shape, functools.partial(lambda j, i, fn: fn(i, j), fn=fn)) for _, shape, fn in extras]
    return pl.pallas_call(
        functools.partial(_mm_kernel, epilogue=epilogue, n_extra=len(extras)),
        grid=(N // tn, M // tm),
        in_specs=in_specs,
        out_specs=pl.BlockSpec((tm, tn), lambda j, i: (i, j)),
        out_shape=jax.ShapeDtypeStruct((M, N), out_dtype),
        scratch_shapes=[pltpu.VMEM((K, tn), BF16)],
        compiler_params=_cparams(("parallel", "arbitrary")),
        name=name,
    )(x, w, *[a for a, _, _ in extras])


def _qk_epilogue(acc, g_ref, c_ref, s_ref, *, scale):
    g = g_ref[...]
    c = c_ref[...]
    s = s_ref[...]
    outs = []
    for h in range(acc.shape[1] // HEAD_DIM):
        y = acc[:, h * HEAD_DIM:(h + 1) * HEAD_DIM]
        y = y * lax.rsqrt(jnp.mean(y * y, axis=-1, keepdims=True) + NORM_EPS) * g
        y = y * c + pltpu.roll(y, HEAD_DIM // 2, 1) * s
        outs.append(y * scale if scale != 1.0 else y)
    return jnp.concatenate(outs, axis=1) if len(outs) > 1 else outs[0]


def _sigmoid_epilogue(acc):
    return _sigmoid(acc)


def _residual_epilogue(acc, x_ref):
    return x_ref[...] + acc


def _attn_kernel(q_ref, k_ref, vt_ref, bias_ref, _prev_ref, o_ref, m_sc, l_sc, acc_sc, *, tq, nk):
    j = pl.program_id(3)
    M = Q_PER_KV * tq

    @pl.when(j == 0)
    def _():
        m_sc[...] = jnp.full(m_sc.shape, NEG_BIG, F32)
        l_sc[...] = jnp.zeros(l_sc.shape, F32)
        acc_sc[...] = jnp.zeros(acc_sc.shape, F32)

    q = q_ref[...]
    q4 = jnp.concatenate([q[:, h * HEAD_DIM:(h + 1) * HEAD_DIM] for h in range(Q_PER_KV)], axis=0)

    def step(use_bias):
        s = _dot_nt(k_ref[...], q4)
        if use_bias:
            s = s + jnp.tile(bias_ref[...], (1, M // LANES))
        m_prev = m_sc[...]
        m_new = jnp.maximum(m_prev, jnp.max(s, axis=0, keepdims=True))
        alpha = jnp.exp2(m_prev - m_new)
        p = jnp.exp2(s - m_new)
        l_sc[...] = alpha * l_sc[...] + jnp.sum(p, axis=0, keepdims=True)
        acc_sc[...] = alpha * acc_sc[...] + _dot(vt_ref[...], p.astype(BF16))
        m_sc[...] = m_new

    if nk == 1:
        step(True)
    else:
        pl.when(j == 0)(functools.partial(step, True))
        pl.when(j > 0)(functools.partial(step, False))

    @pl.when(j == nk - 1)
    def _():
        o = (acc_sc[...] / l_sc[...]).T
        o_ref[...] = jnp.concatenate([o[h * tq:(h + 1) * tq] for h in range(Q_PER_KV)], axis=1).astype(o_ref.dtype)


def _attention(q, k, vt, bias, B, Np, row0, out_prev):
    tq = _pick(math.gcd(Np, row0), ROW_TILES)
    tk = Np if Np <= SINGLE_KEY_TILE_MAX else _pick(Np, KEY_TILES)
    nq, nk = Np // tq, Np // tk
    q0 = row0 // tq
    M = Q_PER_KV * tq
    return pl.pallas_call(
        functools.partial(_attn_kernel, tq=tq, nk=nk),
        grid=(B, N_KV_HEADS, nq, nk),
        in_specs=[
            pl.BlockSpec((tq, Q_PER_KV * HEAD_DIM), lambda b, g, i, j: (q0 + b * nq + i, g)),
            pl.BlockSpec((tk, HEAD_DIM), lambda b, g, i, j: (b * nk + j, g)),
            pl.BlockSpec((HEAD_DIM, tk), lambda b, g, i, j: (g, b * nk + j)),
            pl.BlockSpec((tk, LANES), lambda b, g, i, j: (j, 0)),
            pl.BlockSpec(memory_space=pl.ANY),
        ],
        out_specs=pl.BlockSpec((tq, Q_PER_KV * HEAD_DIM), lambda b, g, i, j: (q0 + b * nq + i, g)),
        out_shape=jax.ShapeDtypeStruct(out_prev.shape, out_prev.dtype),
        input_output_aliases={4: 0},
        scratch_shapes=[
            pltpu.VMEM((1, M), F32),
            pltpu.VMEM((1, M), F32),
            pltpu.VMEM((HEAD_DIM, M), F32),
        ],
        compiler_params=_cparams(("parallel", "parallel", "parallel", "arbitrary")),
        name="attention",
    )(q, k, vt, bias, out_prev)


V_SP, V_SN, V_DBASE, V_IBASE, V_KK, V_KA, V_RK, V_LNW, V_LNB = 0, 3, 6, 8, 10, 11, 12, 13, 14
VEC_ROWS = 16
HALO = 8


def _prep_kernel(r_ref, k_ref, v_ref, rp_ref, rn_ref, kp_ref, kn_ref, vp_ref, vn_ref, lora_ref,
                 dup0_ref, dup1_ref, iup0_ref, iup1_ref, gup_ref, vec_ref, ones_ref,
                 r_o, v_o, kk_o, ew0_o, kd0_o, b0_o, ew1_o, kd1_o, b1_o, bonus_o, g_o, *, tm, n_row_tiles):
    i = pl.program_id(0)
    has_prev = (i > 0).astype(F32)
    has_next = (i < n_row_tiles - 1).astype(F32)
    vec = vec_ref[...]
    row = lax.broadcasted_iota(jnp.int32, (tm, 1), 0)

    def shifted(x_ref, p_ref, n_ref, idx):
        x = x_ref[...]
        prev = jnp.where(row == 0, p_ref[HALO - 1:HALO, :] * has_prev, pltpu.roll(x, 1, 0))
        nxt = jnp.where(row == tm - 1, n_ref[0:1, :] * has_next, pltpu.roll(x, tm - 1, 0))
        return x + vec[V_SP + idx:V_SP + idx + 1] * (prev - x) + vec[V_SN + idx:V_SN + idx + 1] * (nxt - x)

    r = shifted(r_ref, rp_ref, rn_ref, 0)
    k = shifted(k_ref, kp_ref, kn_ref, 1)
    v = shifted(v_ref, vp_ref, vn_ref, 2)
    ones = ones_ref[...]

    lora = lora_ref[...]
    decay_h = jnp.tanh(lora[:, 0:DECAY_LORA])
    iclr_h = lora[:, DECAY_LORA:DECAY_LORA + ICLR_LORA]
    gate_h = _sigmoid(lora[:, DECAY_LORA + ICLR_LORA:LORA_PAD])

    kkr = k * vec[V_KK:V_KK + 1]
    kk = kkr / jnp.maximum(jnp.sqrt(_segsum(kkr * kkr, ones)), 1e-12)
    r_o[...] = r.astype(r_o.dtype)
    v_o[...] = v.astype(v_o.dtype)
    kk_o[...] = kk.astype(kk_o.dtype)
    bonus_o[...] = (_segsum(r * k * vec[V_RK:V_RK + 1], ones) * v).astype(bonus_o.dtype)
    g_o[...] = _dot(gate_h.astype(BF16), gup_ref[...]).astype(g_o.dtype)

    for d, (dup_ref, iup_ref, ew_o, kd_o, b_o) in enumerate(
            ((dup0_ref, iup0_ref, ew0_o, kd0_o, b0_o), (dup1_ref, iup1_ref, ew1_o, kd1_o, b1_o))):
        u = -(vec[V_DBASE + d:V_DBASE + d + 1] + _dot1(decay_h, dup_ref[...]))
        softplus = jnp.maximum(u, 0.0) + jnp.log(1.0 + jnp.exp(-jnp.abs(u)))
        ew_o[...] = jnp.exp(-softplus - 0.5)
        a = _sigmoid(vec[V_IBASE + d:V_IBASE + d + 1] + _dot1(iclr_h, iup_ref[...]))
        kd_o[...] = (k * (1.0 + (a - 1.0) * vec[V_KA:V_KA + 1])).astype(kd_o.dtype)
        b_o[...] = (kk * a).astype(b_o.dtype)


def _rwkv_prep(rkv, lora, dup, iup, gup, vec, ones_bd, tm, cn):
    R = rkv.shape[0]
    RW = rkv.shape[1] // 3
    J = RW // cn
    n_row_tiles = R // tm
    hb = tm // HALO
    last_hb = R // HALO - 1

    def main(c):
        return pl.BlockSpec((tm, cn), lambda i, j: (i, c * J + j))

    def prev(c):
        return pl.BlockSpec((HALO, cn), lambda i, j: (jnp.maximum(i * hb - 1, 0), c * J + j))

    def nxt(c):
        return pl.BlockSpec((HALO, cn), lambda i, j: (jnp.minimum((i + 1) * hb, last_hb), c * J + j))

    up = pl.BlockSpec((DECAY_LORA, cn), lambda i, j: (0, j))
    in_specs = [main(0), main(1), main(2), prev(0), nxt(0), prev(1), nxt(1), prev(2), nxt(2),
                pl.BlockSpec((tm, LORA_PAD), lambda i, j: (i, 0)),
                up, up, up, up,
                pl.BlockSpec((GATE_PAD, cn), lambda i, j: (0, j)),
                pl.BlockSpec((VEC_ROWS, cn), lambda i, j: (0, j)),
                pl.BlockSpec((cn, cn), lambda i, j: (0, 0))]
    out_spec = pl.BlockSpec((tm, cn), lambda i, j: (i, j))
    out_dtypes = [BF16, BF16, BF16, F32, BF16, BF16, F32, BF16, BF16, BF16, BF16]
    return pl.pallas_call(
        functools.partial(_prep_kernel, tm=tm, n_row_tiles=n_row_tiles),
        grid=(n_row_tiles, J),
        in_specs=in_specs,
        out_specs=[out_spec] * len(out_dtypes),
        out_shape=[jax.ShapeDtypeStruct((R, RW), dt) for dt in out_dtypes],
        compiler_params=_cparams(("parallel", "parallel")),
        name="rwkv_prep",
    )(rkv, rkv, rkv, rkv, rkv, rkv, rkv, rkv, rkv, lora, dup[0], dup[1], iup[0], iup[1], gup, vec, ones_bd)


def _wkv_kernel(first_ref, r_ref, v_ref, kk_ref, ew_ref, kd_ref, b_ref, tri_ref, y_ref, h_sc, *, reverse, npairs):
    c = pl.program_id(1)

    @pl.when(first_ref[c] == 1)
    def _():
        h_sc[...] = jnp.zeros(h_sc.shape, F32)

    L = CHUNK
    ew = ew_ref[...]
    cs = _dot_exact_lhs(tri_ref[...], ew)
    tot = cs[0:1] if reverse else cs[L - 1:L]
    e_neg = jnp.exp(-cs)
    e_prev = jnp.exp(ew - cs)
    e_pos = jnp.exp(cs)
    e_fin = jnp.exp(cs - tot)
    w_tot = jnp.exp(-tot)

    ri = lax.broadcasted_iota(jnp.int32, (PAIR, PAIR), 0)
    ci = lax.broadcasted_iota(jnp.int32, (PAIR, PAIR), 1)
    same = (ri // L) == (ci // L)
    t_i = ri % L
    s_i = ci % L
    if reverse:
        strict = same & (s_i > t_i)
        incl = same & (s_i >= t_i)
    else:
        strict = same & (s_i < t_i)
        incl = same & (s_i <= t_i)
    eye = ri == ci
    head0 = lax.broadcasted_iota(jnp.int32, (L, PAIR), 1) < RWKV_HEAD

    def stack(x):
        return jnp.concatenate([jnp.where(head0, x, 0.0), jnp.where(head0, 0.0, x)], axis=0)

    P2 = 2 * PAIR
    pairs = range(npairs)
    sls = [slice(p * PAIR, (p + 1) * PAIR) for p in pairs]

    r_st, a_st, v_st, bk2, kf_t, bf_t = [], [], [], [], [], []
    for sl in sls:
        kd = kd_ref[:, sl].astype(F32)
        b = b_ref[:, sl].astype(F32)
        bt = (b * e_pos[:, sl]).astype(BF16)
        kt = (kd * e_pos[:, sl]).astype(BF16)
        r_st.append(stack(r_ref[:, sl].astype(F32) * e_neg[:, sl]))
        a_st.append(stack(-kk_ref[:, sl].astype(F32) * e_prev[:, sl]))
        v_st.append(stack(v_ref[:, sl].astype(F32)).astype(BF16))
        bk2.append(jnp.concatenate([bt, bt, kt, kt], axis=0))
        kf_t.append(stack(kd * e_fin[:, sl]).T)
        bf_t.append(stack(b * e_fin[:, sl]).T)

    a_ab, a_ak, wlhs = [], [], []
    for p in pairs:
        sc = _dot_nt(jnp.concatenate([a_st[p], r_st[p]], axis=0).astype(BF16), bk2[p])
        a_ab.append(jnp.where(strict, sc[0:PAIR, 0:PAIR], 0.0))
        a_ak.append(jnp.where(strict, sc[0:PAIR, PAIR:P2], 0.0).astype(BF16))
        m_rb = jnp.where(incl, sc[PAIR:P2, 0:PAIR], 0.0)
        m_rk = jnp.where(incl, sc[PAIR:P2, PAIR:P2], 0.0)
        wlhs.append(jnp.concatenate([jnp.concatenate([m_rk, m_rb], axis=1),
                                     jnp.concatenate([kf_t[p], bf_t[p]], axis=1)], axis=0).astype(BF16))

    x = [jnp.concatenate([a_st[p], _dot(a_ak[p], v_st[p])], axis=1) for p in pairs]
    npow = a_ab
    n_sq = int(math.log2(L)) - 1
    for it in range(n_sq + 1):
        lhs = [npow[p].astype(BF16) for p in pairs]
        if it < n_sq:
            prod = [_dot(lhs[p], jnp.concatenate([lhs[p], x[p].astype(BF16)], axis=1)) for p in pairs]
            npow = [prod[p][:, 0:PAIR] for p in pairs]
            x = [x[p] + prod[p][:, PAIR:] for p in pairs]
        else:
            x = [x[p] + _dot(lhs[p], x[p].astype(BF16)) for p in pairs]

    zero = jnp.zeros((PAIR, PAIR), BF16)
    for p in pairs:
        sl = sls[p]
        rhs = jnp.concatenate([jnp.concatenate([zero, v_st[p]], axis=1), x[p].astype(BF16)], axis=0)
        w = _dot(wlhs[p], rhs)
        g_st = r_st[p] + w[0:PAIR, 0:PAIR]
        y0_st = w[0:PAIR, PAIR:P2]
        phi = jnp.where(eye, w_tot[:, sl], 0.0) + w[PAIR:P2, 0:PAIR]
        psi = w[PAIR:P2, PAIR:P2]
        out = _dot1(jnp.concatenate([g_st, phi], axis=0), h_sc[p])
        y_st = out[0:PAIR] + y0_st
        h_sc[p] = out[PAIR:P2] + psi
        y_ref[:, sl] = y_st[0:L] + y_st[L:2 * L]


def _wkv(first, r, v, kk, ew, kd, b, tri, *, reverse, npairs):
    R, RW = r.shape
    NC = R // CHUNK
    width = npairs * PAIR
    if reverse:
        blk = pl.BlockSpec((CHUNK, width), lambda g, c, f: (NC - 1 - c, g))
    else:
        blk = pl.BlockSpec((CHUNK, width), lambda g, c, f: (c, g))
    grid_spec = pltpu.PrefetchScalarGridSpec(
        num_scalar_prefetch=1,
        grid=(RW // width, NC),
        in_specs=[blk] * 6 + [pl.BlockSpec((CHUNK, CHUNK), lambda g, c, f: (0, 0))],
        out_specs=blk,
        scratch_shapes=[pltpu.VMEM((npairs, PAIR, PAIR), F32)],
    )
    return pl.pallas_call(
        functools.partial(_wkv_kernel, reverse=reverse, npairs=npairs),
        grid_spec=grid_spec,
        out_shape=jax.ShapeDtypeStruct((R, RW), F32),
        compiler_params=_cparams(("parallel", "arbitrary")),
        name="wkv_rev" if reverse else "wkv_fwd",
    )(first, r, v, kk, ew, kd, b, tri)


def _post_kernel(yf_ref, yb_ref, bonus_ref, g_ref, vec_ref, ones_ref, o_ref):
    ones = ones_ref[...]
    vec = vec_ref[...]
    y = yf_ref[...] + yb_ref[...]
    inv_n = 1.0 / RWKV_HEAD
    mu = _segsum(y, ones) * inv_n
    d = y - mu
    var = _segsum(d * d, ones) * inv_n
    yn = d * lax.rsqrt(var + GN_EPS) * vec[V_LNW:V_LNW + 1] + vec[V_LNB:V_LNB + 1]
    o_ref[...] = ((yn + bonus_ref[...]) * g_ref[...]).astype(o_ref.dtype)


def _rwkv_post(yf, yb, bonus, g, vec, ones_bd, tm, cn):
    R, RW = yf.shape
    blk = pl.BlockSpec((tm, cn), lambda i, j: (i, j))
    return pl.pallas_call(
        _post_kernel,
        grid=(R // tm, RW // cn),
        in_specs=[blk, blk, blk, blk, pl.BlockSpec((VEC_ROWS, cn), lambda i, j: (0, j)),
                  pl.BlockSpec((cn, cn), lambda i, j: (0, 0))],
        out_specs=blk,
        out_shape=jax.ShapeDtypeStruct((R, RW), BF16),
        compiler_params=_cparams(("parallel", "parallel")),
        name="rwkv_post",
    )(yf, yb, bonus, g, vec, ones_bd)


def _merge_kernel(a_ref, r_ref, wa_ref, wr_ref, ga_ref, gr_ref, o_ref, wa_sc, wr_sc):
    @pl.when(pl.program_id(1) == 0)
    def _():
        wa_sc[...] = wa_ref[...].astype(wa_sc.dtype)
        wr_sc[...] = wr_ref[...].astype(wr_sc.dtype)

    ya = _dot(a_ref[...], wa_sc[...])
    yr = _dot(r_ref[...], wr_sc[...])
    o_ref[...] = (ga_ref[...].astype(F32) * ya + gr_ref[...].astype(F32) * yr).astype(o_ref.dtype)


def _merge(attn, rwkv, wa, wr, gates, tm, tn):
    R = attn.shape[0]
    D = wa.shape[1]
    J = D // tn
    return pl.pallas_call(
        _merge_kernel,
        grid=(J, R // tm),
        in_specs=[
            pl.BlockSpec((tm, attn.shape[1]), lambda j, i: (i, 0)),
            pl.BlockSpec((tm, rwkv.shape[1]), lambda j, i: (i, 0)),
            pl.BlockSpec((wa.shape[0], tn), lambda j, i: (0, j)),
            pl.BlockSpec((wr.shape[0], tn), lambda j, i: (0, j)),
            pl.BlockSpec((tm, tn), lambda j, i: (i, j)),
            pl.BlockSpec((tm, tn), lambda j, i: (i, J + j)),
        ],
        out_specs=pl.BlockSpec((tm, tn), lambda j, i: (i, j)),
        out_shape=jax.ShapeDtypeStruct((R, D), BF16),
        scratch_shapes=[pltpu.VMEM((wa.shape[0], tn), BF16), pltpu.VMEM((wr.shape[0], tn), BF16)],
        compiler_params=_cparams(("parallel", "arbitrary")),
        name="merge",
    )(attn, rwkv, wa, wr, gates, gates)


def _router_kernel(x_ref, g_ref, w_ref, h_ref, aff_ref):
    x = x_ref[...]
    ms = jnp.mean(x * x, axis=-1, keepdims=True)
    h = x * lax.rsqrt(ms + NORM_EPS) * g_ref[...]
    h_ref[...] = h.astype(h_ref.dtype)
    logits = _dot3(h, w_ref[...])
    lane = lax.broadcasted_iota(jnp.int32, logits.shape, 1)
    logits = jnp.where(lane < N_EXPERTS, logits, NEG_BIG)
    e = jnp.exp(logits - jnp.max(logits, axis=-1, keepdims=True))
    aff_ref[...] = e / jnp.sum(e, axis=-1, keepdims=True)


def _router(x1, g, w_router_pad, tm):
    R, D = x1.shape
    return pl.pallas_call(
        _router_kernel,
        grid=(R // tm,),
        in_specs=[pl.BlockSpec((tm, D), lambda i: (i, 0)), pl.BlockSpec((1, D), lambda i: (0, 0)),
                  pl.BlockSpec((D, LANES), lambda i: (0, 0))],
        out_specs=[pl.BlockSpec((tm, D), lambda i: (i, 0)), pl.BlockSpec((tm, LANES), lambda i: (i, 0))],
        out_shape=[jax.ShapeDtypeStruct((R, D), BF16), jax.ShapeDtypeStruct((R, LANES), F32)],
        compiler_params=_cparams(("parallel",)),
        name="ffn_norm_router",
    )(x1, g.reshape(1, D), w_router_pad)


def _ffn_up_kernel(x_ref, wg_ref, wu_ref, o_ref):
    x = x_ref[...]
    hg = _dot(x, wg_ref[...].astype(BF16))
    hu = _dot(x, wu_ref[...].astype(BF16))
    o_ref[...] = (hg * _sigmoid(hg) * hu).astype(o_ref.dtype)


def _ffn_down_kernel(h_ref, wd_ref, gate_ref, o_ref):
    o_ref[...] = (_dot(h_ref[...], wd_ref[...].astype(BF16)) * gate_ref[...]).astype(o_ref.dtype)


def _expert_ffn(xs, gate, w_gate, w_up, w_down, tm, tf):
    E, D, F = w_gate.shape
    nt = xs.shape[0] // (E * tm)
    wspec = pl.BlockSpec((None, D, tf), lambda e, i, f: (e, 0, f))
    hid = pl.pallas_call(
        _ffn_up_kernel,
        grid=(E, nt, F // tf),
        in_specs=[pl.BlockSpec((tm, D), lambda e, i, f: (e * nt + i, 0)), wspec, wspec],
        out_specs=pl.BlockSpec((tm, tf), lambda e, i, f: (e * nt + i, f)),
        out_shape=jax.ShapeDtypeStruct((xs.shape[0], F), BF16),
        compiler_params=_cparams(("parallel", "parallel", "parallel")),
        name="ffn_up",
    )(xs, w_gate, w_up)
    td = _pick(D, (2 * tf, tf))
    return pl.pallas_call(
        _ffn_down_kernel,
        grid=(E, nt, D // td),
        in_specs=[pl.BlockSpec((tm, F), lambda e, i, f: (e * nt + i, 0)),
                  pl.BlockSpec((None, F, td), lambda e, i, f: (e, 0, f)),
                  pl.BlockSpec((tm, 1), lambda e, i, f: (e * nt + i, 0))],
        out_specs=pl.BlockSpec((tm, td), lambda e, i, f: (e * nt + i, f)),
        out_shape=jax.ShapeDtypeStruct((xs.shape[0], D), BF16),
        compiler_params=_cparams(("parallel", "parallel", "parallel")),
        name="ffn_down",
    )(hid, w_down, gate)


COMBINE_CHUNK = 64
SLOT_ALIGN = 16
COMBINE_SPAN = COMBINE_CHUNK - SLOT_ALIGN


def _combine_kernel(lo_ref, hi_ref, npass_ref, out_hbm, tok_hbm, y_ref, buf, tbuf, sem, *, ns, tmc):
    i = pl.program_id(0)
    E, CH = N_EXPERTS, COMBINE_CHUNK
    base = i * tmc
    lane = lax.broadcasted_iota(jnp.int32, (CH, tmc), 1)
    srow = lax.broadcasted_iota(jnp.int32, (CH, 1), 0)
    y_ref[...] = jnp.zeros(y_ref.shape, F32)

    def one_pass(p, carry):
        copies, bounds = [], []
        for e in range(E):
            a = lo_ref[i * E + e] + p * COMBINE_SPAN
            b = jnp.minimum(a + COMBINE_SPAN, hi_ref[i * E + e])
            start = pl.multiple_of(jnp.clip((a // SLOT_ALIGN) * SLOT_ALIGN, 0, ns - CH), SLOT_ALIGN)
            rows = pl.ds(e * CH, CH)
            pair = (pltpu.make_async_copy(out_hbm.at[pl.ds(start, CH), :], buf.at[rows, :], sem.at[0, e]),
                    pltpu.make_async_copy(tok_hbm.at[pl.ds(start, CH), :], tbuf.at[rows, :], sem.at[1, e]))
            pair[0].start()
            pair[1].start()
            copies.append(pair)
            bounds.append((start, a, b))
        blocks = []
        for e in range(E):
            copies[e][0].wait()
            copies[e][1].wait()
            start, a, b = bounds[e]
            slot = start + srow
            hit = (tbuf[pl.ds(e * CH, CH), :] - base == lane) & (slot >= a) & (slot < b)
            blocks.append(jnp.where(hit, 1.0, 0.0))
        onehot = jnp.concatenate(blocks, axis=0).T.astype(BF16)
        y_ref[...] += _dot(onehot, buf[...])
        return carry

    lax.fori_loop(0, npass_ref[i], one_pass, 0)


def _combine(out, tok, lo, hi, npass, R, tmc):
    ns, D = out.shape
    grid_spec = pltpu.PrefetchScalarGridSpec(
        num_scalar_prefetch=3,
        grid=(R // tmc,),
        in_specs=[pl.BlockSpec(memory_space=pl.ANY), pl.BlockSpec(memory_space=pl.ANY)],
        out_specs=pl.BlockSpec((tmc, D), lambda i, lo, hi, n: (i, 0)),
        scratch_shapes=[pltpu.VMEM((N_EXPERTS * COMBINE_CHUNK, D), BF16),
                        pltpu.VMEM((N_EXPERTS * COMBINE_CHUNK, 1), jnp.int32),
                        pltpu.SemaphoreType.DMA((2, N_EXPERTS))],
    )
    return pl.pallas_call(
        functools.partial(_combine_kernel, ns=ns, tmc=tmc),
        grid_spec=grid_spec,
        out_shape=jax.ShapeDtypeStruct((R, D), F32),
        compiler_params=_cparams(("arbitrary",)),
        name="expert_combine",
    )(lo, hi, npass, out, tok)


def _final_kernel(x_ref, y_ref, g_ref, o_ref):
    x = x_ref[...] + y_ref[...]
    ms = jnp.mean(x * x, axis=-1, keepdims=True)
    o_ref[...] = x * lax.rsqrt(ms + NORM_EPS) * g_ref[...]


def _final(x1, y, g, B, Np, off):
    D = x1.shape[1]
    nb = Np // LANES
    base = off // LANES
    blk = pl.BlockSpec((LANES, D), lambda b, j: (base + b * nb + 1 + j, 0))
    return pl.pallas_call(
        _final_kernel,
        grid=(B, nb - 1),
        in_specs=[blk, blk, pl.BlockSpec((1, D), lambda b, j: (0, 0))],
        out_specs=pl.BlockSpec((None, LANES, D), lambda b, j: (b, j, 0)),
        out_shape=jax.ShapeDtypeStruct((B, Np - LANES, D), F32),
        compiler_params=_cparams(("parallel", "parallel")),
        name="final_norm",
    )(x1, y, g.reshape(1, D))


def _rope_tables(S, B):
    rows = S // GRID_W
    row_ids = jnp.repeat(jnp.arange(rows, dtype=F32), GRID_W)
    col_ids = jnp.tile(jnp.arange(GRID_W, dtype=F32), rows)
    half = HEAD_DIM // 2
    inv_freq = 1.0 / (ROPE_THETA ** (jnp.arange(0, half, 2, dtype=F32) / half))
    ang = jnp.concatenate([row_ids[:, None] * inv_freq, col_ids[:, None] * inv_freq], axis=-1)
    ang = jnp.concatenate([jnp.zeros((LANES, half), F32), ang], axis=0)
    c, s = jnp.cos(ang), jnp.sin(ang)
    return (jnp.tile(jnp.concatenate([c, c], axis=-1), (B, 1)),
            jnp.tile(jnp.concatenate([-s, s], axis=-1), (B, 1)))


def _key_bias(Np):
    col = np.where(np.arange(Np) < FRONT, NEG_BIG, 0.0).astype(np.float32)
    return jnp.asarray(np.repeat(col[:, None], LANES, axis=1))


def _seq_flags(groups):
    fwd = []
    for B, Np in groups:
        nc = Np // CHUNK
        for _ in range(B):
            fwd += [1] + [0] * (nc - 1)
    fwd = np.asarray(fwd, np.int32)
    last = np.roll(fwd, -1)
    return jnp.asarray(fwd), jnp.asarray(last[::-1].copy())


def kernel(x_prompt, x_sample, meta_tokens, norm_mix, w_in, q_norm, k_norm, shift_prev, shift_next, decay_up, decay_base, iclr_up, iclr_base, gate_up, k_k, k_a, r_k, ln_x_w, ln_x_b, w_branch_attn, w_branch_rwkv, w_out, norm_ffn, w_router, w_gate, w_up, w_down, norm_final):
    assert norm_mix.shape[0] == 1, "one layer"
    D = x_prompt.shape[-1]
    RW = D // 2
    groups = [(x.shape[0], x.shape[1] + LANES) for x in (x_prompt, x_sample)]
    seqs = [x.shape[1] for x in (x_prompt, x_sample)]
    assert all(s % LANES == 0 for s in seqs)
    offs = [0, groups[0][0] * groups[0][1]]
    r_used = offs[1] + groups[1][0] * groups[1][1]
    tm = MATMUL_ROW_TILE if r_used >= 8 * MATMUL_ROW_TILE else LANES
    tme = tm // 3 if tm % 3 == 0 else tm
    tmr = tm // 2 if tm % 256 == 0 else tm
    R = -(-r_used // tm) * tm
    tail = R - r_used

    x, h = _pack_norm(x_prompt, x_sample, meta_tokens, norm_mix[0], R)
    tabs = [_rope_tables(S, B) for S, (B, _) in zip(seqs, groups)]
    cos_t = jnp.concatenate([t[0] for t in tabs] + [jnp.zeros((tail, HEAD_DIM), F32)], axis=0)
    sin_t = jnp.concatenate([t[1] for t in tabs] + [jnp.zeros((tail, HEAD_DIM), F32)], axis=0)

    n_in = w_in.shape[-1]

    def in_proj(col0, n_out, **kw):
        tn = _pick(n_out, (512, 384, 256, 128))
        if col0 % tn == 0 and col0 + n_out <= n_in:
            return _matmul(h, w_in, col0=col0, n_out=n_out, tn=tn, tm=tm, **kw)
        w_cols = w_in[0, :, col0:min(col0 + n_out, n_in)]
        w_cols = jnp.pad(w_cols, ((0, 0), (0, n_out - w_cols.shape[1])))
        return _matmul(h, w_cols, tn=tn, tm=tm, **kw)

    rope_specs = lambda g: [(g.reshape(1, HEAD_DIM), (1, HEAD_DIM), lambda i, j: (0, 0)),
                            (cos_t, (tm, HEAD_DIM), lambda i, j: (i, 0)),
                            (sin_t, (tm, HEAD_DIM), lambda i, j: (i, 0))]
    q_scale = HEAD_DIM ** -0.5 * math.log2(math.e)
    c_k = ATTN_WIDTH
    c_v = c_k + KV_WIDTH
    c_rkv = c_v + KV_WIDTH
    c_lora = c_rkv + 3 * RW
    c_merge = c_lora + DECAY_LORA + ICLR_LORA + GATE_LORA
    q = in_proj(0, ATTN_WIDTH, out_dtype=BF16, name="proj_q",
                epilogue=functools.partial(_qk_epilogue, scale=q_scale), extras=rope_specs(q_norm[0]))
    k = in_proj(c_k, KV_WIDTH, out_dtype=BF16, name="proj_k",
                epilogue=functools.partial(_qk_epilogue, scale=1.0), extras=rope_specs(k_norm[0]))
    v = in_proj(c_v, KV_WIDTH, out_dtype=BF16, name="proj_v")
    rkv = in_proj(c_rkv, 3 * RW, out_dtype=F32, name="proj_rkv")
    lora = in_proj(c_lora, LORA_PAD, out_dtype=F32, name="proj_lora")
    gates = in_proj(c_merge, 2 * D, out_dtype=BF16, name="proj_merge_gates", epilogue=_sigmoid_epilogue)

    attn = jnp.zeros((R, ATTN_WIDTH), BF16)
    for (B, Np), off in zip(groups, offs):
        sl = slice(off, off + B * Np)
        attn = _attention(q, k[sl], v[sl].T, _key_bias(Np), B, Np, off, attn)

    cn = _pick(RW, (256, 128))
    vec = jnp.concatenate([shift_prev[0], shift_next[0], decay_base[0], iclr_base[0], k_k, k_a, r_k,
                           ln_x_w, ln_x_b, jnp.zeros((VEC_ROWS - 15, RW), F32)], axis=0).astype(F32)
    hid_idx = np.arange(cn) // RWKV_HEAD
    ones_bd = jnp.asarray((hid_idx[:, None] == hid_idx[None, :]).astype(np.float32)).astype(BF16)
    gup = jnp.pad(gate_up[0], ((0, GATE_PAD - GATE_LORA), (0, 0))).astype(BF16)
    r_s, v_s, kk, ew0, kd0, b0, ew1, kd1, b1, bonus, g_rwkv = _rwkv_prep(
        rkv, lora, decay_up[0], iclr_up[0], gup, vec, ones_bd, tmr, cn)
    first_fwd, first_rev = _seq_flags(groups + ([(1, tail)] if tail else []))
    t_idx = np.arange(CHUNK)
    tri_f = jnp.asarray((t_idx[None, :] <= t_idx[:, None]).astype(np.float32)).astype(BF16)
    tri_r = jnp.asarray((t_idx[None, :] >= t_idx[:, None]).astype(np.float32)).astype(BF16)
    npairs = _pick(RW // PAIR, (16, 8, 4, 2, 1))
    y_f = _wkv(first_fwd, r_s, v_s, kk, ew0, kd0, b0, tri_f, reverse=False, npairs=npairs)
    y_b = _wkv(first_rev, r_s, v_s, kk, ew1, kd1, b1, tri_r, reverse=True, npairs=npairs)
    rwkv = _rwkv_post(y_f, y_b, bonus, g_rwkv, vec, ones_bd, tmr, cn)

    tn_d = _pick(D, (512, 384, 256, 128))
    merged = _merge(attn, rwkv, w_branch_attn[0], w_branch_rwkv[0], gates, tm, tn_d)
    x1 = _matmul(merged, w_out, tm=tm, tn=tn_d, out_dtype=F32, name="out_proj",
                 epilogue=_residual_epilogue, extras=[(x, (tm, tn_d), lambda i, j: (i, j))])

    w_router_pad = jnp.pad(w_router[0], ((0, 0), (0, LANES - N_EXPERTS)))
    h2, aff = _router(x1, norm_ffn[0], w_router_pad, tme)
    aff = aff[:, :N_EXPERTS]
    idx_parts, gate_parts = [], []
    for (B, Np), off in zip(groups, offs):
        valid = jnp.asarray((np.arange(B * Np) % Np) >= FRONT)
        a_g = jnp.where(valid[:, None], aff[off:off + B * Np], -1.0)
        cap = CAPACITY_FACTOR * (B * (Np - FRONT)) // N_EXPERTS
        gate_g, idx_g = lax.top_k(a_g.T, cap)
        idx_g, gate_g = lax.sort((idx_g, gate_g), dimension=1, num_keys=1)
        idx_parts.append(idx_g + off)
        gate_parts.append(gate_g)
    idx = jnp.concatenate(idx_parts, axis=1)
    gate = jnp.concatenate(gate_parts, axis=1)
    c_tot = idx.shape[1]
    nt = -(-c_tot // EXPERT_ROW_TILE_MAX)
    tmx = -(-(-(-c_tot // nt)) // SLOT_ALIGN) * SLOT_ALIGN
    ct = nt * tmx
    tok = jnp.pad(idx, ((0, 0), (0, ct - c_tot)), constant_values=R)
    gate = jnp.pad(gate, ((0, 0), (0, ct - c_tot))).reshape(-1, 1)
    xs = h2.at[jnp.where(tok < R, tok, 0).reshape(-1)].get(mode="promise_in_bounds")
    tf = _pick(D, (256, 128))
    out = _expert_ffn(xs, gate, w_gate[0], w_up[0], w_down[0], tmx, tf)

    tmc = _pick(R, (256, LANES))
    edges = jnp.arange(R // tmc + 1, dtype=jnp.int32) * tmc
    pos = jnp.sum((tok[:, None, :] < edges[None, :, None]).astype(jnp.int32), axis=-1)
    pos = pos + (jnp.arange(N_EXPERTS, dtype=jnp.int32) * ct)[:, None]
    lo, hi = pos[:, :-1].T, pos[:, 1:].T
    npass = jnp.max(-(-(hi - lo) // COMBINE_SPAN), axis=1).astype(jnp.int32)
    y = _combine(out, tok.reshape(-1, 1).astype(jnp.int32), lo.reshape(-1), hi.reshape(-1), npass, R, tmc)

    return tuple(_final(x1, y, norm_final, B, Np, off) for (B, Np), off in zip(groups, offs))
```

```python
import functools
import math

import numpy as np
import jax
import jax.numpy as jnp
from jax import lax
from jax.experimental import pallas as pl
from jax.experimental.pallas import tpu as pltpu

F32 = jnp.float32
BF16 = jnp.bfloat16

N_META = 16
GRID_W = 64
HEAD_DIM = 128
N_Q_HEADS = 16
N_KV_HEADS = 4
Q_PER_KV = N_Q_HEADS // N_KV_HEADS
ATTN_WIDTH = N_Q_HEADS * HEAD_DIM
KV_WIDTH = N_KV_HEADS * HEAD_DIM
ROPE_THETA = 10000.0
RWKV_HEAD = 64
DECAY_LORA = 128
ICLR_LORA = 128
GATE_LORA = 480
N_EXPERTS = 16
CAPACITY_FACTOR = 2
NORM_EPS = 1e-6
GN_EPS = 64e-5

LANES = 128
FRONT = LANES - N_META
CHUNK = 64
PAIR = 2 * RWKV_HEAD
LORA_PAD = 768
GATE_PAD = LORA_PAD - DECAY_LORA - ICLR_LORA
VMEM_LIMIT = 56 * 1024 * 1024
NEG_BIG = -1e30
MATMUL_ROW_TILE = 768
EXPERT_ROW_TILE_MAX = 1040
ROW_TILES = (640, 512, 384, 256, 128)
KEY_TILES = (1024, 896, 768) + ROW_TILES
SINGLE_KEY_TILE_MAX = 2304


def _cparams(sem):
    return pltpu.CompilerParams(dimension_semantics=sem, vmem_limit_bytes=VMEM_LIMIT)


def _pick(n, cands):
    for c in cands:
        if n % c == 0:
            return c
    raise ValueError(f"no tile for {n} in {cands}")


def _split2(x):
    hi = x.astype(BF16)
    lo = (x - hi.astype(F32)).astype(BF16)
    return hi, lo


def _split3(x):
    hi = x.astype(BF16)
    r1 = x - hi.astype(F32)
    mid = r1.astype(BF16)
    lo = (r1 - mid.astype(F32)).astype(BF16)
    return hi, mid, lo


def _dot(a, b):
    return jnp.dot(a, b, preferred_element_type=F32)


def _dot_nt(a, b):
    return lax.dot_general(a, b, (((1,), (1,)), ((), ())), preferred_element_type=F32)


def _dot1(a, b):
    return _dot(a.astype(BF16), b.astype(BF16))


def _dot3(a, b):
    ah, al = _split2(a)
    bh, bl = _split2(b)
    return _dot(ah, bh) + (_dot(ah, bl) + _dot(al, bh))


def _dot_exact_lhs(a_bf16, b):
    hi, mid, lo = _split3(b)
    return _dot(a_bf16, hi) + (_dot(a_bf16, mid) + _dot(a_bf16, lo))


def _segsum(x, ones_bd):
    hi, lo = _split2(x)
    return _dot(hi, ones_bd) + _dot(lo, ones_bd)


def _sigmoid(x):
    return 1.0 / (1.0 + jnp.exp(-x))


BLK_TAIL, BLK_FRONT, BLK_PROMPT, BLK_SAMPLE = 0, 1, 2, 3


def _pack_norm_kernel(kind_ref, pb_ref, sb_ref, xp_ref, xs_ref, meta_ref, g_ref, x_o, h_o):
    kind = kind_ref[pl.program_id(0)]

    def emit(x):
        x_o[...] = x
        ms = jnp.mean(x * x, axis=-1, keepdims=True)
        h_o[...] = (x * lax.rsqrt(ms + NORM_EPS) * g_ref[...]).astype(h_o.dtype)

    @pl.when(kind == BLK_TAIL)
    def _():
        x_o[...] = jnp.zeros(x_o.shape, x_o.dtype)
        h_o[...] = jnp.zeros(h_o.shape, h_o.dtype)

    @pl.when(kind == BLK_FRONT)
    def _():
        emit(jnp.concatenate([jnp.zeros((FRONT, x_o.shape[1]), F32), meta_ref[...]], axis=0))

    pl.when(kind == BLK_PROMPT)(lambda: emit(xp_ref[...]))
    pl.when(kind == BLK_SAMPLE)(lambda: emit(xs_ref[...]))


def _pack_norm(x_prompt, x_sample, meta, g, R):
    D = x_prompt.shape[-1]
    kind, pb, sb = [], [], []
    for which, x in ((BLK_PROMPT, x_prompt), (BLK_SAMPLE, x_sample)):
        B, S = x.shape[0], x.shape[1]
        for b in range(B):
            for j in range(S // LANES + 1):
                kind.append(BLK_FRONT if j == 0 else which)
                src = b * (S // LANES) + max(j - 1, 0)
                pb.append(src if which == BLK_PROMPT else (pb[-1] if pb else 0))
                sb.append(src if which == BLK_SAMPLE else 0)
    n_blk = R // LANES
    kind += [BLK_TAIL] * (n_blk - len(kind))
    pb += [pb[-1]] * (n_blk - len(pb))
    sb += [sb[-1]] * (n_blk - len(sb))
    tables = [jnp.asarray(np.asarray(t, np.int32)) for t in (kind, pb, sb)]
    blk = pl.BlockSpec((LANES, D), lambda i, kind, pb, sb: (i, 0))
    grid_spec = pltpu.PrefetchScalarGridSpec(
        num_scalar_prefetch=3,
        grid=(n_blk,),
        in_specs=[pl.BlockSpec((LANES, D), lambda i, kind, pb, sb: (pb[i], 0)),
                  pl.BlockSpec((LANES, D), lambda i, kind, pb, sb: (sb[i], 0)),
                  pl.BlockSpec((N_META, D), lambda i, kind, pb, sb: (0, 0)),
                  pl.BlockSpec((1, D), lambda i, kind, pb, sb: (0, 0))],
        out_specs=[blk, blk],
    )
    return pl.pallas_call(
        _pack_norm_kernel,
        grid_spec=grid_spec,
        out_shape=[jax.ShapeDtypeStruct((R, D), F32), jax.ShapeDtypeStruct((R, D), BF16)],
        compiler_params=_cparams(("arbitrary",)),
        name="pack_rmsnorm",
    )(*tables, x_prompt.reshape(-1, D), x_sample.reshape(-1, D), meta.astype(F32), g.reshape(1, D))


def _mm_kernel(*refs, epilogue, n_extra, w_transposed):
    x_ref, w_ref = refs[0], refs[1]
    extras = refs[2:2 + n_extra]
    o_ref, w_sc = refs[2 + n_extra], refs[3 + n_extra]

    @pl.when(pl.program_id(1) == 0)
    def _():
        w = w_ref[...].T if w_transposed else w_ref[...]
        w_sc[...] = w.astype(w_sc.dtype)

    acc = _dot(x_ref[...], w_sc[...])
    if epilogue is not None:
        acc = epilogue(acc, *extras)
    o_ref[...] = acc.astype(o_ref.dtype)


def _matmul(x, w, *, tm, out_dtype, name, col0=0, n_out=None, tn=None, epilogue=None, extras=(), w_transposed=False):
    M, K = x.shape
    N = n_out or (w.shape[0] if w_transposed else w.shape[-1])
    tn = tn or _pick(N, (512, 384, 256, 128))
    assert col0 % tn == 0 and N % tn == 0
    c0 = col0 // tn
    if w_transposed:
        w_spec = pl.BlockSpec((tn, K), lambda j, i: (c0 + j, 0))
    elif w.ndim == 3:
        w_spec = pl.BlockSpec((None, K, tn), lambda j, i: (0, 0, c0 + j))
    else:
        w_spec = pl.BlockSpec((K, tn), lambda j, i: (0, c0 + j))
    in_specs = [pl.BlockSpec((tm, K), lambda j, i: (i, 0)), w_spec]
    in_specs += [pl.BlockSpec(shape, functools.partial(lambda j, i, fn: fn(i, j), fn=fn)) for _, shape, fn in extras]
    return pl.pallas_call(
        functools.partial(_mm_kernel, epilogue=epilogue, n_extra=len(extras), w_transposed=w_transposed),
        grid=(N // tn, M // tm),
        in_specs=in_specs,
        out_specs=pl.BlockSpec((tm, tn), lambda j, i: (i, j)),
        out_shape=jax.ShapeDtypeStruct((M, N), out_dtype),
        scratch_shapes=[pltpu.VMEM((K, tn), BF16)],
        compiler_params=_cparams(("parallel", "arbitrary")),
        name=name,
    )(x, w, *[a for a, _, _ in extras])


def _qk_epilogue(acc, g_ref, c_ref, s_ref, *, scale):
    g = g_ref[...]
    c = c_ref[...]
    s = s_ref[...]
    outs = []
    for h in range(acc.shape[1] // HEAD_DIM):
        y = acc[:, h * HEAD_DIM:(h + 1) * HEAD_DIM]
        y = y * lax.rsqrt(jnp.mean(y * y, axis=-1, keepdims=True) + NORM_EPS) * g
        y = y * c + pltpu.roll(y, HEAD_DIM // 2, 1) * s
        outs.append(y * scale if scale != 1.0 else y)
    return jnp.concatenate(outs, axis=1) if len(outs) > 1 else outs[0]


def _sigmoid_epilogue(acc):
    return _sigmoid(acc)


def _residual_epilogue(acc, x_ref):
    return x_ref[...] + acc


def _attn_kernel(q_ref, k_ref, vt_ref, bias_ref, _prev_ref, o_ref, m_sc, l_sc, acc_sc, *, tq, nk):
    j = pl.program_id(3)
    M = Q_PER_KV * tq

    @pl.when(j == 0)
    def _():
        m_sc[...] = jnp.full(m_sc.shape, NEG_BIG, F32)
        l_sc[...] = jnp.zeros(l_sc.shape, F32)
        acc_sc[...] = jnp.zeros(acc_sc.shape, F32)

    q = q_ref[...]
    q4 = jnp.concatenate([q[:, h * HEAD_DIM:(h + 1) * HEAD_DIM] for h in range(Q_PER_KV)], axis=0)

    def step(use_bias):
        s = _dot_nt(k_ref[...], q4)
        if use_bias:
            s = s + jnp.tile(bias_ref[...], (1, M // LANES))
        m_prev = m_sc[...]
        m_new = jnp.maximum(m_prev, jnp.max(s, axis=0, keepdims=True))
        alpha = jnp.exp2(m_prev - m_new)
        p = jnp.exp2(s - m_new)
        l_sc[...] = alpha * l_sc[...] + jnp.sum(p, axis=0, keepdims=True)
        acc_sc[...] = alpha * acc_sc[...] + _dot(vt_ref[...], p.astype(BF16))
        m_sc[...] = m_new

    if nk == 1:
        step(True)
    else:
        pl.when(j == 0)(functools.partial(step, True))
        pl.when(j > 0)(functools.partial(step, False))

    @pl.when(j == nk - 1)
    def _():
        o = (acc_sc[...] / l_sc[...]).T
        o_ref[...] = jnp.concatenate([o[h * tq:(h + 1) * tq] for h in range(Q_PER_KV)], axis=1).astype(o_ref.dtype)


def _attention(q, k, vt, bias, B, Np, row0, out_prev):
    tq = _pick(math.gcd(Np, row0), ROW_TILES)
    tk = Np if Np <= SINGLE_KEY_TILE_MAX else _pick(Np, KEY_TILES)
    nq, nk = Np // tq, Np // tk
    q0 = row0 // tq
    M = Q_PER_KV * tq
    return pl.pallas_call(
        functools.partial(_attn_kernel, tq=tq, nk=nk),
        grid=(B, N_KV_HEADS, nq, nk),
        in_specs=[
            pl.BlockSpec((tq, Q_PER_KV * HEAD_DIM), lambda b, g, i, j: (q0 + b * nq + i, g)),
            pl.BlockSpec((tk, HEAD_DIM), lambda b, g, i, j: (b * nk + j, g)),
            pl.BlockSpec((HEAD_DIM, tk), lambda b, g, i, j: (g, b * nk + j)),
            pl.BlockSpec((tk, LANES), lambda b, g, i, j: (j, 0)),
            pl.BlockSpec(memory_space=pl.ANY),
        ],
        out_specs=pl.BlockSpec((tq, Q_PER_KV * HEAD_DIM), lambda b, g, i, j: (q0 + b * nq + i, g)),
        out_shape=jax.ShapeDtypeStruct(out_prev.shape, out_prev.dtype),
        input_output_aliases={4: 0},
        scratch_shapes=[
            pltpu.VMEM((1, M), F32),
            pltpu.VMEM((1, M), F32),
            pltpu.VMEM((HEAD_DIM, M), F32),
        ],
        compiler_params=_cparams(("parallel", "parallel", "parallel", "arbitrary")),
        name="attention",
    )(q, k, vt, bias, out_prev)


V_SP, V_SN, V_DBASE, V_IBASE, V_KK, V_KA, V_RK, V_LNW, V_LNB = 0, 3, 6, 8, 10, 11, 12, 13, 14
VEC_ROWS = 16
HALO = 8


def _prep_kernel(r_ref, k_ref, v_ref, rp_ref, rn_ref, kp_ref, kn_ref, vp_ref, vn_ref, lora_ref,
                 dup0_ref, dup1_ref, iup0_ref, iup1_ref, gup_ref, vec_ref, ones_ref,
                 r_o, v_o, kk_o, ew0_o, kd0_o, b0_o, ew1_o, kd1_o, b1_o, bonus_o, g_o, *, tm, n_row_tiles):
    i = pl.program_id(0)
    has_prev = (i > 0).astype(F32)
    has_next = (i < n_row_tiles - 1).astype(F32)
    vec = vec_ref[...]
    row = lax.broadcasted_iota(jnp.int32, (tm, 1), 0)

    def shifted(x_ref, p_ref, n_ref, idx):
        x = x_ref[...]
        prev = jnp.where(row == 0, p_ref[HALO - 1:HALO, :] * has_prev, pltpu.roll(x, 1, 0))
        nxt = jnp.where(row == tm - 1, n_ref[0:1, :] * has_next, pltpu.roll(x, tm - 1, 0))
        return x + vec[V_SP + idx:V_SP + idx + 1] * (prev - x) + vec[V_SN + idx:V_SN + idx + 1] * (nxt - x)

    r = shifted(r_ref, rp_ref, rn_ref, 0)
    k = shifted(k_ref, kp_ref, kn_ref, 1)
    v = shifted(v_ref, vp_ref, vn_ref, 2)
    ones = ones_ref[...]

    lora = lora_ref[...]
    decay_h = jnp.tanh(lora[:, 0:DECAY_LORA])
    iclr_h = lora[:, DECAY_LORA:DECAY_LORA + ICLR_LORA]
    gate_h = _sigmoid(lora[:, DECAY_LORA + ICLR_LORA:LORA_PAD])

    kkr = k * vec[V_KK:V_KK + 1]
    kk = kkr / jnp.maximum(jnp.sqrt(_segsum(kkr * kkr, ones)), 1e-12)
    r_o[...] = r.astype(r_o.dtype)
    v_o[...] = v.astype(v_o.dtype)
    kk_o[...] = kk.astype(kk_o.dtype)
    bonus_o[...] = (_segsum(r * k * vec[V_RK:V_RK + 1], ones) * v).astype(bonus_o.dtype)
    g_o[...] = _dot(gate_h.astype(BF16), gup_ref[...]).astype(g_o.dtype)

    for d, (dup_ref, iup_ref, ew_o, kd_o, b_o) in enumerate(
            ((dup0_ref, iup0_ref, ew0_o, kd0_o, b0_o), (dup1_ref, iup1_ref, ew1_o, kd1_o, b1_o))):
        u = -(vec[V_DBASE + d:V_DBASE + d + 1] + _dot1(decay_h, dup_ref[...]))
        softplus = jnp.maximum(u, 0.0) + jnp.log(1.0 + jnp.exp(-jnp.abs(u)))
        ew_o[...] = jnp.exp(-softplus - 0.5)
        a = _sigmoid(vec[V_IBASE + d:V_IBASE + d + 1] + _dot1(iclr_h, iup_ref[...]))
        kd_o[...] = (k * (1.0 + (a - 1.0) * vec[V_KA:V_KA + 1])).astype(kd_o.dtype)
        b_o[...] = (kk * a).astype(b_o.dtype)


def _rwkv_prep(rkv, lora, dup, iup, gup, vec, ones_bd, tm, cn):
    R = rkv.shape[0]
    RW = rkv.shape[1] // 3
    J = RW // cn
    n_row_tiles = R // tm
    hb = tm // HALO
    last_hb = R // HALO - 1

    def main(c):
        return pl.BlockSpec((tm, cn), lambda i, j: (i, c * J + j))

    def prev(c):
        return pl.BlockSpec((HALO, cn), lambda i, j: (jnp.maximum(i * hb - 1, 0), c * J + j))

    def nxt(c):
        return pl.BlockSpec((HALO, cn), lambda i, j: (jnp.minimum((i + 1) * hb, last_hb), c * J + j))

    up = pl.BlockSpec((DECAY_LORA, cn), lambda i, j: (0, j))
    in_specs = [main(0), main(1), main(2), prev(0), nxt(0), prev(1), nxt(1), prev(2), nxt(2),
                pl.BlockSpec((tm, LORA_PAD), lambda i, j: (i, 0)),
                up, up, up, up,
                pl.BlockSpec((GATE_PAD, cn), lambda i, j: (0, j)),
                pl.BlockSpec((VEC_ROWS, cn), lambda i, j: (0, j)),
                pl.BlockSpec((cn, cn), lambda i, j: (0, 0))]
    out_spec = pl.BlockSpec((tm, cn), lambda i, j: (i, j))
    out_dtypes = [BF16, BF16, BF16, F32, BF16, BF16, F32, BF16, BF16, BF16, BF16]
    return pl.pallas_call(
        functools.partial(_prep_kernel, tm=tm, n_row_tiles=n_row_tiles),
        grid=(n_row_tiles, J),
        in_specs=in_specs,
        out_specs=[out_spec] * len(out_dtypes),
        out_shape=[jax.ShapeDtypeStruct((R, RW), dt) for dt in out_dtypes],
        compiler_params=_cparams(("parallel", "parallel")),
        name="rwkv_prep",
    )(rkv, rkv, rkv, rkv, rkv, rkv, rkv, rkv, rkv, lora, dup[0], dup[1], iup[0], iup[1], gup, vec, ones_bd)


def _wkv_kernel(first_ref, r_ref, v_ref, kk_ref, ew_ref, kd_ref, b_ref, tri_ref, y_ref, h_sc, *, reverse, npairs):
    c = pl.program_id(1)

    @pl.when(first_ref[c] == 1)
    def _():
        h_sc[...] = jnp.zeros(h_sc.shape, F32)

    L = CHUNK
    ew = ew_ref[...]
    cs = _dot_exact_lhs(tri_ref[...], ew)
    tot = cs[0:1] if reverse else cs[L - 1:L]
    e_neg = jnp.exp(-cs)
    e_prev = jnp.exp(ew - cs)
    e_pos = jnp.exp(cs)
    e_fin = jnp.exp(cs - tot)
    w_tot = jnp.exp(-tot)

    ri = lax.broadcasted_iota(jnp.int32, (PAIR, PAIR), 0)
    ci = lax.broadcasted_iota(jnp.int32, (PAIR, PAIR), 1)
    same = (ri // L) == (ci // L)
    t_i = ri % L
    s_i = ci % L
    if reverse:
        strict = same & (s_i > t_i)
        incl = same & (s_i >= t_i)
    else:
        strict = same & (s_i < t_i)
        incl = same & (s_i <= t_i)
    eye = ri == ci
    head0 = lax.broadcasted_iota(jnp.int32, (L, PAIR), 1) < RWKV_HEAD

    def stack(x):
        return jnp.concatenate([jnp.where(head0, x, 0.0), jnp.where(head0, 0.0, x)], axis=0)

    P2 = 2 * PAIR
    pairs = range(npairs)
    sls = [slice(p * PAIR, (p + 1) * PAIR) for p in pairs]

    r_st, a_st, v_st, bk2, kf_t, bf_t = [], [], [], [], [], []
    for sl in sls:
        kd = kd_ref[:, sl].astype(F32)
        b = b_ref[:, sl].astype(F32)
        bt = (b * e_pos[:, sl]).astype(BF16)
        kt = (kd * e_pos[:, sl]).astype(BF16)
        r_st.append(stack(r_ref[:, sl].astype(F32) * e_neg[:, sl]))
        a_st.append(stack(-kk_ref[:, sl].astype(F32) * e_prev[:, sl]))
        v_st.append(stack(v_ref[:, sl].astype(F32)).astype(BF16))
        bk2.append(jnp.concatenate([bt, bt, kt, kt], axis=0))
        kf_t.append(stack(kd * e_fin[:, sl]).T)
        bf_t.append(stack(b * e_fin[:, sl]).T)

    a_ab, a_ak, wlhs = [], [], []
    for p in pairs:
        sc = _dot_nt(jnp.concatenate([a_st[p], r_st[p]], axis=0).astype(BF16), bk2[p])
        a_ab.append(jnp.where(strict, sc[0:PAIR, 0:PAIR], 0.0))
        a_ak.append(jnp.where(strict, sc[0:PAIR, PAIR:P2], 0.0).astype(BF16))
        m_rb = jnp.where(incl, sc[PAIR:P2, 0:PAIR], 0.0)
        m_rk = jnp.where(incl, sc[PAIR:P2, PAIR:P2], 0.0)
        wlhs.append(jnp.concatenate([jnp.concatenate([m_rk, m_rb], axis=1),
                                     jnp.concatenate([kf_t[p], bf_t[p]], axis=1)], axis=0).astype(BF16))

    x = [jnp.concatenate([a_st[p], _dot(a_ak[p], v_st[p])], axis=1) for p in pairs]
    npow = a_ab
    n_sq = int(math.log2(L)) - 1
    for it in range(n_sq + 1):
        lhs = [npow[p].astype(BF16) for p in pairs]
        if it < n_sq:
            prod = [_dot(lhs[p], jnp.concatenate([lhs[p], x[p].astype(BF16)], axis=1)) for p in pairs]
            npow = [prod[p][:, 0:PAIR] for p in pairs]
            x = [x[p] + prod[p][:, PAIR:] for p in pairs]
        else:
            x = [x[p] + _dot(lhs[p], x[p].astype(BF16)) for p in pairs]

    zero = jnp.zeros((PAIR, PAIR), BF16)
    for p in pairs:
        sl = sls[p]
        rhs = jnp.concatenate([jnp.concatenate([zero, v_st[p]], axis=1), x[p].astype(BF16)], axis=0)
        w = _dot(wlhs[p], rhs)
        g_st = r_st[p] + w[0:PAIR, 0:PAIR]
        y0_st = w[0:PAIR, PAIR:P2]
        phi = jnp.where(eye, w_tot[:, sl], 0.0) + w[PAIR:P2, 0:PAIR]
        psi = w[PAIR:P2, PAIR:P2]
        out = _dot1(jnp.concatenate([g_st, phi], axis=0), h_sc[p])
        y_st = out[0:PAIR] + y0_st
        h_sc[p] = out[PAIR:P2] + psi
        y_ref[:, sl] = y_st[0:L] + y_st[L:2 * L]


def _wkv(first, r, v, kk, ew, kd, b, tri, *, reverse, npairs):
    R, RW = r.shape
    NC = R // CHUNK
    width = npairs * PAIR
    if reverse:
        blk = pl.BlockSpec((CHUNK, width), lambda g, c, f: (NC - 1 - c, g))
    else:
        blk = pl.BlockSpec((CHUNK, width), lambda g, c, f: (c, g))
    grid_spec = pltpu.PrefetchScalarGridSpec(
        num_scalar_prefetch=1,
        grid=(RW // width, NC),
        in_specs=[blk] * 6 + [pl.BlockSpec((CHUNK, CHUNK), lambda g, c, f: (0, 0))],
        out_specs=blk,
        scratch_shapes=[pltpu.VMEM((npairs, PAIR, PAIR), F32)],
    )
    return pl.pallas_call(
        functools.partial(_wkv_kernel, reverse=reverse, npairs=npairs),
        grid_spec=grid_spec,
        out_shape=jax.ShapeDtypeStruct((R, RW), F32),
        compiler_params=_cparams(("parallel", "arbitrary")),
        name="wkv_rev" if reverse else "wkv_fwd",
    )(first, r, v, kk, ew, kd, b, tri)


def _post_kernel(yf_ref, yb_ref, bonus_ref, g_ref, vec_ref, ones_ref, o_ref):
    ones = ones_ref[...]
    vec = vec_ref[...]
    y = yf_ref[...] + yb_ref[...]
    inv_n = 1.0 / RWKV_HEAD
    mu = _segsum(y, ones) * inv_n
    d = y - mu
    var = _segsum(d * d, ones) * inv_n
    yn = d * lax.rsqrt(var + GN_EPS) * vec[V_LNW:V_LNW + 1] + vec[V_LNB:V_LNB + 1]
    o_ref[...] = ((yn + bonus_ref[...]) * g_ref[...]).astype(o_ref.dtype)


def _rwkv_post(yf, yb, bonus, g, vec, ones_bd, tm, cn):
    R, RW = yf.shape
    blk = pl.BlockSpec((tm, cn), lambda i, j: (i, j))
    return pl.pallas_call(
        _post_kernel,
        grid=(R // tm, RW // cn),
        in_specs=[blk, blk, blk, blk, pl.BlockSpec((VEC_ROWS, cn), lambda i, j: (0, j)),
                  pl.BlockSpec((cn, cn), lambda i, j: (0, 0))],
        out_specs=blk,
        out_shape=jax.ShapeDtypeStruct((R, RW), BF16),
        compiler_params=_cparams(("parallel", "parallel")),
        name="rwkv_post",
    )(yf, yb, bonus, g, vec, ones_bd)


def _merge_kernel(a_ref, r_ref, wa_ref, wr_ref, ga_ref, gr_ref, o_ref, wa_sc, wr_sc):
    @pl.when(pl.program_id(1) == 0)
    def _():
        wa_sc[...] = wa_ref[...].astype(wa_sc.dtype)
        wr_sc[...] = wr_ref[...].astype(wr_sc.dtype)

    ya = _dot(a_ref[...], wa_sc[...])
    yr = _dot(r_ref[...], wr_sc[...])
    o_ref[...] = (ga_ref[...].astype(F32) * ya + gr_ref[...].astype(F32) * yr).astype(o_ref.dtype)


def _merge(attn, rwkv, wa, wr, gates, tm, tn):
    R = attn.shape[0]
    D = wa.shape[1]
    J = D // tn
    return pl.pallas_call(
        _merge_kernel,
        grid=(J, R // tm),
        in_specs=[
            pl.BlockSpec((tm, attn.shape[1]), lambda j, i: (i, 0)),
            pl.BlockSpec((tm, rwkv.shape[1]), lambda j, i: (i, 0)),
            pl.BlockSpec((wa.shape[0], tn), lambda j, i: (0, j)),
            pl.BlockSpec((wr.shape[0], tn), lambda j, i: (0, j)),
            pl.BlockSpec((tm, tn), lambda j, i: (i, j)),
            pl.BlockSpec((tm, tn), lambda j, i: (i, J + j)),
        ],
        out_specs=pl.BlockSpec((tm, tn), lambda j, i: (i, j)),
        out_shape=jax.ShapeDtypeStruct((R, D), BF16),
        scratch_shapes=[pltpu.VMEM((wa.shape[0], tn), BF16), pltpu.VMEM((wr.shape[0], tn), BF16)],
        compiler_params=_cparams(("parallel", "arbitrary")),
        name="merge",
    )(attn, rwkv, wa, wr, gates, gates)


def _router_kernel(x_ref, g_ref, w_ref, h_ref, aff_ref):
    x = x_ref[...]
    ms = jnp.mean(x * x, axis=-1, keepdims=True)
    h = x * lax.rsqrt(ms + NORM_EPS) * g_ref[...]
    h_ref[...] = h.astype(h_ref.dtype)
    logits = _dot3(h, w_ref[...])
    lane = lax.broadcasted_iota(jnp.int32, logits.shape, 1)
    logits = jnp.where(lane < N_EXPERTS, logits, NEG_BIG)
    e = jnp.exp(logits - jnp.max(logits, axis=-1, keepdims=True))
    aff_ref[...] = e / jnp.sum(e, axis=-1, keepdims=True)


def _router(x1, g, w_router_pad, tm):
    R, D = x1.shape
    return pl.pallas_call(
        _router_kernel,
        grid=(R // tm,),
        in_specs=[pl.BlockSpec((tm, D), lambda i: (i, 0)), pl.BlockSpec((1, D), lambda i: (0, 0)),
                  pl.BlockSpec((D, LANES), lambda i: (0, 0))],
        out_specs=[pl.BlockSpec((tm, D), lambda i: (i, 0)), pl.BlockSpec((tm, LANES), lambda i: (i, 0))],
        out_shape=[jax.ShapeDtypeStruct((R, D), BF16), jax.ShapeDtypeStruct((R, LANES), F32)],
        compiler_params=_cparams(("parallel",)),
        name="ffn_norm_router",
    )(x1, g.reshape(1, D), w_router_pad)


def _ffn_up_kernel(x_ref, wg_ref, wu_ref, o_ref):
    x = x_ref[...]
    hg = _dot(x, wg_ref[...].astype(BF16))
    hu = _dot(x, wu_ref[...].astype(BF16))
    o_ref[...] = (hg * _sigmoid(hg) * hu).astype(o_ref.dtype)


def _ffn_down_kernel(h_ref, wd_ref, gate_ref, o_ref):
    o_ref[...] = (_dot(h_ref[...], wd_ref[...].astype(BF16)) * gate_ref[...]).astype(o_ref.dtype)


def _expert_ffn(xs, gate, w_gate, w_up, w_down, tm, tf):
    E, D, F = w_gate.shape
    nt = xs.shape[0] // (E * tm)
    wspec = pl.BlockSpec((None, D, tf), lambda e, i, f: (e, 0, f))
    hid = pl.pallas_call(
        _ffn_up_kernel,
        grid=(E, nt, F // tf),
        in_specs=[pl.BlockSpec((tm, D), lambda e, i, f: (e * nt + i, 0)), wspec, wspec],
        out_specs=pl.BlockSpec((tm, tf), lambda e, i, f: (e * nt + i, f)),
        out_shape=jax.ShapeDtypeStruct((xs.shape[0], F), BF16),
        compiler_params=_cparams(("parallel", "parallel", "parallel")),
        name="ffn_up",
    )(xs, w_gate, w_up)
    td = _pick(D, (2 * tf, tf))
    return pl.pallas_call(
        _ffn_down_kernel,
        grid=(E, nt, D // td),
        in_specs=[pl.BlockSpec((tm, F), lambda e, i, f: (e * nt + i, 0)),
                  pl.BlockSpec((None, F, td), lambda e, i, f: (e, 0, f)),
                  pl.BlockSpec((tm, 1), lambda e, i, f: (e * nt + i, 0))],
        out_specs=pl.BlockSpec((tm, td), lambda e, i, f: (e * nt + i, f)),
        out_shape=jax.ShapeDtypeStruct((xs.shape[0], D), BF16),
        compiler_params=_cparams(("parallel", "parallel", "parallel")),
        name="ffn_down",
    )(hid, w_down, gate)


COMBINE_CHUNK = 64
SLOT_ALIGN = 16
COMBINE_SPAN = COMBINE_CHUNK - SLOT_ALIGN


def _combine_kernel(lo_ref, hi_ref, npass_ref, out_hbm, tok_hbm, y_ref, buf, tbuf, sem, *, ns, tmc):
    i = pl.program_id(0)
    E, CH = N_EXPERTS, COMBINE_CHUNK
    base = i * tmc
    lane = lax.broadcasted_iota(jnp.int32, (CH, tmc), 1)
    srow = lax.broadcasted_iota(jnp.int32, (CH, 1), 0)
    y_ref[...] = jnp.zeros(y_ref.shape, F32)

    def one_pass(p, carry):
        copies, bounds = [], []
        for e in range(E):
            a = lo_ref[i * E + e] + p * COMBINE_SPAN
            b = jnp.minimum(a + COMBINE_SPAN, hi_ref[i * E + e])
            start = pl.multiple_of(jnp.clip((a // SLOT_ALIGN) * SLOT_ALIGN, 0, ns - CH), SLOT_ALIGN)
            rows = pl.ds(e * CH, CH)
            pair = (pltpu.make_async_copy(out_hbm.at[pl.ds(start, CH), :], buf.at[rows, :], sem.at[0, e]),
                    pltpu.make_async_copy(tok_hbm.at[pl.ds(start, CH), :], tbuf.at[rows, :], sem.at[1, e]))
            pair[0].start()
            pair[1].start()
            copies.append(pair)
            bounds.append((start, a, b))
        blocks = []
        for e in range(E):
            copies[e][0].wait()
            copies[e][1].wait()
            start, a, b = bounds[e]
            slot = start + srow
            hit = (tbuf[pl.ds(e * CH, CH), :] - base == lane) & (slot >= a) & (slot < b)
            blocks.append(jnp.where(hit, 1.0, 0.0))
        onehot = jnp.concatenate(blocks, axis=0).T.astype(BF16)
        y_ref[...] += _dot(onehot, buf[...])
        return carry

    lax.fori_loop(0, npass_ref[i], one_pass, 0)


def _combine(out, tok, lo, hi, npass, R, tmc):
    ns, D = out.shape
    grid_spec = pltpu.PrefetchScalarGridSpec(
        num_scalar_prefetch=3,
        grid=(R // tmc,),
        in_specs=[pl.BlockSpec(memory_space=pl.ANY), pl.BlockSpec(memory_space=pl.ANY)],
        out_specs=pl.BlockSpec((tmc, D), lambda i, lo, hi, n: (i, 0)),
        scratch_shapes=[pltpu.VMEM((N_EXPERTS * COMBINE_CHUNK, D), BF16),
                        pltpu.VMEM((N_EXPERTS * COMBINE_CHUNK, 1), jnp.int32),
                        pltpu.SemaphoreType.DMA((2, N_EXPERTS))],
    )
    return pl.pallas_call(
        functools.partial(_combine_kernel, ns=ns, tmc=tmc),
        grid_spec=grid_spec,
        out_shape=jax.ShapeDtypeStruct((R, D), F32),
        compiler_params=_cparams(("arbitrary",)),
        name="expert_combine",
    )(lo, hi, npass, out, tok)


def _final_kernel(x_ref, y_ref, g_ref, o_ref):
    x = x_ref[...] + y_ref[...]
    ms = jnp.mean(x * x, axis=-1, keepdims=True)
    o_ref[...] = x * lax.rsqrt(ms + NORM_EPS) * g_ref[...]


def _final(x1, y, g, B, Np, off):
    D = x1.shape[1]
    nb = Np // LANES
    base = off // LANES
    blk = pl.BlockSpec((LANES, D), lambda b, j: (base + b * nb + 1 + j, 0))
    return pl.pallas_call(
        _final_kernel,
        grid=(B, nb - 1),
        in_specs=[blk, blk, pl.BlockSpec((1, D), lambda b, j: (0, 0))],
        out_specs=pl.BlockSpec((None, LANES, D), lambda b, j: (b, j, 0)),
        out_shape=jax.ShapeDtypeStruct((B, Np - LANES, D), F32),
        compiler_params=_cparams(("parallel", "parallel")),
        name="final_norm",
    )(x1, y, g.reshape(1, D))


def _rope_tables(S, B):
    rows = S // GRID_W
    row_ids = jnp.repeat(jnp.arange(rows, dtype=F32), GRID_W)
    col_ids = jnp.tile(jnp.arange(GRID_W, dtype=F32), rows)
    half = HEAD_DIM // 2
    inv_freq = 1.0 / (ROPE_THETA ** (jnp.arange(0, half, 2, dtype=F32) / half))
    ang = jnp.concatenate([row_ids[:, None] * inv_freq, col_ids[:, None] * inv_freq], axis=-1)
    ang = jnp.concatenate([jnp.zeros((LANES, half), F32), ang], axis=0)
    c, s = jnp.cos(ang), jnp.sin(ang)
    return (jnp.tile(jnp.concatenate([c, c], axis=-1), (B, 1)),
            jnp.tile(jnp.concatenate([-s, s], axis=-1), (B, 1)))


def _key_bias(Np):
    col = np.where(np.arange(Np) < FRONT, NEG_BIG, 0.0).astype(np.float32)
    return jnp.asarray(np.repeat(col[:, None], LANES, axis=1))


def _seq_flags(groups):
    fwd = []
    for B, Np in groups:
        nc = Np // CHUNK
        for _ in range(B):
            fwd += [1] + [0] * (nc - 1)
    fwd = np.asarray(fwd, np.int32)
    last = np.roll(fwd, -1)
    return jnp.asarray(fwd), jnp.asarray(last[::-1].copy())


def kernel(x_prompt, x_sample, meta_tokens, norm_mix, w_in, q_norm, k_norm, shift_prev, shift_next, decay_up, decay_base, iclr_up, iclr_base, gate_up, k_k, k_a, r_k, ln_x_w, ln_x_b, w_branch_attn, w_branch_rwkv, w_out, norm_ffn, w_router, w_gate, w_up, w_down, norm_final):
    assert norm_mix.shape[0] == 1, "one layer"
    D = x_prompt.shape[-1]
    RW = D // 2
    groups = [(x.shape[0], x.shape[1] + LANES) for x in (x_prompt, x_sample)]
    seqs = [x.shape[1] for x in (x_prompt, x_sample)]
    assert all(s % LANES == 0 for s in seqs)
    offs = [0, groups[0][0] * groups[0][1]]
    r_used = offs[1] + groups[1][0] * groups[1][1]
    tm = MATMUL_ROW_TILE if r_used >= 8 * MATMUL_ROW_TILE else LANES
    tme = tm // 3 if tm % 3 == 0 else tm
    tmr = tm // 2 if tm % 256 == 0 else tm
    R = -(-r_used // tm) * tm
    tail = R - r_used

    x, h = _pack_norm(x_prompt, x_sample, meta_tokens, norm_mix[0], R)
    tabs = [_rope_tables(S, B) for S, (B, _) in zip(seqs, groups)]
    cos_t = jnp.concatenate([t[0] for t in tabs] + [jnp.zeros((tail, HEAD_DIM), F32)], axis=0)
    sin_t = jnp.concatenate([t[1] for t in tabs] + [jnp.zeros((tail, HEAD_DIM), F32)], axis=0)

    w_in_t = jnp.transpose(w_in[0])
    n_in = w_in_t.shape[0]

    def in_proj(col0, n_out, **kw):
        tn = _pick(n_out, (512, 384, 256, 128))
        if col0 % tn == 0 and col0 + n_out <= n_in:
            return _matmul(h, w_in_t, col0=col0, n_out=n_out, tn=tn, tm=tm, w_transposed=True, **kw)
        w_rows = w_in_t[col0:min(col0 + n_out, n_in)]
        w_rows = jnp.pad(w_rows, ((0, n_out - w_rows.shape[0]), (0, 0)))
        return _matmul(h, w_rows, tn=tn, tm=tm, w_transposed=True, **kw)

    rope_specs = lambda g: [(g.reshape(1, HEAD_DIM), (1, HEAD_DIM), lambda i, j: (0, 0)),
                            (cos_t, (tm, HEAD_DIM), lambda i, j: (i, 0)),
                            (sin_t, (tm, HEAD_DIM), lambda i, j: (i, 0))]
    q_scale = HEAD_DIM ** -0.5 * math.log2(math.e)
    c_k = ATTN_WIDTH
    c_v = c_k + KV_WIDTH
    c_rkv = c_v + KV_WIDTH
    c_lora = c_rkv + 3 * RW
    c_merge = c_lora + DECAY_LORA + ICLR_LORA + GATE_LORA
    q = in_proj(0, ATTN_WIDTH, out_dtype=BF16, name="proj_q",
                epilogue=functools.partial(_qk_epilogue, scale=q_scale), extras=rope_specs(q_norm[0]))
    k = in_proj(c_k, KV_WIDTH, out_dtype=BF16, name="proj_k",
                epilogue=functools.partial(_qk_epilogue, scale=1.0), extras=rope_specs(k_norm[0]))
    v = in_proj(c_v, KV_WIDTH, out_dtype=BF16, name="proj_v")
    rkv = in_proj(c_rkv, 3 * RW, out_dtype=F32, name="proj_rkv")
    lora = in_proj(c_lora, LORA_PAD, out_dtype=F32, name="proj_lora")
    gates = in_proj(c_merge, 2 * D, out_dtype=BF16, name="proj_merge_gates", epilogue=_sigmoid_epilogue)

    attn = jnp.zeros((R, ATTN_WIDTH), BF16)
    for (B, Np), off in zip(groups, offs):
        sl = slice(off, off + B * Np)
        attn = _attention(q, k[sl], v[sl].T, _key_bias(Np), B, Np, off, attn)

    cn = _pick(RW, (256, 128))
    vec = jnp.concatenate([shift_prev[0], shift_next[0], decay_base[0], iclr_base[0], k_k, k_a, r_k,
                           ln_x_w, ln_x_b, jnp.zeros((VEC_ROWS - 15, RW), F32)], axis=0).astype(F32)
    hid_idx = np.arange(cn) // RWKV_HEAD
    ones_bd = jnp.asarray((hid_idx[:, None] == hid_idx[None, :]).astype(np.float32)).astype(BF16)
    gup = jnp.pad(gate_up[0], ((0, GATE_PAD - GATE_LORA), (0, 0))).astype(BF16)
    r_s, v_s, kk, ew0, kd0, b0, ew1, kd1, b1, bonus, g_rwkv = _rwkv_prep(
        rkv, lora, decay_up[0], iclr_up[0], gup, vec, ones_bd, tmr, cn)
    first_fwd, first_rev = _seq_flags(groups + ([(1, tail)] if tail else []))
    t_idx = np.arange(CHUNK)
    tri_f = jnp.asarray((t_idx[None, :] <= t_idx[:, None]).astype(np.float32)).astype(BF16)
    tri_r = jnp.asarray((t_idx[None, :] >= t_idx[:, None]).astype(np.float32)).astype(BF16)
    npairs = _pick(RW // PAIR, (16, 8, 4, 2, 1))
    y_f = _wkv(first_fwd, r_s, v_s, kk, ew0, kd0, b0, tri_f, reverse=False, npairs=npairs)
    y_b = _wkv(first_rev, r_s, v_s, kk, ew1, kd1, b1, tri_r, reverse=True, npairs=npairs)
    rwkv = _rwkv_post(y_f, y_b, bonus, g_rwkv, vec, ones_bd, tmr, cn)

    tn_d = _pick(D, (512, 384, 256, 128))
    merged = _merge(attn, rwkv, w_branch_attn[0], w_branch_rwkv[0], gates, tm, tn_d)
    x1 = _matmul(merged, w_out, tm=tm, tn=tn_d, out_dtype=F32, name="out_proj",
                 epilogue=_residual_epilogue, extras=[(x, (tm, tn_d), lambda i, j: (i, j))])

    w_router_pad = jnp.pad(w_router[0], ((0, 0), (0, LANES - N_EXPERTS)))
    h2, aff = _router(x1, norm_ffn[0], w_router_pad, tme)
    aff = aff[:, :N_EXPERTS]
    idx_parts, gate_parts = [], []
    for (B, Np), off in zip(groups, offs):
        valid = jnp.asarray((np.arange(B * Np) % Np) >= FRONT)
        a_g = jnp.where(valid[:, None], aff[off:off + B * Np], -1.0)
        cap = CAPACITY_FACTOR * (B * (Np - FRONT)) // N_EXPERTS
        gate_g, idx_g = lax.top_k(a_g.T, cap)
        idx_g, gate_g = lax.sort((idx_g, gate_g), dimension=1, num_keys=1)
        idx_parts.append(idx_g + off)
        gate_parts.append(gate_g)
    idx = jnp.concatenate(idx_parts, axis=1)
    gate = jnp.concatenate(gate_parts, axis=1)
    c_tot = idx.shape[1]
    nt = -(-c_tot // EXPERT_ROW_TILE_MAX)
    tmx = -(-(-(-c_tot // nt)) // SLOT_ALIGN) * SLOT_ALIGN
    ct = nt * tmx
    tok = jnp.pad(idx, ((0, 0), (0, ct - c_tot)), constant_values=R)
    gate = jnp.pad(gate, ((0, 0), (0, ct - c_tot))).reshape(-1, 1)
    xs = h2.at[jnp.where(tok < R, tok, 0).reshape(-1)].get(mode="promise_in_bounds")
    tf = _pick(D, (256, 128))
    out = _expert_ffn(xs, gate, w_gate[0], w_up[0], w_down[0], tmx, tf)

    tmc = _pick(R, (256, LANES))
    edges = jnp.arange(R // tmc + 1, dtype=jnp.int32) * tmc
    pos = jnp.sum((tok[:, None, :] < edges[None, :, None]).astype(jnp.int32), axis=-1)
    pos = pos + (jnp.arange(N_EXPERTS, dtype=jnp.int32) * ct)[:, None]
    lo, hi = pos[:, :-1].T, pos[:, 1:].T
    npass = jnp.max(-(-(hi - lo) // COMBINE_SPAN), axis=1).astype(jnp.int32)
    y = _combine(out, tok.reshape(-1, 1).astype(jnp.int32), lo.reshape(-1), hi.reshape(-1), npass, R, tmc)

    return tuple(_final(x1, y, norm_final, B, Np, off) for (B, Np), off in zip(groups, offs))
```

```python
import functools
import math

import numpy as np
import jax
import jax.numpy as jnp
from jax import lax
from jax.experimental import pallas as pl
from jax.experimental.pallas import tpu as pltpu

F32 = jnp.float32
BF16 = jnp.bfloat16

N_META = 16
GRID_W = 64
HEAD_DIM = 128
N_Q_HEADS = 16
N_KV_HEADS = 4
Q_PER_KV = N_Q_HEADS // N_KV_HEADS
ATTN_WIDTH = N_Q_HEADS * HEAD_DIM
KV_WIDTH = N_KV_HEADS * HEAD_DIM
ROPE_THETA = 10000.0
RWKV_HEAD = 64
DECAY_LORA = 128
ICLR_LORA = 128
GATE_LORA = 480
N_EXPERTS = 16
CAPACITY_FACTOR = 2
NORM_EPS = 1e-6
GN_EPS = 64e-5

LANES = 128
FRONT = LANES - N_META
CHUNK = 64
PAIR = 2 * RWKV_HEAD
LORA_PAD = 768
GATE_PAD = LORA_PAD - DECAY_LORA - ICLR_LORA
VMEM_LIMIT = 56 * 1024 * 1024
NEG_BIG = -1e30
MATMUL_ROW_TILE = 768
EXPERT_ROW_TILE_MAX = 1040
ROW_TILES = (640, 512, 384, 256, 128)
KEY_TILES = (1024, 896, 768) + ROW_TILES
SINGLE_KEY_TILE_MAX = 2304


def _cparams(sem):
    return pltpu.CompilerParams(dimension_semantics=sem, vmem_limit_bytes=VMEM_LIMIT)


def _pick(n, cands):
    for c in cands:
        if n % c == 0:
            return c
    raise ValueError(f"no tile for {n} in {cands}")


def _split2(x):
    hi = x.astype(BF16)
    lo = (x - hi.astype(F32)).astype(BF16)
    return hi, lo


def _split3(x):
    hi = x.astype(BF16)
    r1 = x - hi.astype(F32)
    mid = r1.astype(BF16)
    lo = (r1 - mid.astype(F32)).astype(BF16)
    return hi, mid, lo


def _dot(a, b):
    return jnp.dot(a, b, preferred_element_type=F32)


def _dot_nt(a, b):
    return lax.dot_general(a, b, (((1,), (1,)), ((), ())), preferred_element_type=F32)


def _dot1(a, b):
    return _dot(a.astype(BF16), b.astype(BF16))


def _dot3(a, b):
    ah, al = _split2(a)
    bh, bl = _split2(b)
    return _dot(ah, bh) + (_dot(ah, bl) + _dot(al, bh))


def _dot_exact_lhs(a_bf16, b):
    hi, mid, lo = _split3(b)
    return _dot(a_bf16, hi) + (_dot(a_bf16, mid) + _dot(a_bf16, lo))


def _segsum(x, ones_bd):
    hi, lo = _split2(x)
    return _dot(hi, ones_bd) + _dot(lo, ones_bd)


def _sigmoid(x):
    return 1.0 / (1.0 + jnp.exp(-x))


BLK_TAIL, BLK_FRONT, BLK_PROMPT, BLK_SAMPLE = 0, 1, 2, 3


def _pack_norm_kernel(kind_ref, pb_ref, sb_ref, xp_ref, xs_ref, meta_ref, g_ref, x_o, h_o):
    kind = kind_ref[pl.program_id(0)]

    def emit(x):
        x_o[...] = x
        ms = jnp.mean(x * x, axis=-1, keepdims=True)
        h_o[...] = (x * lax.rsqrt(ms + NORM_EPS) * g_ref[...]).astype(h_o.dtype)

    @pl.when(kind == BLK_TAIL)
    def _():
        x_o[...] = jnp.zeros(x_o.shape, x_o.dtype)
        h_o[...] = jnp.zeros(h_o.shape, h_o.dtype)

    @pl.when(kind == BLK_FRONT)
    def _():
        emit(jnp.concatenate([jnp.zeros((FRONT, x_o.shape[1]), F32), meta_ref[...]], axis=0))

    pl.when(kind == BLK_PROMPT)(lambda: emit(xp_ref[...]))
    pl.when(kind == BLK_SAMPLE)(lambda: emit(xs_ref[...]))


def _pack_norm(x_prompt, x_sample, meta, g, R):
    D = x_prompt.shape[-1]
    kind, pb, sb = [], [], []
    for which, x in ((BLK_PROMPT, x_prompt), (BLK_SAMPLE, x_sample)):
        B, S = x.shape[0], x.shape[1]
        for b in range(B):
            for j in range(S // LANES + 1):
                kind.append(BLK_FRONT if j == 0 else which)
                src = b * (S // LANES) + max(j - 1, 0)
                pb.append(src if which == BLK_PROMPT else (pb[-1] if pb else 0))
                sb.append(src if which == BLK_SAMPLE else 0)
    n_blk = R // LANES
    kind += [BLK_TAIL] * (n_blk - len(kind))
    pb += [pb[-1]] * (n_blk - len(pb))
    sb += [sb[-1]] * (n_blk - len(sb))
    tables = [jnp.asarray(np.asarray(t, np.int32)) for t in (kind, pb, sb)]
    blk = pl.BlockSpec((LANES, D), lambda i, kind, pb, sb: (i, 0))
    grid_spec = pltpu.PrefetchScalarGridSpec(
        num_scalar_prefetch=3,
        grid=(n_blk,),
        in_specs=[pl.BlockSpec((LANES, D), lambda i, kind, pb, sb: (pb[i], 0)),
                  pl.BlockSpec((LANES, D), lambda i, kind, pb, sb: (sb[i], 0)),
                  pl.BlockSpec((N_META, D), lambda i, kind, pb, sb: (0, 0)),
                  pl.BlockSpec((1, D), lambda i, kind, pb, sb: (0, 0))],
        out_specs=[blk, blk],
    )
    return pl.pallas_call(
        _pack_norm_kernel,
        grid_spec=grid_spec,
        out_shape=[jax.ShapeDtypeStruct((R, D), F32), jax.ShapeDtypeStruct((R, D), BF16)],
        compiler_params=_cparams(("arbitrary",)),
        name="pack_rmsnorm",
    )(*tables, x_prompt.reshape(-1, D), x_sample.reshape(-1, D), meta.astype(F32), g.reshape(1, D))


def _mm_kernel(*refs, epilogue, n_extra, w_transposed):
    x_ref, w_ref = refs[0], refs[1]
    extras = refs[2:2 + n_extra]
    o_ref, w_sc = refs[2 + n_extra], refs[3 + n_extra]

    @pl.when(pl.program_id(1) == 0)
    def _():
        w = w_ref[...].T if w_transposed else w_ref[...]
        w_sc[...] = w.astype(w_sc.dtype)

    acc = _dot(x_ref[...], w_sc[...])
    if epilogue is not None:
        acc = epilogue(acc, *extras)
    o_ref[...] = acc.astype(o_ref.dtype)


def _matmul(x, w, *, tm, out_dtype, name, col0=0, n_out=None, tn=None, epilogue=None, extras=(), w_transposed=False):
    M, K = x.shape
    N = n_out or (w.shape[0] if w_transposed else w.shape[-1])
    tn = tn or _pick(N, (512, 384, 256, 128))
    assert col0 % tn == 0 and N % tn == 0
    c0 = col0 // tn
    if w_transposed:
        w_spec = pl.BlockSpec((tn, K), lambda j, i: (c0 + j, 0))
    elif w.ndim == 3:
        w_spec = pl.BlockSpec((None, K, tn), lambda j, i: (0, 0, c0 + j))
    else:
        w_spec = pl.BlockSpec((K, tn), lambda j, i: (0, c0 + j))
    in_specs = [pl.BlockSpec((tm, K), lambda j, i: (i, 0)), w_spec]
    in_specs += [pl.BlockSpec(shape, functools.partial(lambda j, i, fn: fn(i, j), fn=fn)) for _, shape, fn in extras]
    return pl.pallas_call(
        functools.partial(_mm_kernel, epilogue=epilogue, n_extra=len(extras), w_transposed=w_transposed),
        grid=(N // tn, M // tm),
        in_specs=in_specs,
        out_specs=pl.BlockSpec((tm, tn), lambda j, i: (i, j)),
        out_shape=jax.ShapeDtypeStruct((M, N), out_dtype),
        scratch_shapes=[pltpu.VMEM((K, tn), BF16)],
        compiler_params=_cparams(("parallel", "arbitrary")),
        name=name,
    )(x, w, *[a for a, _, _ in extras])


def _qk_epilogue(acc, g_ref, c_ref, s_ref, *, scale):
    g = g_ref[...]
    c = c_ref[...]
    s = s_ref[...]
    outs = []
    for h in range(acc.shape[1] // HEAD_DIM):
        y = acc[:, h * HEAD_DIM:(h + 1) * HEAD_DIM]
        y = y * lax.rsqrt(jnp.mean(y * y, axis=-1, keepdims=True) + NORM_EPS) * g
        y = y * c + pltpu.roll(y, HEAD_DIM // 2, 1) * s
        outs.append(y * scale if scale != 1.0 else y)
    return jnp.concatenate(outs, axis=1) if len(outs) > 1 else outs[0]


def _sigmoid_epilogue(acc):
    return _sigmoid(acc)


def _residual_epilogue(acc, x_ref):
    return x_ref[...] + acc


def _attn_kernel(q_ref, k_ref, vt_ref, bias_ref, _prev_ref, o_ref, m_sc, l_sc, acc_sc, *, tq, nk):
    j = pl.program_id(2)
    M = Q_PER_KV * tq
    QW = Q_PER_KV * HEAD_DIM

    @pl.when(j == 0)
    def _():
        m_sc[...] = jnp.full(m_sc.shape, NEG_BIG, F32)
        l_sc[...] = jnp.zeros(l_sc.shape, F32)
        acc_sc[...] = jnp.zeros(acc_sc.shape, F32)

    def step(use_bias):
        for g in range(N_KV_HEADS):
            q = q_ref[:, g * QW:(g + 1) * QW]
            q4 = jnp.concatenate([q[:, h * HEAD_DIM:(h + 1) * HEAD_DIM] for h in range(Q_PER_KV)], axis=0)
            s = _dot_nt(k_ref[:, g * HEAD_DIM:(g + 1) * HEAD_DIM], q4)
            if use_bias:
                s = s + jnp.tile(bias_ref[...], (1, M // LANES))
            m_prev = m_sc[g]
            m_new = jnp.maximum(m_prev, jnp.max(s, axis=0, keepdims=True))
            alpha = jnp.exp2(m_prev - m_new)
            p = jnp.exp2(s - m_new)
            l_sc[g] = alpha * l_sc[g] + jnp.sum(p, axis=0, keepdims=True)
            acc_sc[g] = alpha * acc_sc[g] + _dot(vt_ref[g * HEAD_DIM:(g + 1) * HEAD_DIM, :], p.astype(BF16))
            m_sc[g] = m_new

    if nk == 1:
        step(True)
    else:
        pl.when(j == 0)(functools.partial(step, True))
        pl.when(j > 0)(functools.partial(step, False))

    @pl.when(j == nk - 1)
    def _():
        for g in range(N_KV_HEADS):
            o = (acc_sc[g] / l_sc[g]).T
            o_ref[:, g * QW:(g + 1) * QW] = jnp.concatenate(
                [o[h * tq:(h + 1) * tq] for h in range(Q_PER_KV)], axis=1).astype(o_ref.dtype)


def _attention(q, k, vt, bias, B, Np, row0, out_prev):
    tq = _pick(math.gcd(Np, row0), ROW_TILES)
    tk = Np if Np <= SINGLE_KEY_TILE_MAX else _pick(Np, KEY_TILES)
    nq, nk = Np // tq, Np // tk
    q0 = row0 // tq
    M = Q_PER_KV * tq
    return pl.pallas_call(
        functools.partial(_attn_kernel, tq=tq, nk=nk),
        grid=(B, nq, nk),
        in_specs=[
            pl.BlockSpec((tq, ATTN_WIDTH), lambda b, i, j: (q0 + b * nq + i, 0)),
            pl.BlockSpec((tk, KV_WIDTH), lambda b, i, j: (b * nk + j, 0)),
            pl.BlockSpec((KV_WIDTH, tk), lambda b, i, j: (0, b * nk + j)),
            pl.BlockSpec((tk, LANES), lambda b, i, j: (j, 0)),
            pl.BlockSpec(memory_space=pl.ANY),
        ],
        out_specs=pl.BlockSpec((tq, ATTN_WIDTH), lambda b, i, j: (q0 + b * nq + i, 0)),
        out_shape=jax.ShapeDtypeStruct(out_prev.shape, out_prev.dtype),
        input_output_aliases={4: 0},
        scratch_shapes=[
            pltpu.VMEM((N_KV_HEADS, 1, M), F32),
            pltpu.VMEM((N_KV_HEADS, 1, M), F32),
            pltpu.VMEM((N_KV_HEADS, HEAD_DIM, M), F32),
        ],
        compiler_params=_cparams(("parallel", "parallel", "arbitrary")),
        name="attention",
    )(q, k, vt, bias, out_prev)


V_SP, V_SN, V_DBASE, V_IBASE, V_KK, V_KA, V_RK, V_LNW, V_LNB = 0, 3, 6, 8, 10, 11, 12, 13, 14
VEC_ROWS = 16
HALO = 8


def _prep_kernel(r_ref, k_ref, v_ref, rp_ref, rn_ref, kp_ref, kn_ref, vp_ref, vn_ref, lora_ref,
                 dup0_ref, dup1_ref, iup0_ref, iup1_ref, gup_ref, vec_ref, ones_ref,
                 r_o, v_o, kk_o, ew0_o, kd0_o, b0_o, ew1_o, kd1_o, b1_o, bonus_o, g_o, *, tm, n_row_tiles):
    i = pl.program_id(0)
    has_prev = (i > 0).astype(F32)
    has_next = (i < n_row_tiles - 1).astype(F32)
    vec = vec_ref[...]
    row = lax.broadcasted_iota(jnp.int32, (tm, 1), 0)

    def shifted(x_ref, p_ref, n_ref, idx):
        x = x_ref[...]
        prev = jnp.where(row == 0, p_ref[HALO - 1:HALO, :] * has_prev, pltpu.roll(x, 1, 0))
        nxt = jnp.where(row == tm - 1, n_ref[0:1, :] * has_next, pltpu.roll(x, tm - 1, 0))
        return x + vec[V_SP + idx:V_SP + idx + 1] * (prev - x) + vec[V_SN + idx:V_SN + idx + 1] * (nxt - x)

    r = shifted(r_ref, rp_ref, rn_ref, 0)
    k = shifted(k_ref, kp_ref, kn_ref, 1)
    v = shifted(v_ref, vp_ref, vn_ref, 2)
    ones = ones_ref[...]

    lora = lora_ref[...]
    decay_h = jnp.tanh(lora[:, 0:DECAY_LORA])
    iclr_h = lora[:, DECAY_LORA:DECAY_LORA + ICLR_LORA]
    gate_h = _sigmoid(lora[:, DECAY_LORA + ICLR_LORA:LORA_PAD])

    kkr = k * vec[V_KK:V_KK + 1]
    kk = kkr / jnp.maximum(jnp.sqrt(_segsum(kkr * kkr, ones)), 1e-12)
    r_o[...] = r.astype(r_o.dtype)
    v_o[...] = v.astype(v_o.dtype)
    kk_o[...] = kk.astype(kk_o.dtype)
    bonus_o[...] = (_segsum(r * k * vec[V_RK:V_RK + 1], ones) * v).astype(bonus_o.dtype)
    g_o[...] = _dot(gate_h.astype(BF16), gup_ref[...]).astype(g_o.dtype)

    for d, (dup_ref, iup_ref, ew_o, kd_o, b_o) in enumerate(
            ((dup0_ref, iup0_ref, ew0_o, kd0_o, b0_o), (dup1_ref, iup1_ref, ew1_o, kd1_o, b1_o))):
        u = -(vec[V_DBASE + d:V_DBASE + d + 1] + _dot1(decay_h, dup_ref[...]))
        softplus = jnp.maximum(u, 0.0) + jnp.log(1.0 + jnp.exp(-jnp.abs(u)))
        ew_o[...] = jnp.exp(-softplus - 0.5)
        a = _sigmoid(vec[V_IBASE + d:V_IBASE + d + 1] + _dot1(iclr_h, iup_ref[...]))
        kd_o[...] = (k * (1.0 + (a - 1.0) * vec[V_KA:V_KA + 1])).astype(kd_o.dtype)
        b_o[...] = (kk * a).astype(b_o.dtype)


def _rwkv_prep(rkv, lora, dup, iup, gup, vec, ones_bd, tm, cn):
    R = rkv.shape[0]
    RW = rkv.shape[1] // 3
    J = RW // cn
    n_row_tiles = R // tm
    hb = tm // HALO
    last_hb = R // HALO - 1

    def main(c):
        return pl.BlockSpec((tm, cn), lambda i, j: (i, c * J + j))

    def prev(c):
        return pl.BlockSpec((HALO, cn), lambda i, j: (jnp.maximum(i * hb - 1, 0), c * J + j))

    def nxt(c):
        return pl.BlockSpec((HALO, cn), lambda i, j: (jnp.minimum((i + 1) * hb, last_hb), c * J + j))

    up = pl.BlockSpec((DECAY_LORA, cn), lambda i, j: (0, j))
    in_specs = [main(0), main(1), main(2), prev(0), nxt(0), prev(1), nxt(1), prev(2), nxt(2),
                pl.BlockSpec((tm, LORA_PAD), lambda i, j: (i, 0)),
                up, up, up, up,
                pl.BlockSpec((GATE_PAD, cn), lambda i, j: (0, j)),
                pl.BlockSpec((VEC_ROWS, cn), lambda i, j: (0, j)),
                pl.BlockSpec((cn, cn), lambda i, j: (0, 0))]
    out_spec = pl.BlockSpec((tm, cn), lambda i, j: (i, j))
    out_dtypes = [BF16, BF16, BF16, F32, BF16, BF16, F32, BF16, BF16, BF16, BF16]
    return pl.pallas_call(
        functools.partial(_prep_kernel, tm=tm, n_row_tiles=n_row_tiles),
        grid=(n_row_tiles, J),
        in_specs=in_specs,
        out_specs=[out_spec] * len(out_dtypes),
        out_shape=[jax.ShapeDtypeStruct((R, RW), dt) for dt in out_dtypes],
        compiler_params=_cparams(("parallel", "parallel")),
        name="rwkv_prep",
    )(rkv, rkv, rkv, rkv, rkv, rkv, rkv, rkv, rkv, lora, dup[0], dup[1], iup[0], iup[1], gup, vec, ones_bd)


def _wkv_kernel(first_ref, r_ref, v_ref, kk_ref, ew_ref, kd_ref, b_ref, tri_ref, y_ref, h_sc, *, reverse, npairs):
    c = pl.program_id(1)

    @pl.when(first_ref[c] == 1)
    def _():
        h_sc[...] = jnp.zeros(h_sc.shape, F32)

    L = CHUNK
    ew = ew_ref[...]
    cs = _dot_exact_lhs(tri_ref[...], ew)
    tot = cs[0:1] if reverse else cs[L - 1:L]
    e_neg = jnp.exp(-cs)
    e_prev = jnp.exp(ew - cs)
    e_pos = jnp.exp(cs)
    e_fin = jnp.exp(cs - tot)
    w_tot = jnp.exp(-tot)

    ri = lax.broadcasted_iota(jnp.int32, (PAIR, PAIR), 0)
    ci = lax.broadcasted_iota(jnp.int32, (PAIR, PAIR), 1)
    same = (ri // L) == (ci // L)
    t_i = ri % L
    s_i = ci % L
    if reverse:
        strict = same & (s_i > t_i)
        incl = same & (s_i >= t_i)
    else:
        strict = same & (s_i < t_i)
        incl = same & (s_i <= t_i)
    eye = ri == ci
    head0 = lax.broadcasted_iota(jnp.int32, (L, PAIR), 1) < RWKV_HEAD

    def stack(x):
        return jnp.concatenate([jnp.where(head0, x, 0.0), jnp.where(head0, 0.0, x)], axis=0)

    P2 = 2 * PAIR
    pairs = range(npairs)
    sls = [slice(p * PAIR, (p + 1) * PAIR) for p in pairs]

    r_st, a_st, v_st, bk2, kf_t, bf_t = [], [], [], [], [], []
    for sl in sls:
        kd = kd_ref[:, sl].astype(F32)
        b = b_ref[:, sl].astype(F32)
        bt = (b * e_pos[:, sl]).astype(BF16)
        kt = (kd * e_pos[:, sl]).astype(BF16)
        r_st.append(stack(r_ref[:, sl].astype(F32) * e_neg[:, sl]))
        a_st.append(stack(-kk_ref[:, sl].astype(F32) * e_prev[:, sl]))
        v_st.append(stack(v_ref[:, sl].astype(F32)).astype(BF16))
        bk2.append(jnp.concatenate([bt, bt, kt, kt], axis=0))
        kf_t.append(stack(kd * e_fin[:, sl]).T)
        bf_t.append(stack(b * e_fin[:, sl]).T)

    a_ab, a_ak, wlhs = [], [], []
    for p in pairs:
        sc = _dot_nt(jnp.concatenate([a_st[p], r_st[p]], axis=0).astype(BF16), bk2[p])
        a_ab.append(jnp.where(strict, sc[0:PAIR, 0:PAIR], 0.0))
        a_ak.append(jnp.where(strict, sc[0:PAIR, PAIR:P2], 0.0).astype(BF16))
        m_rb = jnp.where(incl, sc[PAIR:P2, 0:PAIR], 0.0)
        m_rk = jnp.where(incl, sc[PAIR:P2, PAIR:P2], 0.0)
        wlhs.append(jnp.concatenate([jnp.concatenate([m_rk, m_rb], axis=1),
                                     jnp.concatenate([kf_t[p], bf_t[p]], axis=1)], axis=0).astype(BF16))

    x = [jnp.concatenate([a_st[p], _dot(a_ak[p], v_st[p])], axis=1) for p in pairs]
    npow = a_ab
    n_sq = int(math.log2(L)) - 1
    for it in range(n_sq + 1):
        lhs = [npow[p].astype(BF16) for p in pairs]
        if it < n_sq:
            prod = [_dot(lhs[p], jnp.concatenate([lhs[p], x[p].astype(BF16)], axis=1)) for p in pairs]
            npow = [prod[p][:, 0:PAIR] for p in pairs]
            x = [x[p] + prod[p][:, PAIR:] for p in pairs]
        else:
            x = [x[p] + _dot(lhs[p], x[p].astype(BF16)) for p in pairs]

    zero = jnp.zeros((PAIR, PAIR), BF16)
    for p in pairs:
        sl = sls[p]
        rhs = jnp.concatenate([jnp.concatenate([zero, v_st[p]], axis=1), x[p].astype(BF16)], axis=0)
        w = _dot(wlhs[p], rhs)
        g_st = r_st[p] + w[0:PAIR, 0:PAIR]
        y0_st = w[0:PAIR, PAIR:P2]
        phi = jnp.where(eye, w_tot[:, sl], 0.0) + w[PAIR:P2, 0:PAIR]
        psi = w[PAIR:P2, PAIR:P2]
        out = _dot1(jnp.concatenate([g_st, phi], axis=0), h_sc[p])
        y_st = out[0:PAIR] + y0_st
        h_sc[p] = out[PAIR:P2] + psi
        y_ref[:, sl] = y_st[0:L] + y_st[L:2 * L]


def _wkv(first, r, v, kk, ew, kd, b, tri, *, reverse, npairs):
    R, RW = r.shape
    NC = R // CHUNK
    width = npairs * PAIR
    if reverse:
        blk = pl.BlockSpec((CHUNK, width), lambda g, c, f: (NC - 1 - c, g))
    else:
        blk = pl.BlockSpec((CHUNK, width), lambda g, c, f: (c, g))
    grid_spec = pltpu.PrefetchScalarGridSpec(
        num_scalar_prefetch=1,
        grid=(RW // width, NC),
        in_specs=[blk] * 6 + [pl.BlockSpec((CHUNK, CHUNK), lambda g, c, f: (0, 0))],
        out_specs=blk,
        scratch_shapes=[pltpu.VMEM((npairs, PAIR, PAIR), F32)],
    )
    return pl.pallas_call(
        functools.partial(_wkv_kernel, reverse=reverse, npairs=npairs),
        grid_spec=grid_spec,
        out_shape=jax.ShapeDtypeStruct((R, RW), F32),
        compiler_params=_cparams(("parallel", "arbitrary")),
        name="wkv_rev" if reverse else "wkv_fwd",
    )(first, r, v, kk, ew, kd, b, tri)


def _post_kernel(yf_ref, yb_ref, bonus_ref, g_ref, vec_ref, ones_ref, o_ref):
    ones = ones_ref[...]
    vec = vec_ref[...]
    y = yf_ref[...] + yb_ref[...]
    inv_n = 1.0 / RWKV_HEAD
    mu = _segsum(y, ones) * inv_n
    d = y - mu
    var = _segsum(d * d, ones) * inv_n
    yn = d * lax.rsqrt(var + GN_EPS) * vec[V_LNW:V_LNW + 1] + vec[V_LNB:V_LNB + 1]
    o_ref[...] = ((yn + bonus_ref[...]) * g_ref[...]).astype(o_ref.dtype)


def _rwkv_post(yf, yb, bonus, g, vec, ones_bd, tm, cn):
    R, RW = yf.shape
    blk = pl.BlockSpec((tm, cn), lambda i, j: (i, j))
    return pl.pallas_call(
        _post_kernel,
        grid=(R // tm, RW // cn),
        in_specs=[blk, blk, blk, blk, pl.BlockSpec((VEC_ROWS, cn), lambda i, j: (0, j)),
                  pl.BlockSpec((cn, cn), lambda i, j: (0, 0))],
        out_specs=blk,
        out_shape=jax.ShapeDtypeStruct((R, RW), BF16),
        compiler_params=_cparams(("parallel", "parallel")),
        name="rwkv_post",
    )(yf, yb, bonus, g, vec, ones_bd)


def _merge_kernel(a_ref, r_ref, wa_ref, wr_ref, ga_ref, gr_ref, o_ref, wa_sc, wr_sc):
    @pl.when(pl.program_id(1) == 0)
    def _():
        wa_sc[...] = wa_ref[...].astype(wa_sc.dtype)
        wr_sc[...] = wr_ref[...].astype(wr_sc.dtype)

    ya = _dot(a_ref[...], wa_sc[...])
    yr = _dot(r_ref[...], wr_sc[...])
    o_ref[...] = (ga_ref[...].astype(F32) * ya + gr_ref[...].astype(F32) * yr).astype(o_ref.dtype)


def _merge(attn, rwkv, wa, wr, gates, tm, tn):
    R = attn.shape[0]
    D = wa.shape[1]
    J = D // tn
    return pl.pallas_call(
        _merge_kernel,
        grid=(J, R // tm),
        in_specs=[
            pl.BlockSpec((tm, attn.shape[1]), lambda j, i: (i, 0)),
            pl.BlockSpec((tm, rwkv.shape[1]), lambda j, i: (i, 0)),
            pl.BlockSpec((wa.shape[0], tn), lambda j, i: (0, j)),
            pl.BlockSpec((wr.shape[0], tn), lambda j, i: (0, j)),
            pl.BlockSpec((tm, tn), lambda j, i: (i, j)),
            pl.BlockSpec((tm, tn), lambda j, i: (i, J + j)),
        ],
        out_specs=pl.BlockSpec((tm, tn), lambda j, i: (i, j)),
        out_shape=jax.ShapeDtypeStruct((R, D), BF16),
        scratch_shapes=[pltpu.VMEM((wa.shape[0], tn), BF16), pltpu.VMEM((wr.shape[0], tn), BF16)],
        compiler_params=_cparams(("parallel", "arbitrary")),
        name="merge",
    )(attn, rwkv, wa, wr, gates, gates)


def _router_kernel(x_ref, g_ref, w_ref, h_ref, aff_ref):
    x = x_ref[...]
    ms = jnp.mean(x * x, axis=-1, keepdims=True)
    h = x * lax.rsqrt(ms + NORM_EPS) * g_ref[...]
    h_ref[...] = h.astype(h_ref.dtype)
    logits = _dot3(h, w_ref[...])
    lane = lax.broadcasted_iota(jnp.int32, logits.shape, 1)
    logits = jnp.where(lane < N_EXPERTS, logits, NEG_BIG)
    e = jnp.exp(logits - jnp.max(logits, axis=-1, keepdims=True))
    aff_ref[...] = e / jnp.sum(e, axis=-1, keepdims=True)


def _router(x1, g, w_router_pad, tm):
    R, D = x1.shape
    return pl.pallas_call(
        _router_kernel,
        grid=(R // tm,),
        in_specs=[pl.BlockSpec((tm, D), lambda i: (i, 0)), pl.BlockSpec((1, D), lambda i: (0, 0)),
                  pl.BlockSpec((D, LANES), lambda i: (0, 0))],
        out_specs=[pl.BlockSpec((tm, D), lambda i: (i, 0)), pl.BlockSpec((tm, LANES), lambda i: (i, 0))],
        out_shape=[jax.ShapeDtypeStruct((R, D), BF16), jax.ShapeDtypeStruct((R, LANES), F32)],
        compiler_params=_cparams(("parallel",)),
        name="ffn_norm_router",
    )(x1, g.reshape(1, D), w_router_pad)


def _ffn_up_kernel(x_ref, wg_ref, wu_ref, o_ref):
    x = x_ref[...]
    hg = _dot(x, wg_ref[...].astype(BF16))
    hu = _dot(x, wu_ref[...].astype(BF16))
    o_ref[...] = (hg * _sigmoid(hg) * hu).astype(o_ref.dtype)


def _ffn_down_kernel(h_ref, wd_ref, gate_ref, o_ref):
    o_ref[...] = (_dot(h_ref[...], wd_ref[...].astype(BF16)) * gate_ref[...]).astype(o_ref.dtype)


def _expert_ffn(xs, gate, w_gate, w_up, w_down, tm, tf):
    E, D, F = w_gate.shape
    nt = xs.shape[0] // (E * tm)
    wspec = pl.BlockSpec((None, D, tf), lambda e, i, f: (e, 0, f))
    hid = pl.pallas_call(
        _ffn_up_kernel,
        grid=(E, nt, F // tf),
        in_specs=[pl.BlockSpec((tm, D), lambda e, i, f: (e * nt + i, 0)), wspec, wspec],
        out_specs=pl.BlockSpec((tm, tf), lambda e, i, f: (e * nt + i, f)),
        out_shape=jax.ShapeDtypeStruct((xs.shape[0], F), BF16),
        compiler_params=_cparams(("parallel", "parallel", "parallel")),
        name="ffn_up",
    )(xs, w_gate, w_up)
    td = _pick(D, (2 * tf, tf))
    return pl.pallas_call(
        _ffn_down_kernel,
        grid=(E, nt, D // td),
        in_specs=[pl.BlockSpec((tm, F), lambda e, i, f: (e * nt + i, 0)),
                  pl.BlockSpec((None, F, td), lambda e, i, f: (e, 0, f)),
                  pl.BlockSpec((tm, 1), lambda e, i, f: (e * nt + i, 0))],
        out_specs=pl.BlockSpec((tm, td), lambda e, i, f: (e * nt + i, f)),
        out_shape=jax.ShapeDtypeStruct((xs.shape[0], D), BF16),
        compiler_params=_cparams(("parallel", "parallel", "parallel")),
        name="ffn_down",
    )(hid, w_down, gate)


COMBINE_CHUNK = 64
SLOT_ALIGN = 16
COMBINE_SPAN = COMBINE_CHUNK - SLOT_ALIGN


def _combine_kernel(lo_ref, hi_ref, npass_ref, out_hbm, tok_hbm, y_ref, buf, tbuf, sem, *, ns, tmc):
    i = pl.program_id(0)
    E, CH = N_EXPERTS, COMBINE_CHUNK
    base = i * tmc
    lane = lax.broadcasted_iota(jnp.int32, (CH, tmc), 1)
    srow = lax.broadcasted_iota(jnp.int32, (CH, 1), 0)
    y_ref[...] = jnp.zeros(y_ref.shape, F32)

    def one_pass(p, carry):
        copies, bounds = [], []
        for e in range(E):
            a = lo_ref[i * E + e] + p * COMBINE_SPAN
            b = jnp.minimum(a + COMBINE_SPAN, hi_ref[i * E + e])
            start = pl.multiple_of(jnp.clip((a // SLOT_ALIGN) * SLOT_ALIGN, 0, ns - CH), SLOT_ALIGN)
            rows = pl.ds(e * CH, CH)
            pair = (pltpu.make_async_copy(out_hbm.at[pl.ds(start, CH), :], buf.at[rows, :], sem.at[0, e]),
                    pltpu.make_async_copy(tok_hbm.at[pl.ds(start, CH), :], tbuf.at[rows, :], sem.at[1, e]))
            pair[0].start()
            pair[1].start()
            copies.append(pair)
            bounds.append((start, a, b))
        blocks = []
        for e in range(E):
            copies[e][0].wait()
            copies[e][1].wait()
            start, a, b = bounds[e]
            slot = start + srow
            hit = (tbuf[pl.ds(e * CH, CH), :] - base == lane) & (slot >= a) & (slot < b)
            blocks.append(jnp.where(hit, 1.0, 0.0))
        onehot = jnp.concatenate(blocks, axis=0).T.astype(BF16)
        y_ref[...] += _dot(onehot, buf[...])
        return carry

    lax.fori_loop(0, npass_ref[i], one_pass, 0)


def _combine(out, tok, lo, hi, npass, R, tmc):
    ns, D = out.shape
    grid_spec = pltpu.PrefetchScalarGridSpec(
        num_scalar_prefetch=3,
        grid=(R // tmc,),
        in_specs=[pl.BlockSpec(memory_space=pl.ANY), pl.BlockSpec(memory_space=pl.ANY)],
        out_specs=pl.BlockSpec((tmc, D), lambda i, lo, hi, n: (i, 0)),
        scratch_shapes=[pltpu.VMEM((N_EXPERTS * COMBINE_CHUNK, D), BF16),
                        pltpu.VMEM((N_EXPERTS * COMBINE_CHUNK, 1), jnp.int32),
                        pltpu.SemaphoreType.DMA((2, N_EXPERTS))],
    )
    return pl.pallas_call(
        functools.partial(_combine_kernel, ns=ns, tmc=tmc),
        grid_spec=grid_spec,
        out_shape=jax.ShapeDtypeStruct((R, D), F32),
        compiler_params=_cparams(("arbitrary",)),
        name="expert_combine",
    )(lo, hi, npass, out, tok)


def _final_kernel(x_ref, y_ref, g_ref, o_ref):
    x = x_ref[...] + y_ref[...]
    ms = jnp.mean(x * x, axis=-1, keepdims=True)
    o_ref[...] = x * lax.rsqrt(ms + NORM_EPS) * g_ref[...]


def _final(x1, y, g, B, Np, off):
    D = x1.shape[1]
    nb = Np // LANES
    base = off // LANES
    blk = pl.BlockSpec((LANES, D), lambda b, j: (base + b * nb + 1 + j, 0))
    return pl.pallas_call(
        _final_kernel,
        grid=(B, nb - 1),
        in_specs=[blk, blk, pl.BlockSpec((1, D), lambda b, j: (0, 0))],
        out_specs=pl.BlockSpec((None, LANES, D), lambda b, j: (b, j, 0)),
        out_shape=jax.ShapeDtypeStruct((B, Np - LANES, D), F32),
        compiler_params=_cparams(("parallel", "parallel")),
        name="final_norm",
    )(x1, y, g.reshape(1, D))


def _rope_tables(S, B):
    rows = S // GRID_W
    row_ids = jnp.repeat(jnp.arange(rows, dtype=F32), GRID_W)
    col_ids = jnp.tile(jnp.arange(GRID_W, dtype=F32), rows)
    half = HEAD_DIM // 2
    inv_freq = 1.0 / (ROPE_THETA ** (jnp.arange(0, half, 2, dtype=F32) / half))
    ang = jnp.concatenate([row_ids[:, None] * inv_freq, col_ids[:, None] * inv_freq], axis=-1)
    ang = jnp.concatenate([jnp.zeros((LANES, half), F32), ang], axis=0)
    c, s = jnp.cos(ang), jnp.sin(ang)
    return (jnp.tile(jnp.concatenate([c, c], axis=-1), (B, 1)),
            jnp.tile(jnp.concatenate([-s, s], axis=-1), (B, 1)))


def _key_bias(Np):
    col = np.where(np.arange(Np) < FRONT, NEG_BIG, 0.0).astype(np.float32)
    return jnp.asarray(np.repeat(col[:, None], LANES, axis=1))


def _seq_flags(groups):
    fwd = []
    for B, Np in groups:
        nc = Np // CHUNK
        for _ in range(B):
            fwd += [1] + [0] * (nc - 1)
    fwd = np.asarray(fwd, np.int32)
    last = np.roll(fwd, -1)
    return jnp.asarray(fwd), jnp.asarray(last[::-1].copy())


def kernel(x_prompt, x_sample, meta_tokens, norm_mix, w_in, q_norm, k_norm, shift_prev, shift_next, decay_up, decay_base, iclr_up, iclr_base, gate_up, k_k, k_a, r_k, ln_x_w, ln_x_b, w_branch_attn, w_branch_rwkv, w_out, norm_ffn, w_router, w_gate, w_up, w_down, norm_final):
    assert norm_mix.shape[0] == 1, "one layer"
    D = x_prompt.shape[-1]
    RW = D // 2
    groups = [(x.shape[0], x.shape[1] + LANES) for x in (x_prompt, x_sample)]
    seqs = [x.shape[1] for x in (x_prompt, x_sample)]
    assert all(s % LANES == 0 for s in seqs)
    offs = [0, groups[0][0] * groups[0][1]]
    r_used = offs[1] + groups[1][0] * groups[1][1]
    tm = MATMUL_ROW_TILE if r_used >= 8 * MATMUL_ROW_TILE else LANES
    tme = tm // 3 if tm % 3 == 0 else tm
    tmr = tm // 2 if tm % 256 == 0 else tm
    R = -(-r_used // tm) * tm
    tail = R - r_used

    x, h = _pack_norm(x_prompt, x_sample, meta_tokens, norm_mix[0], R)
    tabs = [_rope_tables(S, B) for S, (B, _) in zip(seqs, groups)]
    cos_t = jnp.concatenate([t[0] for t in tabs] + [jnp.zeros((tail, HEAD_DIM), F32)], axis=0)
    sin_t = jnp.concatenate([t[1] for t in tabs] + [jnp.zeros((tail, HEAD_DIM), F32)], axis=0)

    w_in_t = jnp.transpose(w_in[0])
    n_in = w_in_t.shape[0]

    def in_proj(col0, n_out, **kw):
        tn = _pick(n_out, (512, 384, 256, 128))
        if col0 % tn == 0 and col0 + n_out <= n_in:
            return _matmul(h, w_in_t, col0=col0, n_out=n_out, tn=tn, tm=tm, w_transposed=True, **kw)
        w_rows = w_in_t[col0:min(col0 + n_out, n_in)]
        w_rows = jnp.pad(w_rows, ((0, n_out - w_rows.shape[0]), (0, 0)))
        return _matmul(h, w_rows, tn=tn, tm=tm, w_transposed=True, **kw)

    rope_specs = lambda g: [(g.reshape(1, HEAD_DIM), (1, HEAD_DIM), lambda i, j: (0, 0)),
                            (cos_t, (tm, HEAD_DIM), lambda i, j: (i, 0)),
                            (sin_t, (tm, HEAD_DIM), lambda i, j: (i, 0))]
    q_scale = HEAD_DIM ** -0.5 * math.log2(math.e)
    c_k = ATTN_WIDTH
    c_v = c_k + KV_WIDTH
    c_rkv = c_v + KV_WIDTH
    c_lora = c_rkv + 3 * RW
    c_merge = c_lora + DECAY_LORA + ICLR_LORA + GATE_LORA
    q = in_proj(0, ATTN_WIDTH, out_dtype=BF16, name="proj_q",
                epilogue=functools.partial(_qk_epilogue, scale=q_scale), extras=rope_specs(q_norm[0]))
    k = in_proj(c_k, KV_WIDTH, out_dtype=BF16, name="proj_k",
                epilogue=functools.partial(_qk_epilogue, scale=1.0), extras=rope_specs(k_norm[0]))
    v = in_proj(c_v, KV_WIDTH, out_dtype=BF16, name="proj_v")
    rkv = in_proj(c_rkv, 3 * RW, out_dtype=F32, name="proj_rkv")
    lora = in_proj(c_lora, LORA_PAD, out_dtype=F32, name="proj_lora")
    gates = in_proj(c_merge, 2 * D, out_dtype=BF16, name="proj_merge_gates", epilogue=_sigmoid_epilogue)

    attn = jnp.zeros((R, ATTN_WIDTH), BF16)
    for (B, Np), off in zip(groups, offs):
        sl = slice(off, off + B * Np)
        attn = _attention(q, k[sl], v[sl].T, _key_bias(Np), B, Np, off, attn)

    cn = _pick(RW, (256, 128))
    vec = jnp.concatenate([shift_prev[0], shift_next[0], decay_base[0], iclr_base[0], k_k, k_a, r_k,
                           ln_x_w, ln_x_b, jnp.zeros((VEC_ROWS - 15, RW), F32)], axis=0).astype(F32)
    hid_idx = np.arange(cn) // RWKV_HEAD
    ones_bd = jnp.asarray((hid_idx[:, None] == hid_idx[None, :]).astype(np.float32)).astype(BF16)
    gup = jnp.pad(gate_up[0], ((0, GATE_PAD - GATE_LORA), (0, 0))).astype(BF16)
    r_s, v_s, kk, ew0, kd0, b0, ew1, kd1, b1, bonus, g_rwkv = _rwkv_prep(
        rkv, lora, decay_up[0], iclr_up[0], gup, vec, ones_bd, tmr, cn)
    first_fwd, first_rev = _seq_flags(groups + ([(1, tail)] if tail else []))
    t_idx = np.arange(CHUNK)
    tri_f = jnp.asarray((t_idx[None, :] <= t_idx[:, None]).astype(np.float32)).astype(BF16)
    tri_r = jnp.asarray((t_idx[None, :] >= t_idx[:, None]).astype(np.float32)).astype(BF16)
    npairs = _pick(RW // PAIR, (16, 8, 4, 2, 1))
    y_f = _wkv(first_fwd, r_s, v_s, kk, ew0, kd0, b0, tri_f, reverse=False, npairs=npairs)
    y_b = _wkv(first_rev, r_s, v_s, kk, ew1, kd1, b1, tri_r, reverse=True, npairs=npairs)
    rwkv = _rwkv_post(y_f, y_b, bonus, g_rwkv, vec, ones_bd, tmr, cn)

    tn_d = _pick(D, (512, 384, 256, 128))
    merged = _merge(attn, rwkv, w_branch_attn[0], w_branch_rwkv[0], gates, tm, tn_d)
    x1 = _matmul(merged, w_out, tm=tm, tn=tn_d, out_dtype=F32, name="out_proj",
                 epilogue=_residual_epilogue, extras=[(x, (tm, tn_d), lambda i, j: (i, j))])

    w_router_pad = jnp.pad(w_router[0], ((0, 0), (0, LANES - N_EXPERTS)))
    h2, aff = _router(x1, norm_ffn[0], w_router_pad, tme)
    aff = aff[:, :N_EXPERTS]
    idx_parts, gate_parts = [], []
    for (B, Np), off in zip(groups, offs):
        valid = jnp.asarray((np.arange(B * Np) % Np) >= FRONT)
        a_g = jnp.where(valid[:, None], aff[off:off + B * Np], -1.0)
        cap = CAPACITY_FACTOR * (B * (Np - FRONT)) // N_EXPERTS
        gate_g, idx_g = lax.top_k(a_g.T, cap)
        idx_g, gate_g = lax.sort((idx_g, gate_g), dimension=1, num_keys=1)
        idx_parts.append(idx_g + off)
        gate_parts.append(gate_g)
    idx = jnp.concatenate(idx_parts, axis=1)
    gate = jnp.concatenate(gate_parts, axis=1)
    c_tot = idx.shape[1]
    nt = -(-c_tot // EXPERT_ROW_TILE_MAX)
    tmx = -(-(-(-c_tot // nt)) // SLOT_ALIGN) * SLOT_ALIGN
    ct = nt * tmx
    tok = jnp.pad(idx, ((0, 0), (0, ct - c_tot)), constant_values=R)
    gate = jnp.pad(gate, ((0, 0), (0, ct - c_tot))).reshape(-1, 1)
    xs = h2.at[jnp.where(tok < R, tok, 0).reshape(-1)].get(mode="promise_in_bounds")
    tf = _pick(D, (256, 128))
    out = _expert_ffn(xs, gate, w_gate[0], w_up[0], w_down[0], tmx, tf)

    tmc = _pick(R, (256, LANES))
    edges = jnp.arange(R // tmc + 1, dtype=jnp.int32) * tmc
    pos = jnp.sum((tok[:, None, :] < edges[None, :, None]).astype(jnp.int32), axis=-1)
    pos = pos + (jnp.arange(N_EXPERTS, dtype=jnp.int32) * ct)[:, None]
    lo, hi = pos[:, :-1].T, pos[:, 1:].T
    npass = jnp.max(-(-(hi - lo) // COMBINE_SPAN), axis=1).astype(jnp.int32)
    y = _combine(out, tok.reshape(-1, 1).astype(jnp.int32), lo.reshape(-1), hi.reshape(-1), npass, R, tmc)

    return tuple(_final(x1, y, norm_final, B, Np, off) for (B, Np), off in zip(groups, offs))
```
